```python
import math
import jax
import jax.numpy as jnp
from jax import lax
import numpy as np

D_MODEL = 4096
BATCH = 4
SEQ = 2048
DEPTH = 4
DEC_BATCH = 8
DEC_SEQ = 8
PAST_LEN = 8192
PAGE_SIZE = 128

N_MIXERS = 3
NORM_EPS = 1e-6
D_FF = ((8 * D_MODEL // 3 + 255) // 256) * 256

SSD_D_INNER = 2 * D_MODEL
SSD_HEAD_DIM = 64
SSD_HEADS = SSD_D_INNER // SSD_HEAD_DIM
SSD_STATE = 128
SSD_GROUPS = 8
SSD_CONV = 4
SSD_CHUNK = 128
SSD_CONV_DIM = SSD_D_INNER + 2 * SSD_GROUPS * SSD_STATE

LRU_WIDTH = D_MODEL
LRU_BLOCKS = 16
LRU_BLOCK_DIM = LRU_WIDTH // LRU_BLOCKS
LRU_CONV = 4
LRU_C = 8.0

NSA_HEADS = 32
NSA_GROUPS = 4
NSA_HPG = NSA_HEADS // NSA_GROUPS
NSA_HEAD_DIM = D_MODEL // NSA_HEADS
ATTN_SCALE = NSA_HEAD_DIM ** -0.5
CMP_LEN = 32
CMP_STRIDE = 16
SEL_LEN = 64
SEL_TOPK = 16
WINDOW = 512
SEL_QBLOCK = 64
WIN_QBLOCK = 128
FORCE_BONUS = 1e4
NEG_INF = -1e30
REL_BUCKETS = 32
REL_MAX_DIST = 128

N_SSD_LAYERS = (DEPTH + 2) // 3
N_LRU_LAYERS = (DEPTH + 1) // 3
N_NSA_LAYERS = DEPTH // 3

kernel_name = "hybrid_ssd_rglru_nsa_macaron_step"

F32 = jnp.float32


def rms_norm(x, w):
    xf = x.astype(F32)
    y = xf * lax.rsqrt(jnp.mean(xf * xf, axis=-1, keepdims=True) + NORM_EPS)
    return (y * w.astype(F32)).astype(x.dtype)


def swiglu(x, w_in, w_out):
    g, u = jnp.split(x @ w_in, 2, axis=-1)
    return (jax.nn.silu(g) * u) @ w_out


def causal_dwconv(x, buf, w, b):
    width, L = w.shape[0], x.shape[1]
    xp = jnp.concatenate([buf.astype(x.dtype), x], axis=1)
    y = b + sum(xp[:, k:k + L] * w[k] for k in range(width))
    return y, xp[:, L:]


def masked_softmax(logits, mask):
    p = jax.nn.softmax(jnp.where(mask, logits, NEG_INF), axis=-1)
    return p * mask


def rel_bucket(dist):
    n = jnp.maximum(dist, 0)
    exact = REL_BUCKETS // 2
    log_ratio = jnp.log(jnp.maximum(n, 1).astype(F32) / exact) / math.log(REL_MAX_DIST / exact)
    large = jnp.minimum(exact + (log_ratio * (REL_BUCKETS - exact)).astype(jnp.int32), REL_BUCKETS - 1)
    return jnp.where(n < exact, n, large)


def sweep_query_blocks(fn, T, cap):
    qb = cap if T % cap == 0 else T
    out = lax.map(lambda i: fn(i * qb, qb), jnp.arange(T // qb))
    return jnp.moveaxis(out, 0, 1).reshape((out.shape[1], T) + out.shape[3:])


def ssd_chunked_scan(x, dt, a, bm, cm, h0):
    b, L, H, P = x.shape
    G, N = bm.shape[2], bm.shape[3]
    E = H // G
    Q = min(SSD_CHUNK, L)
    pad = (-L) % Q
    x, dt, bm, cm = (jnp.pad(t.astype(F32), [(0, 0), (0, pad)] + [(0, 0)] * (t.ndim - 2)) for t in (x, dt, bm, cm))
    c = (L + pad) // Q
    xdt = (x * dt[..., None]).reshape(b, c, Q, G, E, P)
    da = (dt * a.astype(F32)).reshape(b, c, Q, G, E)
    bm = bm.reshape(b, c, Q, G, N)
    cm = cm.reshape(b, c, Q, G, N)
    cs = jnp.cumsum(da, axis=2)
    causal = jnp.tril(jnp.ones((Q, Q), bool))[:, :, None, None]
    seg = cs[:, :, :, None] - cs[:, :, None, :]
    decay = jnp.exp(jnp.where(causal, seg, NEG_INF))
    cb = jnp.einsum('bclgn,bcsgn->bclsg', cm, bm)
    y_diag = jnp.einsum('bclsg,bclsge,bcsgep->bclgep', cb, decay, xdt)
    to_end = jnp.exp(cs[:, :, -1:] - cs)
    chunk_states = jnp.einsum('bcsgn,bcsge,bcsgep->bcgepn', bm, to_end, xdt)
    chunk_decay = jnp.exp(cs[:, :, -1])

    def step(h, inp):
        st, dec = inp
        return dec[..., None, None] * h + st, h

    h_last, h_in = lax.scan(step, h0.astype(F32).reshape(b, G, E, P, N),
                            (jnp.moveaxis(chunk_states, 1, 0), jnp.moveaxis(chunk_decay, 1, 0)))
    h_in = jnp.moveaxis(h_in, 0, 1)
    y_off = jnp.einsum('bclgn,bcgepn,bclge->bclgep', cm, h_in, jnp.exp(cs))
    y = (y_diag + y_off).reshape(b, c * Q, H, P)[:, :L]
    return y, h_last.reshape(b, H, P, N)


def ssd_mixer(h, conv_buf, ssm_state, w_in, conv_w, conv_b, dt_bias, a_log, d_skip, norm_w, w_out):
    b, L, _ = h.shape
    proj = h @ w_in
    z = proj[..., :SSD_D_INNER]
    xbc = proj[..., SSD_D_INNER:SSD_D_INNER + SSD_CONV_DIM]
    dt_raw = proj[..., SSD_D_INNER + SSD_CONV_DIM:]
    xbc, new_buf = causal_dwconv(xbc, conv_buf, conv_w, conv_b)
    xbc = jax.nn.silu(xbc)
    gn = SSD_GROUPS * SSD_STATE
    xs = xbc[..., :SSD_D_INNER].reshape(b, L, SSD_HEADS, SSD_HEAD_DIM)
    bm = xbc[..., SSD_D_INNER:SSD_D_INNER + gn].reshape(b, L, SSD_GROUPS, SSD_STATE)
    cm = xbc[..., SSD_D_INNER + gn:].reshape(b, L, SSD_GROUPS, SSD_STATE)
    dt = jax.nn.softplus(dt_raw.astype(F32) + dt_bias.astype(F32))
    a = -jnp.exp(a_log.astype(F32))
    y, new_state = ssd_chunked_scan(xs, dt, a, bm, cm, ssm_state)
    y = y + d_skip.astype(F32)[:, None] * xs.astype(F32)
    y = y.reshape(b, L, SSD_D_INNER) * jax.nn.silu(z.astype(F32))
    y = rms_norm(y.reshape(b, L, SSD_GROUPS, SSD_D_INNER // SSD_GROUPS),
                 norm_w.reshape(SSD_GROUPS, -1)).reshape(b, L, SSD_D_INNER)
    return y.astype(h.dtype) @ w_out, new_buf, new_state


def rglru_mixer(h, conv_buf, h_state, w_in, conv_w, conv_b, w_r, b_r, w_i, b_i, lam, w_out):
    b, L, _ = h.shape
    gate, xb = jnp.split(h @ w_in, 2, axis=-1)
    xc, new_buf = causal_dwconv(xb, conv_buf, conv_w, conv_b)
    xblk = xc.reshape(b, L, LRU_BLOCKS, LRU_BLOCK_DIM)
    r = jax.nn.sigmoid(jnp.einsum('blki,kij->blkj', xblk, w_r) + b_r).reshape(b, L, LRU_WIDTH)
    ig = jax.nn.sigmoid(jnp.einsum('blki,kij->blkj', xblk, w_i) + b_i).reshape(b, L, LRU_WIDTH)
    log_a = (-LRU_C * r.astype(F32)) * jax.nn.softplus(-lam.astype(F32))
    a = jnp.exp(log_a)
    u = jnp.sqrt(-jnp.expm1(2.0 * log_a)) * ig.astype(F32) * xc.astype(F32)
    u = u.at[:, 0].add(a[:, 0] * h_state.astype(F32))

    def combine(left, right):
        a_l, b_l = left
        a_r, b_r2 = right
        return a_l * a_r, a_r * b_l + b_r2

    _, hs = lax.associative_scan(combine, (a, u), axis=1)
    y = hs * jax.nn.gelu(gate.astype(F32))
    return y.astype(h.dtype) @ w_out, new_buf, hs[:, -1]


def compress_blocks(rows, pe, w1, w2):
    n = (rows.shape[1] - CMP_LEN) // CMP_STRIDE + 1
    idx = jnp.arange(n)[:, None] * CMP_STRIDE + jnp.arange(CMP_LEN)[None, :]
    blk = rows[:, idx] + pe[:, None, :]
    hid = jax.nn.silu(jnp.einsum('bnlgd,lde->bnge', blk, w1))
    return jnp.einsum('bnge,ef->bngf', hid, w2)


def cmp_attend(q, q_pos, kc, vc, tab):
    n = kc.shape[1]
    blk_end = jnp.arange(n) * CMP_STRIDE + CMP_LEN - 1
    dist = q_pos[:, None] - blk_end[None, :]
    mask = (dist >= 0)[None, :, None, None, :]
    bias = jnp.transpose(tab[rel_bucket(dist)], (0, 2, 3, 1))[None]
    logits = jnp.einsum('bqged,bngd->bqgen', q, kc).astype(F32) * ATTN_SCALE + bias
    p = masked_softmax(logits, mask)
    o = jnp.einsum('bqgen,bngd->bqged', p.astype(vc.dtype), vc)
    return o, p.sum(axis=3)


def sel_attend(q, q_pos, blk_idx, ks, vs, tab):
    b, Qb, G = blk_idx.shape[:3]
    tok = (blk_idx[..., None] * SEL_LEN + jnp.arange(SEL_LEN)).reshape(b, Qb, G, -1)
    bi = jnp.arange(b)[:, None, None, None]
    gi = jnp.arange(G)[None, None, :, None]
    k_g = ks[bi, tok, gi]
    v_g = vs[bi, tok, gi]
    dist = q_pos[None, :, None, None] - tok
    mask = (dist >= 0)[:, :, :, None, :]
    bias = jnp.moveaxis(tab[rel_bucket(dist), gi], -1, 3)
    logits = jnp.einsum('bqged,bqgsd->bqges', q, k_g).astype(F32) * ATTN_SCALE + bias
    p = masked_softmax(logits, mask)
    return jnp.einsum('bqges,bqgsd->bqged', p.astype(v_g.dtype), v_g)


def win_attend(q, q_pos, k_band, v_band, k_pos, tab):
    dist = q_pos[:, None] - k_pos[None, :]
    mask = ((dist >= 0) & (dist < WINDOW) & (k_pos[None, :] >= 0))[None, :, None, None, :]
    bias = jnp.transpose(tab[rel_bucket(dist)], (0, 2, 3, 1))[None]
    logits = jnp.einsum('bqged,bsgd->bqges', q, k_band).astype(F32) * ATTN_SCALE + bias
    p = masked_softmax(logits, mask)
    return jnp.einsum('bqges,bsgd->bqged', p.astype(v_band.dtype), v_band)


def nsa_mixer(h, past_rows, win_buf, q_pos0, w_in, q_norm_w, k_norm_w, cmp_pe, cmp_w1, cmp_w2, table, w_out):
    b, T, _ = h.shape
    G, E, d = NSA_GROUPS, NSA_HPG, NSA_HEAD_DIM
    n_q, n_kv = NSA_HEADS * d, 6 * G * d
    proj = h @ w_in
    q = rms_norm(proj[..., :n_q].reshape(b, T, G, E, d), q_norm_w)
    kv_new = proj[..., n_q:n_q + n_kv].reshape(b, T, 6, G, d)
    gates = jax.nn.sigmoid(proj[..., n_q + n_kv:].astype(F32)).reshape(b, T, G, E, 3)
    tab = table.reshape(REL_BUCKETS, G, E)
    q_pos = q_pos0 + jnp.arange(T)
    rows = jnp.concatenate([past_rows.astype(kv_new.dtype), kv_new[:, :, :4]], axis=1)
    Tk = rows.shape[1]
    kc = rms_norm(compress_blocks(rows[:, :, 0], cmp_pe[0], cmp_w1[0], cmp_w2[0]), k_norm_w[0])
    vc = compress_blocks(rows[:, :, 1], cmp_pe[1], cmp_w1[1], cmp_w2[1])
    o_cmp, p_grp = cmp_attend(q, q_pos, kc, vc, tab)
    n_cmp = kc.shape[1]
    n_sel = -(-Tk // SEL_LEN)
    c_start = jnp.arange(n_cmp) * CMP_STRIDE
    s_start = jnp.arange(n_sel) * SEL_LEN
    cover = ((c_start[:, None] < s_start[None, :] + SEL_LEN) &
             (c_start[:, None] + CMP_LEN > s_start[None, :])).astype(F32)
    p_sel = jnp.einsum('bqgn,nj->bqgj', p_grp, cover)
    cur = q_pos // SEL_LEN
    jj = jnp.arange(n_sel)[None, :]
    visible = s_start[None, :] <= q_pos[:, None]
    forced = (jj == 0) | (jj == cur[:, None]) | (jj == cur[:, None] - 1)
    score = jnp.where(visible[None, :, None, :], p_sel + FORCE_BONUS * forced[None, :, None, :].astype(F32), NEG_INF)
    _, blk_idx = lax.top_k(score, min(SEL_TOPK, n_sel))
    pad = n_sel * SEL_LEN - Tk
    ks = jnp.pad(rms_norm(rows[:, :, 2], k_norm_w[1]), ((0, 0), (0, pad), (0, 0), (0, 0)))
    vs = jnp.pad(rows[:, :, 3], ((0, 0), (0, pad), (0, 0), (0, 0)))

    def sel_block(start, qb):
        return sel_attend(lax.dynamic_slice_in_dim(q, start, qb, 1),
                          lax.dynamic_slice_in_dim(q_pos, start, qb, 0),
                          lax.dynamic_slice_in_dim(blk_idx, start, qb, 1), ks, vs, tab)

    o_sel = sweep_query_blocks(sel_block, T, SEL_QBLOCK)
    wb = win_buf.shape[1]
    band = jnp.concatenate([jnp.zeros((b, WINDOW - wb, 2, G, d), kv_new.dtype),
                            win_buf.astype(kv_new.dtype), kv_new[:, :, 4:]], axis=1)
    kw = rms_norm(band[:, :, 0], k_norm_w[2])
    vw = band[:, :, 1]

    def win_block(start, qb):
        k_pos = q_pos0 - WINDOW + start + jnp.arange(WINDOW + qb)
        return win_attend(lax.dynamic_slice_in_dim(q, start, qb, 1), q_pos0 + start + jnp.arange(qb),
                          lax.dynamic_slice_in_dim(kw, start, WINDOW + qb, 1),
                          lax.dynamic_slice_in_dim(vw, start, WINDOW + qb, 1), k_pos, tab)

    o_win = sweep_query_blocks(win_block, T, WIN_QBLOCK)
    o = gates[..., 0:1] * o_cmp + gates[..., 1:2] * o_sel + gates[..., 2:3] * o_win
    y = o.reshape(b, T, n_q).astype(h.dtype) @ w_out
    new_win = jnp.concatenate([win_buf.astype(kv_new.dtype), kv_new[:, :, 4:]], axis=1)[:, -wb:]
    return y, kv_new[:, :, :4], new_win


def trunk(x, q_pos0, ssd_h, ssd_buf, lru_h, lru_buf, nsa_past, nsa_win, prm):
    n_ssd_h, n_ssd_buf, n_lru_h, n_lru_buf, n_rows, n_win = [], [], [], [], [], []
    for i in range(DEPTH):
        kind, j = i % N_MIXERS, i // N_MIXERS
        x = x + 0.5 * swiglu(rms_norm(x, prm['ffn_norm'][i, 0]), prm['w_ffn_in'][i, 0], prm['w_ffn_out'][i, 0])
        hn = rms_norm(x, prm['mix_norm'][i])
        if kind == 0:
            m, buf, st = ssd_mixer(hn, ssd_buf[j], ssd_h[j], prm['ssd_w_in'][j], prm['ssd_conv_w'][j],
                                   prm['ssd_conv_b'][j], prm['ssd_dt_bias'][j], prm['ssd_a_log'][j],
                                   prm['ssd_d'][j], prm['ssd_norm'][j], prm['ssd_w_out'][j])
            n_ssd_buf.append(buf)
            n_ssd_h.append(st)
        elif kind == 1:
            m, buf, st = rglru_mixer(hn, lru_buf[j], lru_h[j], prm['lru_w_in'][j], prm['lru_conv_w'][j],
                                     prm['lru_conv_b'][j], prm['lru_w_r'][j], prm['lru_b_r'][j],
                                     prm['lru_w_i'][j], prm['lru_b_i'][j], prm['lru_lambda'][j], prm['lru_w_out'][j])
            n_lru_buf.append(buf)
            n_lru_h.append(st)
        else:
            m, rows, win = nsa_mixer(hn, nsa_past[j], nsa_win[j], q_pos0, prm['nsa_w_in'][j], prm['nsa_q_norm'][j],
                                     prm['nsa_k_norm'][j], prm['nsa_cmp_pe'][j], prm['nsa_cmp_w1'][j],
                                     prm['nsa_cmp_w2'][j], prm['rel_bias_table'], prm['nsa_w_out'][j])
            n_rows.append(rows)
            n_win.append(win)
        x = x + m
        x = x + 0.5 * swiglu(rms_norm(x, prm['ffn_norm'][i, 1]), prm['w_ffn_in'][i, 1], prm['w_ffn_out'][i, 1])
    return (x, jnp.stack(n_ssd_h), jnp.stack(n_ssd_buf), jnp.stack(n_lru_h), jnp.stack(n_lru_buf),
            jnp.stack(n_rows), jnp.stack(n_win))


def setup_inputs(seed: int = 0) -> dict:
    key = jax.random.key(seed)
    keys = iter(jax.random.split(key, 64))

    def nrm(shape, scale):
        return jax.random.normal(next(keys), shape, F32) * scale

    n_pages = PAST_LEN // PAGE_SIZE
    n_used = DEC_BATCH * n_pages
    n_pool = n_used + (n_used + 3) // 4
    wb = min(WINDOW, PAST_LEN)
    G, d = NSA_GROUPS, NSA_HEAD_DIM
    nsa_in_cols = NSA_HEADS * d + 6 * G * d + 3 * NSA_HEADS
    ssd_in_cols = SSD_D_INNER + SSD_CONV_DIM + SSD_HEADS

    inp = {}
    inp['x_prompt'] = nrm((BATCH, SEQ, D_MODEL), 1.0)
    inp['x_sample'] = nrm((DEC_BATCH, DEC_SEQ, D_MODEL), 1.0)
    inp['state_ssd'] = nrm((N_SSD_LAYERS, DEC_BATCH, SSD_HEADS, SSD_HEAD_DIM, SSD_STATE), 0.5)
    inp['state_ssd_conv'] = nrm((N_SSD_LAYERS, DEC_BATCH, SSD_CONV - 1, SSD_CONV_DIM), 1.0)
    inp['state_lru'] = nrm((N_LRU_LAYERS, DEC_BATCH, LRU_WIDTH), 0.5)
    inp['state_lru_conv'] = nrm((N_LRU_LAYERS, DEC_BATCH, LRU_CONV - 1, LRU_WIDTH), 1.0)
    inp['cache_nsa_kv'] = nrm((N_NSA_LAYERS, n_pool, PAGE_SIZE, 4, G, d), 1.0)
    inp['cache_nsa_win'] = nrm((N_NSA_LAYERS, DEC_BATCH, wb, 2, G, d), 1.0)
    inp['page_table'] = jax.random.permutation(next(keys), n_pool)[:n_used].reshape(DEC_BATCH, n_pages).astype(jnp.int32)
    inp['ffn_norm'] = 1.0 + nrm((DEPTH, 2, D_MODEL), 0.02)
    inp['w_ffn_in'] = nrm((DEPTH, 2, D_MODEL, 2 * D_FF), D_MODEL ** -0.5)
    inp['w_ffn_out'] = nrm((DEPTH, 2, D_FF, D_MODEL), D_FF ** -0.5)
    inp['mix_norm'] = 1.0 + nrm((DEPTH, D_MODEL), 0.02)
    inp['ssd_w_in'] = nrm((N_SSD_LAYERS, D_MODEL, ssd_in_cols), D_MODEL ** -0.5)
    inp['ssd_conv_w'] = nrm((N_SSD_LAYERS, SSD_CONV, SSD_CONV_DIM), SSD_CONV ** -0.5)
    inp['ssd_conv_b'] = nrm((N_SSD_LAYERS, SSD_CONV_DIM), 0.02)
    dt0 = jnp.exp(jax.random.uniform(next(keys), (N_SSD_LAYERS, SSD_HEADS), F32, math.log(1e-3), math.log(1e-1)))
    inp['ssd_dt_bias'] = dt0 + jnp.log(-jnp.expm1(-dt0))
    inp['ssd_a_log'] = jnp.log(jax.random.uniform(next(keys), (N_SSD_LAYERS, SSD_HEADS), F32, 1.0, 16.0))
    inp['ssd_d'] = 1.0 + nrm((N_SSD_LAYERS, SSD_HEADS), 0.1)
    inp['ssd_norm'] = 1.0 + nrm((N_SSD_LAYERS, SSD_D_INNER), 0.02)
    inp['ssd_w_out'] = nrm((N_SSD_LAYERS, SSD_D_INNER, D_MODEL), SSD_D_INNER ** -0.5)
    inp['lru_w_in'] = nrm((N_LRU_LAYERS, D_MODEL, 2 * LRU_WIDTH), D_MODEL ** -0.5)
    inp['lru_conv_w'] = nrm((N_LRU_LAYERS, LRU_CONV, LRU_WIDTH), LRU_CONV ** -0.5)
    inp['lru_conv_b'] = nrm((N_LRU_LAYERS, LRU_WIDTH), 0.02)
    inp['lru_w_r'] = nrm((N_LRU_LAYERS, LRU_BLOCKS, LRU_BLOCK_DIM, LRU_BLOCK_DIM), LRU_BLOCK_DIM ** -0.5)
    inp['lru_b_r'] = nrm((N_LRU_LAYERS, LRU_BLOCKS, LRU_BLOCK_DIM), 0.02)
    inp['lru_w_i'] = nrm((N_LRU_LAYERS, LRU_BLOCKS, LRU_BLOCK_DIM, LRU_BLOCK_DIM), LRU_BLOCK_DIM ** -0.5)
    inp['lru_b_i'] = nrm((N_LRU_LAYERS, LRU_BLOCKS, LRU_BLOCK_DIM), 0.02)
    s0 = jax.random.uniform(next(keys), (N_LRU_LAYERS, LRU_WIDTH), F32, 0.9, 0.999) ** (1.0 / LRU_C)
    inp['lru_lambda'] = jnp.log(s0) - jnp.log1p(-s0)
    inp['lru_w_out'] = nrm((N_LRU_LAYERS, LRU_WIDTH, D_MODEL), LRU_WIDTH ** -0.5)
    inp['nsa_w_in'] = nrm((N_NSA_LAYERS, D_MODEL, nsa_in_cols), D_MODEL ** -0.5)
    inp['nsa_q_norm'] = 1.0 + nrm((N_NSA_LAYERS, d), 0.02)
    inp['nsa_k_norm'] = 1.0 + nrm((N_NSA_LAYERS, 3, d), 0.02)
    inp['nsa_cmp_pe'] = nrm((N_NSA_LAYERS, 2, CMP_LEN, d), 0.1)
    inp['nsa_cmp_w1'] = nrm((N_NSA_LAYERS, 2, CMP_LEN, d, d), (CMP_LEN * d) ** -0.5)
    inp['nsa_cmp_w2'] = nrm((N_NSA_LAYERS, 2, d, d), d ** -0.5)
    inp['rel_bias_table'] = nrm((REL_BUCKETS, NSA_HEADS), 0.5)
    inp['nsa_w_out'] = nrm((N_NSA_LAYERS, NSA_HEADS * d, D_MODEL), (NSA_HEADS * d) ** -0.5)
    return inp


def reference(x_prompt, x_sample, state_ssd, state_ssd_conv, state_lru, state_lru_conv, cache_nsa_kv,
              cache_nsa_win, page_table, ffn_norm, w_ffn_in, w_ffn_out, mix_norm, ssd_w_in, ssd_conv_w,
              ssd_conv_b, ssd_dt_bias, ssd_a_log, ssd_d, ssd_norm, ssd_w_out, lru_w_in, lru_conv_w, lru_conv_b,
              lru_w_r, lru_b_r, lru_w_i, lru_b_i, lru_lambda, lru_w_out, nsa_w_in, nsa_q_norm, nsa_k_norm,
              nsa_cmp_pe, nsa_cmp_w1, nsa_cmp_w2, rel_bias_table, nsa_w_out):
    prm = dict(ffn_norm=ffn_norm, w_ffn_in=w_ffn_in, w_ffn_out=w_ffn_out, mix_norm=mix_norm,
               ssd_w_in=ssd_w_in, ssd_conv_w=ssd_conv_w, ssd_conv_b=ssd_conv_b, ssd_dt_bias=ssd_dt_bias,
               ssd_a_log=ssd_a_log, ssd_d=ssd_d, ssd_norm=ssd_norm, ssd_w_out=ssd_w_out,
               lru_w_in=lru_w_in, lru_conv_w=lru_conv_w, lru_conv_b=lru_conv_b, lru_w_r=lru_w_r,
               lru_b_r=lru_b_r, lru_w_i=lru_w_i, lru_b_i=lru_b_i, lru_lambda=lru_lambda, lru_w_out=lru_w_out,
               nsa_w_in=nsa_w_in, nsa_q_norm=nsa_q_norm, nsa_k_norm=nsa_k_norm, nsa_cmp_pe=nsa_cmp_pe,
               nsa_cmp_w1=nsa_cmp_w1, nsa_cmp_w2=nsa_cmp_w2, rel_bias_table=rel_bias_table, nsa_w_out=nsa_w_out)
    G, d = NSA_GROUPS, NSA_HEAD_DIM
    bp, bs = x_prompt.shape[0], x_sample.shape[0]
    wb = cache_nsa_win.shape[2]
    (y_prompt, ssd_h_p, ssd_buf_p, lru_h_p, lru_buf_p, rows_p, win_p) = trunk(
        x_prompt, 0,
        jnp.zeros((N_SSD_LAYERS, bp) + state_ssd.shape[2:], state_ssd.dtype),
        jnp.zeros((N_SSD_LAYERS, bp) + state_ssd_conv.shape[2:], state_ssd_conv.dtype),
        jnp.zeros((N_LRU_LAYERS, bp) + state_lru.shape[2:], state_lru.dtype),
        jnp.zeros((N_LRU_LAYERS, bp) + state_lru_conv.shape[2:], state_lru_conv.dtype),
        jnp.zeros((N_NSA_LAYERS, bp, 0, 4, G, d), cache_nsa_kv.dtype),
        jnp.zeros((N_NSA_LAYERS, bp, wb, 2, G, d), cache_nsa_win.dtype), prm)
    n_pages = page_table.shape[1]
    past = cache_nsa_kv[:, page_table].reshape(N_NSA_LAYERS, bs, n_pages * PAGE_SIZE, 4, G, d)
    (y_sample, ssd_h_s, ssd_buf_s, lru_h_s, lru_buf_s, rows_s, win_s) = trunk(
        x_sample, n_pages * PAGE_SIZE, state_ssd, state_ssd_conv, state_lru, state_lru_conv,
        past, cache_nsa_win, prm)
    return (y_prompt, y_sample, ssd_h_p, ssd_h_s, ssd_buf_p, ssd_buf_s, lru_h_p, lru_h_s,
            lru_buf_p, lru_buf_s, rows_p, rows_s, win_p, win_s)
```

```python
import functools
import math

import jax
import jax.numpy as jnp
from jax import lax
from jax.experimental import pallas as pl
from jax.experimental.pallas import tpu as pltpu

F32 = jnp.float32
BF16 = jnp.bfloat16

NORM_EPS = 1e-6
N_MIXERS = 3
SSD_HEAD_DIM = 64
SSD_STATE = 128
SSD_GROUPS = 8
SSD_CHUNK = 128
LRU_BLOCKS = 16
LRU_C = 8.0
NSA_HEADS = 32
NSA_GROUPS = 4
CMP_LEN = 32
CMP_STRIDE = 16
SEL_LEN = 64
SEL_TOPK = 16
WINDOW = 512
SEL_QBLOCK = 64
WIN_QBLOCK = 128
FORCE_BONUS = 1e4
NEG_INF = -1e30
REL_BUCKETS = 32
REL_MAX_DIST = 128

V7X_VMEM_LIMIT_BYTES = 56 * 1024 * 1024
LANE = 128


def _row_tile(m, cap):
    best = None
    for t in range(16, min(m, cap) + 1, 16):
        if m % t == 0:
            best = t
    assert best is not None, m
    return best


def _rmsnorm_kernel(x_ref, w_ref, o_ref):
    x = x_ref[...]
    y = x * lax.rsqrt(jnp.mean(x * x, axis=-1, keepdims=True) + NORM_EPS)
    o_ref[...] = (y * w_ref[...]).astype(o_ref.dtype)


def _rmsnorm(x, w_row):
    m, d = x.shape
    tm = _row_tile(m, 512)
    return pl.pallas_call(
        _rmsnorm_kernel,
        grid=(m // tm,),
        in_specs=[pl.BlockSpec((tm, d), lambda i: (i, 0)),
                  pl.BlockSpec((1, d), lambda i: (0, 0))],
        out_specs=pl.BlockSpec((tm, d), lambda i: (i, 0)),
        out_shape=jax.ShapeDtypeStruct((m, d), BF16),
        compiler_params=pltpu.CompilerParams(dimension_semantics=("parallel",)),
        name="rmsnorm",
    )(x, w_row)


def _mm_kernel(a_ref, w_ref, o_ref):
    acc = jnp.dot(a_ref[...], w_ref[...].astype(BF16), preferred_element_type=F32)
    o_ref[...] = acc.astype(o_ref.dtype)


def _mm_res_kernel(a_ref, w_ref, r_ref, o_ref, *, scale):
    acc = jnp.dot(a_ref[...], w_ref[...].astype(BF16), preferred_element_type=F32)
    o_ref[...] = r_ref[...] + scale * acc


def _mm_swiglu_kernel(a_ref, wg_ref, wu_ref, o_ref):
    a = a_ref[...]
    g = jnp.dot(a, wg_ref[...].astype(BF16), preferred_element_type=F32)
    u = jnp.dot(a, wu_ref[...].astype(BF16), preferred_element_type=F32)
    o_ref[...] = (jax.nn.silu(g) * u).astype(o_ref.dtype)


def _w_spec(w, lead, kdim, tn, k_blk, n_blk0):
    nlead = len(lead)
    return pl.BlockSpec((None,) * nlead + (kdim, tn),
                        lambda i, j: tuple(lead) + (k_blk, n_blk0 + j))


def _matmul(a, w, lead, *, a_k0=0, w_k0=0, kdim, n0, ndim, tn, tm_cap=1408,
            res=None, scale=1.0, out_dtype=F32, name="matmul"):
    m = a.shape[0]
    tm = _row_tile(m, tm_cap)
    assert a_k0 % kdim == 0 and w_k0 % kdim == 0 and n0 % tn == 0 and ndim % tn == 0
    a_spec = pl.BlockSpec((tm, kdim), lambda i, j: (i, a_k0 // kdim))
    w_spec = _w_spec(w, lead, kdim, tn, w_k0 // kdim, n0 // tn)
    o_spec = pl.BlockSpec((tm, tn), lambda i, j: (i, j))
    grid = (m // tm, ndim // tn)
    params = pltpu.CompilerParams(dimension_semantics=("parallel", "parallel"),
                                  vmem_limit_bytes=V7X_VMEM_LIMIT_BYTES)
    if res is None:
        return pl.pallas_call(
            _mm_kernel, grid=grid, in_specs=[a_spec, w_spec], out_specs=o_spec,
            out_shape=jax.ShapeDtypeStruct((m, ndim), out_dtype),
            compiler_params=params, name=name)(a, w)
    return pl.pallas_call(
        functools.partial(_mm_res_kernel, scale=scale), grid=grid,
        in_specs=[a_spec, w_spec, o_spec], out_specs=o_spec,
        out_shape=jax.ShapeDtypeStruct((m, ndim), F32),
        compiler_params=params, name=name)(a, w, res)


def _matmul_swiglu(a, w, lead, *, tn=256, tm_cap=1408):
    m, k = a.shape
    f = w.shape[-1] // 2
    tm = _row_tile(m, tm_cap)
    assert f % tn == 0
    a_spec = pl.BlockSpec((tm, k), lambda i, j: (i, 0))
    wg_spec = _w_spec(w, lead, k, tn, 0, 0)
    wu_spec = _w_spec(w, lead, k, tn, 0, f // tn)
    return pl.pallas_call(
        _mm_swiglu_kernel, grid=(m // tm, f // tn),
        in_specs=[a_spec, wg_spec, wu_spec],
        out_specs=pl.BlockSpec((tm, tn), lambda i, j: (i, j)),
        out_shape=jax.ShapeDtypeStruct((m, f), BF16),
        compiler_params=pltpu.CompilerParams(dimension_semantics=("parallel", "parallel"),
                                             vmem_limit_bytes=V7X_VMEM_LIMIT_BYTES),
        name="ffn_in_swiglu")(a, w, w)


def _out_proj(a, w, lead, res, scale, name):
    k = a.shape[1]
    n = w.shape[-1]
    nsplit = 1
    while (k // nsplit) > 5632 or k % nsplit:
        nsplit += 1
    kdim = k // nsplit
    assert kdim % LANE == 0
    out = res
    for s in range(nsplit):
        out = _matmul(a, w, lead, a_k0=s * kdim, w_k0=s * kdim, kdim=kdim, n0=0, ndim=n, tn=256,
                      res=out, scale=scale, name=name)
    return out


def _ffn(x, prm, i, which):
    xn = _rmsnorm(x, prm['ffn_norm'][i, which][None])
    h = _matmul_swiglu(xn, prm['w_ffn_in'], (i, which))
    return _out_proj(h, prm['w_ffn_out'], (i, which), x, 0.5, "ffn_out")


def _rms_norm_jnp(x, w):
    y = x * lax.rsqrt(jnp.mean(x * x, axis=-1, keepdims=True) + NORM_EPS)
    return y * w


def _causal_dwconv(x, buf, w, b):
    width, L = w.shape[0], x.shape[1]
    xp = jnp.concatenate([buf.astype(x.dtype), x], axis=1)
    y = b + sum(xp[:, k:k + L] * w[k] for k in range(width))
    return y, xp[:, L:]


def _masked_softmax(logits, mask):
    p = jax.nn.softmax(jnp.where(mask, logits, NEG_INF), axis=-1)
    return p * mask


def _rel_bucket(dist):
    n = jnp.maximum(dist, 0)
    exact = REL_BUCKETS // 2
    log_ratio = jnp.log(jnp.maximum(n, 1).astype(F32) / exact) / math.log(REL_MAX_DIST / exact)
    large = jnp.minimum(exact + (log_ratio * (REL_BUCKETS - exact)).astype(jnp.int32), REL_BUCKETS - 1)
    return jnp.where(n < exact, n, large)


def _sweep_query_blocks(fn, T, cap):
    qb = cap if T % cap == 0 else T
    out = lax.map(lambda i: fn(i * qb, qb), jnp.arange(T // qb))
    return jnp.moveaxis(out, 0, 1).reshape((out.shape[1], T) + out.shape[3:])


def _ssd_chunked_scan(x, dt, a, bm, cm, h0):
    b, L, H, P = x.shape
    G, N = bm.shape[2], bm.shape[3]
    E = H // G
    Q = min(SSD_CHUNK, L)
    pad = (-L) % Q
    x, dt, bm, cm = (jnp.pad(t.astype(F32), [(0, 0), (0, pad)] + [(0, 0)] * (t.ndim - 2)) for t in (x, dt, bm, cm))
    c = (L + pad) // Q
    xdt = (x * dt[..., None]).reshape(b, c, Q, G, E, P)
    da = (dt * a.astype(F32)).reshape(b, c, Q, G, E)
    bm = bm.reshape(b, c, Q, G, N)
    cm = cm.reshape(b, c, Q, G, N)
    cs = jnp.cumsum(da, axis=2)
    causal = jnp.tril(jnp.ones((Q, Q), bool))[:, :, None, None]
    seg = cs[:, :, :, None] - cs[:, :, None, :]
    decay = jnp.exp(jnp.where(causal, seg, NEG_INF))
    cb = jnp.einsum('bclgn,bcsgn->bclsg', cm, bm)
    y_diag = jnp.einsum('bclsg,bclsge,bcsgep->bclgep', cb, decay, xdt)
    to_end = jnp.exp(cs[:, :, -1:] - cs)
    chunk_states = jnp.einsum('bcsgn,bcsge,bcsgep->bcgepn', bm, to_end, xdt)
    chunk_decay = jnp.exp(cs[:, :, -1])

    def step(h, inp):
        st, dec = inp
        return dec[..., None, None] * h + st, h

    h_last, h_in = lax.scan(step, h0.astype(F32).reshape(b, G, E, P, N),
                            (jnp.moveaxis(chunk_states, 1, 0), jnp.moveaxis(chunk_decay, 1, 0)))
    h_in = jnp.moveaxis(h_in, 0, 1)
    y_off = jnp.einsum('bclgn,bcgepn,bclge->bclgep', cm, h_in, jnp.exp(cs))
    y = (y_diag + y_off).reshape(b, c * Q, H, P)[:, :L]
    return y, h_last.reshape(b, H, P, N)


def _ssd_core(z, xbc, dt_raw, conv_buf, ssm_state, conv_w, conv_b, dt_bias, a_log, d_skip, norm_w):
    b, L, d_inner = z.shape
    heads = d_inner // SSD_HEAD_DIM
    xbc, new_buf = _causal_dwconv(xbc, conv_buf, conv_w, conv_b)
    xbc = jax.nn.silu(xbc)
    gn = SSD_GROUPS * SSD_STATE
    xs = xbc[..., :d_inner].reshape(b, L, heads, SSD_HEAD_DIM)
    bm = xbc[..., d_inner:d_inner + gn].reshape(b, L, SSD_GROUPS, SSD_STATE)
    cm = xbc[..., d_inner + gn:].reshape(b, L, SSD_GROUPS, SSD_STATE)
    dt = jax.nn.softplus(dt_raw + dt_bias)
    a = -jnp.exp(a_log)
    y, new_state = _ssd_chunked_scan(xs, dt, a, bm, cm, ssm_state)
    y = y + d_skip[:, None] * xs
    y = y.reshape(b, L, d_inner) * jax.nn.silu(z)
    y = _rms_norm_jnp(y.reshape(b, L, SSD_GROUPS, d_inner // SSD_GROUPS),
                      norm_w.reshape(SSD_GROUPS, -1)).reshape(b, L, d_inner)
    return y, new_buf, new_state


def _rglru_core(gate, xb, conv_buf, h_state, conv_w, conv_b, w_r, b_r, w_i, b_i, lam):
    b, L, width = xb.shape
    xc, new_buf = _causal_dwconv(xb, conv_buf, conv_w, conv_b)
    xblk = xc.reshape(b, L, LRU_BLOCKS, width // LRU_BLOCKS)
    r = jax.nn.sigmoid(jnp.einsum('blki,kij->blkj', xblk, w_r) + b_r).reshape(b, L, width)
    ig = jax.nn.sigmoid(jnp.einsum('blki,kij->blkj', xblk, w_i) + b_i).reshape(b, L, width)
    log_a = (-LRU_C * r) * jax.nn.softplus(-lam)
    a = jnp.exp(log_a)
    u = jnp.sqrt(-jnp.expm1(2.0 * log_a)) * ig * xc
    u = u.at[:, 0].add(a[:, 0] * h_state)

    def combine(left, right):
        a_l, b_l = left
        a_r, b_r2 = right
        return a_l * a_r, a_r * b_l + b_r2

    _, hs = lax.associative_scan(combine, (a, u), axis=1)
    y = hs * jax.nn.gelu(gate)
    return y, new_buf, hs[:, -1]


def _compress_blocks(rows, pe, w1, w2):
    n = (rows.shape[1] - CMP_LEN) // CMP_STRIDE + 1
    idx = jnp.arange(n)[:, None] * CMP_STRIDE + jnp.arange(CMP_LEN)[None, :]
    blk = rows[:, idx] + pe[:, None, :]
    hid = jax.nn.silu(jnp.einsum('bnlgd,lde->bnge', blk, w1))
    return jnp.einsum('bnge,ef->bngf', hid, w2)


def _cmp_attend(q, q_pos, kc, vc, tab, scale):
    n = kc.shape[1]
    blk_end = jnp.arange(n) * CMP_STRIDE + CMP_LEN - 1
    dist = q_pos[:, None] - blk_end[None, :]
    mask = (dist >= 0)[None, :, None, None, :]
    bias = jnp.transpose(tab[_rel_bucket(dist)], (0, 2, 3, 1))[None]
    logits = jnp.einsum('bqged,bngd->bqgen', q, kc) * scale + bias
    p = _masked_softmax(logits, mask)
    o = jnp.einsum('bqgen,bngd->bqged', p, vc)
    return o, p.sum(axis=3)


def _sel_attend(q, q_pos, blk_idx, ks, vs, tab, scale):
    b, Qb, G = blk_idx.shape[:3]
    tok = (blk_idx[..., None] * SEL_LEN + jnp.arange(SEL_LEN)).reshape(b, Qb, G, -1)
    bi = jnp.arange(b)[:, None, None, None]
    gi = jnp.arange(G)[None, None, :, None]
    k_g = ks[bi, tok, gi]
    v_g = vs[bi, tok, gi]
    dist = q_pos[None, :, None, None] - tok
    mask = (dist >= 0)[:, :, :, None, :]
    bias = jnp.moveaxis(tab[_rel_bucket(dist), gi], -1, 3)
    logits = jnp.einsum('bqged,bqgsd->bqges', q, k_g) * scale + bias
    p = _masked_softmax(logits, mask)
    return jnp.einsum('bqges,bqgsd->bqged', p, v_g)


def _win_attend(q, q_pos, k_band, v_band, k_pos, tab, scale):
    dist = q_pos[:, None] - k_pos[None, :]
    mask = ((dist >= 0) & (dist < WINDOW) & (k_pos[None, :] >= 0))[None, :, None, None, :]
    bias = jnp.transpose(tab[_rel_bucket(dist)], (0, 2, 3, 1))[None]
    logits = jnp.einsum('bqged,bsgd->bqges', q, k_band) * scale + bias
    p = _masked_softmax(logits, mask)
    return jnp.einsum('bqges,bsgd->bqged', p, v_band)


def _nsa_core(qp, kvp, gp, past_rows, win_buf, q_pos0, q_norm_w, k_norm_w, cmp_pe, cmp_w1, cmp_w2, table):
    b, T, n_q = qp.shape
    G = NSA_GROUPS
    E = NSA_HEADS // G
    d = n_q // NSA_HEADS
    scale = d ** -0.5
    q = _rms_norm_jnp(qp.reshape(b, T, G, E, d), q_norm_w)
    kv_new = kvp.reshape(b, T, 6, G, d)
    gates = jax.nn.sigmoid(gp).reshape(b, T, G, E, 3)
    tab = table.reshape(REL_BUCKETS, G, E)
    q_pos = q_pos0 + jnp.arange(T)
    rows = jnp.concatenate([past_rows, kv_new[:, :, :4]], axis=1)
    Tk = rows.shape[1]
    kc = _rms_norm_jnp(_compress_blocks(rows[:, :, 0], cmp_pe[0], cmp_w1[0], cmp_w2[0]), k_norm_w[0])
    vc = _compress_blocks(rows[:, :, 1], cmp_pe[1], cmp_w1[1], cmp_w2[1])
    o_cmp, p_grp = _cmp_attend(q, q_pos, kc, vc, tab, scale)
    n_cmp = kc.shape[1]
    n_sel = -(-Tk // SEL_LEN)
    c_start = jnp.arange(n_cmp) * CMP_STRIDE
    s_start = jnp.arange(n_sel) * SEL_LEN
    cover = ((c_start[:, None] < s_start[None, :] + SEL_LEN) &
             (c_start[:, None] + CMP_LEN > s_start[None, :])).astype(F32)
    p_sel = jnp.einsum('bqgn,nj->bqgj', p_grp, cover)
    cur = q_pos // SEL_LEN
    jj = jnp.arange(n_sel)[None, :]
    visible = s_start[None, :] <= q_pos[:, None]
    forced = (jj == 0) | (jj == cur[:, None]) | (jj == cur[:, None] - 1)
    score = jnp.where(visible[None, :, None, :], p_sel + FORCE_BONUS * forced[None, :, None, :].astype(F32), NEG_INF)
    _, blk_idx = lax.top_k(score, min(SEL_TOPK, n_sel))
    pad = n_sel * SEL_LEN - Tk
    ks = jnp.pad(_rms_norm_jnp(rows[:, :, 2], k_norm_w[1]), ((0, 0), (0, pad), (0, 0), (0, 0)))
    vs = jnp.pad(rows[:, :, 3], ((0, 0), (0, pad), (0, 0), (0, 0)))

    def sel_block(start, qb):
        return _sel_attend(lax.dynamic_slice_in_dim(q, start, qb, 1),
                           lax.dynamic_slice_in_dim(q_pos, start, qb, 0),
                           lax.dynamic_slice_in_dim(blk_idx, start, qb, 1), ks, vs, tab, scale)

    o_sel = _sweep_query_blocks(sel_block, T, SEL_QBLOCK)
    wb = win_buf.shape[1]
    band = jnp.concatenate([jnp.zeros((b, WINDOW - wb, 2, G, d), F32), win_buf, kv_new[:, :, 4:]], axis=1)
    kw = _rms_norm_jnp(band[:, :, 0], k_norm_w[2])
    vw = band[:, :, 1]

    def win_block(start, qb):
        k_pos = q_pos0 - WINDOW + start + jnp.arange(WINDOW + qb)
        return _win_attend(lax.dynamic_slice_in_dim(q, start, qb, 1), q_pos0 + start + jnp.arange(qb),
                           lax.dynamic_slice_in_dim(kw, start, WINDOW + qb, 1),
                           lax.dynamic_slice_in_dim(vw, start, WINDOW + qb, 1), k_pos, tab, scale)

    o_win = _sweep_query_blocks(win_block, T, WIN_QBLOCK)
    o = gates[..., 0:1] * o_cmp + gates[..., 1:2] * o_sel + gates[..., 2:3] * o_win
    new_win = jnp.concatenate([win_buf, kv_new[:, :, 4:]], axis=1)[:, -wb:]
    return o.reshape(b, T, n_q), kv_new[:, :, :4], new_win


def kernel(x_prompt, x_sample, state_ssd, state_ssd_conv, state_lru, state_lru_conv, cache_nsa_kv, cache_nsa_win, page_table, ffn_norm, w_ffn_in, w_ffn_out, mix_norm, ssd_w_in, ssd_conv_w, ssd_conv_b, ssd_dt_bias, ssd_a_log, ssd_d, ssd_norm, ssd_w_out, lru_w_in, lru_conv_w, lru_conv_b, lru_w_r, lru_b_r, lru_w_i, lru_b_i, lru_lambda, lru_w_out, nsa_w_in, nsa_q_norm, nsa_k_norm, nsa_cmp_pe, nsa_cmp_w1, nsa_cmp_w2, rel_bias_table, nsa_w_out):
    prm = dict(ffn_norm=ffn_norm, w_ffn_in=w_ffn_in, w_ffn_out=w_ffn_out)
    bp, sp, dm = x_prompt.shape
    bs, ss, _ = x_sample.shape
    mp, ms = bp * sp, bs * ss
    depth = mix_norm.shape[0]
    G = NSA_GROUPS
    d_head = dm // NSA_HEADS
    wb = cache_nsa_win.shape[2]
    n_pages = page_table.shape[1]
    page = cache_nsa_kv.shape[2]
    past_len = n_pages * page

    def split(t):
        return t[:mp].reshape(bp, sp, -1), t[mp:].reshape(bs, ss, -1)

    def merge(tp, ts):
        return jnp.concatenate([tp.reshape(mp, -1), ts.reshape(ms, -1)], axis=0)

    x = merge(x_prompt, x_sample)
    outs = {k: [] for k in ('ssd_h_p', 'ssd_h_s', 'ssd_buf_p', 'ssd_buf_s', 'lru_h_p', 'lru_h_s',
                            'lru_buf_p', 'lru_buf_s', 'rows_p', 'rows_s', 'win_p', 'win_s')}
    for i in range(depth):
        kind, j = i % N_MIXERS, i // N_MIXERS
        x = _ffn(x, prm, i, 0)
        hn = _rmsnorm(x, mix_norm[i][None])
        if kind == 0:
            d_inner = ssd_w_out.shape[1]
            conv_dim = ssd_conv_w.shape[2]
            heads = ssd_dt_bias.shape[1]
            z = _matmul(hn, ssd_w_in, (j,), kdim=dm, n0=0, ndim=d_inner, tn=512, name="ssd_in_z")
            xbc = _matmul(hn, ssd_w_in, (j,), kdim=dm, n0=d_inner, ndim=conv_dim, tn=512, name="ssd_in_xbc")
            dt_raw = _matmul(hn, ssd_w_in, (j,), kdim=dm, n0=d_inner + conv_dim, ndim=heads, tn=heads,
                             name="ssd_in_dt")
            (z_p, z_s), (xbc_p, xbc_s), (dt_p, dt_s) = split(z), split(xbc), split(dt_raw)
            w = (ssd_conv_w[j], ssd_conv_b[j], ssd_dt_bias[j], ssd_a_log[j], ssd_d[j], ssd_norm[j])
            y_p, buf_p, st_p = _ssd_core(z_p, xbc_p, dt_p, jnp.zeros((bp,) + state_ssd_conv.shape[2:], F32),
                                         jnp.zeros((bp,) + state_ssd.shape[2:], F32), *w)
            y_s, buf_s, st_s = _ssd_core(z_s, xbc_s, dt_s, state_ssd_conv[j], state_ssd[j], *w)
            outs['ssd_h_p'].append(st_p); outs['ssd_h_s'].append(st_s)
            outs['ssd_buf_p'].append(buf_p); outs['ssd_buf_s'].append(buf_s)
            x = _out_proj(merge(y_p, y_s).astype(BF16), ssd_w_out, (j,), x, 1.0, "ssd_out")
        elif kind == 1:
            width = lru_w_out.shape[1]
            gx = _matmul(hn, lru_w_in, (j,), kdim=dm, n0=0, ndim=2 * width, tn=512, name="lru_in")
            gx_p, gx_s = split(gx)
            w = (lru_conv_w[j], lru_conv_b[j], lru_w_r[j], lru_b_r[j], lru_w_i[j], lru_b_i[j], lru_lambda[j])
            y_p, buf_p, st_p = _rglru_core(gx_p[..., :width], gx_p[..., width:],
                                           jnp.zeros((bp,) + state_lru_conv.shape[2:], F32),
                                           jnp.zeros((bp,) + state_lru.shape[2:], F32), *w)
            y_s, buf_s, st_s = _rglru_core(gx_s[..., :width], gx_s[..., width:], state_lru_conv[j], state_lru[j], *w)
            outs['lru_h_p'].append(st_p); outs['lru_h_s'].append(st_s)
            outs['lru_buf_p'].append(buf_p); outs['lru_buf_s'].append(buf_s)
            x = _out_proj(merge(y_p, y_s).astype(BF16), lru_w_out, (j,), x, 1.0, "lru_out")
        else:
            n_q, n_kv = NSA_HEADS * d_head, 6 * G * d_head
            qp = _matmul(hn, nsa_w_in, (j,), kdim=dm, n0=0, ndim=n_q, tn=512, name="nsa_in_q")
            kvp = _matmul(hn, nsa_w_in, (j,), kdim=dm, n0=n_q, ndim=n_kv, tn=512, name="nsa_in_kv")
            n_gate = nsa_w_in.shape[2] - n_q - n_kv
            w_gate = jnp.pad(nsa_w_in[j][None, :, n_q + n_kv:], ((0, 0), (0, 0), (0, LANE - n_gate)))
            gp = _matmul(hn, w_gate, (0,), kdim=dm, n0=0, ndim=LANE, tn=LANE, name="nsa_in_gate")[:, :n_gate]
            (q_p, q_s), (kv_p, kv_s), (g_p, g_s) = split(qp), split(kvp), split(gp)
            w = (nsa_q_norm[j], nsa_k_norm[j], nsa_cmp_pe[j], nsa_cmp_w1[j], nsa_cmp_w2[j], rel_bias_table)
            o_p, rows_p, win_p = _nsa_core(q_p, kv_p, g_p, jnp.zeros((bp, 0, 4, G, d_head), F32),
                                           jnp.zeros((bp, wb, 2, G, d_head), F32), 0, *w)
            past = cache_nsa_kv[j][page_table].reshape(bs, past_len, 4, G, d_head)
            o_s, rows_s, win_s = _nsa_core(q_s, kv_s, g_s, past, cache_nsa_win[j], past_len, *w)
            outs['rows_p'].append(rows_p); outs['rows_s'].append(rows_s)
            outs['win_p'].append(win_p); outs['win_s'].append(win_s)
            x = _out_proj(merge(o_p, o_s).astype(BF16), nsa_w_out, (j,), x, 1.0, "nsa_out")
        x = _ffn(x, prm, i, 1)
    y_prompt, y_sample = x[:mp].reshape(bp, sp, dm), x[mp:].reshape(bs, ss, dm)
    st = {k: jnp.stack(v) for k, v in outs.items()}
    return (y_prompt, y_sample, st['ssd_h_p'], st['ssd_h_s'], st['ssd_buf_p'], st['ssd_buf_s'],
            st['lru_h_p'], st['lru_h_s'], st['lru_buf_p'], st['lru_buf_s'],
            st['rows_p'], st['rows_s'], st['win_p'], st['win_s'])
```

```python
import functools
import math

import jax
import jax.numpy as jnp
from jax import lax
from jax.experimental import pallas as pl
from jax.experimental.pallas import tpu as pltpu

F32 = jnp.float32
BF16 = jnp.bfloat16

NORM_EPS = 1e-6
N_MIXERS = 3
SSD_HEAD_DIM = 64
SSD_STATE = 128
SSD_GROUPS = 8
SSD_CHUNK = 128
LRU_BLOCKS = 16
LRU_C = 8.0
NSA_HEADS = 32
NSA_GROUPS = 4
CMP_LEN = 32
CMP_STRIDE = 16
SEL_LEN = 64
SEL_TOPK = 16
WINDOW = 512
SEL_QBLOCK = 64
WIN_QBLOCK = 128
FORCE_BONUS = 1e4
NEG_INF = -1e30
REL_BUCKETS = 32
REL_MAX_DIST = 128

V7X_VMEM_LIMIT_BYTES = 56 * 1024 * 1024
LANE = 128


def _row_tile(m, cap):
    best = None
    for t in range(16, min(m, cap) + 1, 16):
        if m % t == 0:
            best = t
    assert best is not None, m
    return best


def _rmsnorm_kernel(x_ref, w_ref, o_ref):
    x = x_ref[...]
    y = x * lax.rsqrt(jnp.mean(x * x, axis=-1, keepdims=True) + NORM_EPS)
    o_ref[...] = (y * w_ref[...]).astype(o_ref.dtype)


def _rmsnorm(x, w_row):
    m, d = x.shape
    tm = _row_tile(m, 512)
    return pl.pallas_call(
        _rmsnorm_kernel,
        grid=(m // tm,),
        in_specs=[pl.BlockSpec((tm, d), lambda i: (i, 0)),
                  pl.BlockSpec((1, d), lambda i: (0, 0))],
        out_specs=pl.BlockSpec((tm, d), lambda i: (i, 0)),
        out_shape=jax.ShapeDtypeStruct((m, d), BF16),
        compiler_params=pltpu.CompilerParams(dimension_semantics=("parallel",)),
        name="rmsnorm",
    )(x, w_row)


def _mm_kernel(a_ref, w_ref, o_ref):
    acc = jnp.dot(a_ref[...], w_ref[...].astype(BF16), preferred_element_type=F32)
    o_ref[...] = acc.astype(o_ref.dtype)


def _mm_res_kernel(a_ref, w_ref, r_ref, o_ref, *, scale):
    acc = jnp.dot(a_ref[...], w_ref[...].astype(BF16), preferred_element_type=F32)
    o_ref[...] = r_ref[...] + scale * acc


def _mm_swiglu_kernel(a_ref, wg_ref, wu_ref, o_ref):
    a = a_ref[...]
    g = jnp.dot(a, wg_ref[...].astype(BF16), preferred_element_type=F32)
    u = jnp.dot(a, wu_ref[...].astype(BF16), preferred_element_type=F32)
    o_ref[...] = (jax.nn.silu(g) * u).astype(o_ref.dtype)


def _w_spec(w, lead, kdim, tn, k_blk, n_blk0):
    nlead = len(lead)
    return pl.BlockSpec((None,) * nlead + (kdim, tn),
                        lambda i, j: tuple(lead) + (k_blk, n_blk0 + j))


def _matmul(a, w, lead, *, a_k0=0, w_k0=0, kdim, n0, ndim, tn, tm_cap=1408,
            res=None, scale=1.0, out_dtype=F32, name="matmul"):
    m = a.shape[0]
    tm = _row_tile(m, tm_cap)
    assert a_k0 % kdim == 0 and w_k0 % kdim == 0 and n0 % tn == 0 and ndim % tn == 0
    a_spec = pl.BlockSpec((tm, kdim), lambda i, j: (i, a_k0 // kdim))
    w_spec = _w_spec(w, lead, kdim, tn, w_k0 // kdim, n0 // tn)
    o_spec = pl.BlockSpec((tm, tn), lambda i, j: (i, j))
    grid = (m // tm, ndim // tn)
    params = pltpu.CompilerParams(dimension_semantics=("parallel", "parallel"),
                                  vmem_limit_bytes=V7X_VMEM_LIMIT_BYTES)
    if res is None:
        return pl.pallas_call(
            _mm_kernel, grid=grid, in_specs=[a_spec, w_spec], out_specs=o_spec,
            out_shape=jax.ShapeDtypeStruct((m, ndim), out_dtype),
            compiler_params=params, name=name)(a, w)
    return pl.pallas_call(
        functools.partial(_mm_res_kernel, scale=scale), grid=grid,
        in_specs=[a_spec, w_spec, o_spec], out_specs=o_spec,
        out_shape=jax.ShapeDtypeStruct((m, ndim), F32),
        compiler_params=params, name=name)(a, w, res)


def _matmul_swiglu(a, w, lead, *, tn=256, tm_cap=1408):
    m, k = a.shape
    f = w.shape[-1] // 2
    tm = _row_tile(m, tm_cap)
    assert f % tn == 0
    a_spec = pl.BlockSpec((tm, k), lambda i, j: (i, 0))
    wg_spec = _w_spec(w, lead, k, tn, 0, 0)
    wu_spec = _w_spec(w, lead, k, tn, 0, f // tn)
    return pl.pallas_call(
        _mm_swiglu_kernel, grid=(m // tm, f // tn),
        in_specs=[a_spec, wg_spec, wu_spec],
        out_specs=pl.BlockSpec((tm, tn), lambda i, j: (i, j)),
        out_shape=jax.ShapeDtypeStruct((m, f), BF16),
        compiler_params=pltpu.CompilerParams(dimension_semantics=("parallel", "parallel"),
                                             vmem_limit_bytes=V7X_VMEM_LIMIT_BYTES),
        name="ffn_in_swiglu")(a, w, w)


def _out_proj(a, w, lead, res, scale, name):
    k = a.shape[1]
    n = w.shape[-1]
    nsplit = 1
    while (k // nsplit) > 5632 or k % nsplit:
        nsplit += 1
    kdim = k // nsplit
    assert kdim % LANE == 0
    out = res
    for s in range(nsplit):
        out = _matmul(a, w, lead, a_k0=s * kdim, w_k0=s * kdim, kdim=kdim, n0=0, ndim=n, tn=256,
                      res=out, scale=scale, name=name)
    return out


def _ffn(x, prm, i, which):
    xn = _rmsnorm(x, prm['ffn_norm'][i, which][None])
    h = _matmul_swiglu(xn, prm['w_ffn_in'], (i, which))
    return _out_proj(h, prm['w_ffn_out'], (i, which), x, 0.5, "ffn_out")


def _rms_norm_jnp(x, w):
    y = x * lax.rsqrt(jnp.mean(x * x, axis=-1, keepdims=True) + NORM_EPS)
    return y * w


def _causal_dwconv(x, buf, w, b):
    width, L = w.shape[0], x.shape[1]
    xp = jnp.concatenate([buf.astype(x.dtype), x], axis=1)
    y = b + sum(xp[:, k:k + L] * w[k] for k in range(width))
    return y, xp[:, L:]


def _ssd_chunked_scan(x, dt, a, bm, cm, h0):
    b, L, H, P = x.shape
    G, N = bm.shape[2], bm.shape[3]
    E = H // G
    Q = min(SSD_CHUNK, L)
    pad = (-L) % Q
    x, dt, bm, cm = (jnp.pad(t.astype(F32), [(0, 0), (0, pad)] + [(0, 0)] * (t.ndim - 2)) for t in (x, dt, bm, cm))
    c = (L + pad) // Q
    xdt = (x * dt[..., None]).reshape(b, c, Q, G, E, P)
    da = (dt * a.astype(F32)).reshape(b, c, Q, G, E)
    bm = bm.reshape(b, c, Q, G, N)
    cm = cm.reshape(b, c, Q, G, N)
    cs = jnp.cumsum(da, axis=2)
    causal = jnp.tril(jnp.ones((Q, Q), bool))[:, :, None, None]
    seg = cs[:, :, :, None] - cs[:, :, None, :]
    decay = jnp.exp(jnp.where(causal, seg, NEG_INF))
    cb = jnp.einsum('bclgn,bcsgn->bclsg', cm, bm)
    y_diag = jnp.einsum('bclsg,bclsge,bcsgep->bclgep', cb, decay, xdt)
    to_end = jnp.exp(cs[:, :, -1:] - cs)
    chunk_states = jnp.einsum('bcsgn,bcsge,bcsgep->bcgepn', bm, to_end, xdt)
    chunk_decay = jnp.exp(cs[:, :, -1])

    def step(h, inp):
        st, dec = inp
        return dec[..., None, None] * h + st, h

    h_last, h_in = lax.scan(step, h0.astype(F32).reshape(b, G, E, P, N),
                            (jnp.moveaxis(chunk_states, 1, 0), jnp.moveaxis(chunk_decay, 1, 0)))
    h_in = jnp.moveaxis(h_in, 0, 1)
    y_off = jnp.einsum('bclgn,bcgepn,bclge->bclgep', cm, h_in, jnp.exp(cs))
    y = (y_diag + y_off).reshape(b, c * Q, H, P)[:, :L]
    return y, h_last.reshape(b, H, P, N)


def _ssd_core(z, xbc, dt_raw, conv_buf, ssm_state, conv_w, conv_b, dt_bias, a_log, d_skip, norm_w):
    b, L, d_inner = z.shape
    heads = d_inner // SSD_HEAD_DIM
    xbc, new_buf = _causal_dwconv(xbc, conv_buf, conv_w, conv_b)
    xbc = jax.nn.silu(xbc)
    gn = SSD_GROUPS * SSD_STATE
    xs = xbc[..., :d_inner].reshape(b, L, heads, SSD_HEAD_DIM)
    bm = xbc[..., d_inner:d_inner + gn].reshape(b, L, SSD_GROUPS, SSD_STATE)
    cm = xbc[..., d_inner + gn:].reshape(b, L, SSD_GROUPS, SSD_STATE)
    dt = jax.nn.softplus(dt_raw + dt_bias)
    a = -jnp.exp(a_log)
    y, new_state = _ssd_chunked_scan(xs, dt, a, bm, cm, ssm_state)
    y = y + d_skip[:, None] * xs
    y = y.reshape(b, L, d_inner) * jax.nn.silu(z)
    y = _rms_norm_jnp(y.reshape(b, L, SSD_GROUPS, d_inner // SSD_GROUPS),
                      norm_w.reshape(SSD_GROUPS, -1)).reshape(b, L, d_inner)
    return y, new_buf, new_state


LRU_SCAN_LANES = 512
SUBLANE = 8


def _gelu_tanh(x):
    return 0.5 * x * (1.0 + jnp.tanh(math.sqrt(2.0 / math.pi) * (x + 0.044715 * (x * x * x))))


def _lru_kernel(gate_ref, xb_ref, prev_ref, conv8_ref, h0_ref, cw_ref, cb_ref, wr_ref, br_ref, wi_ref, bi_ref,
                lam_ref, y_ref, hl_ref, e_ref, a_ref, u_ref, h_ref, *, tm, width):
    t = pl.program_id(1)

    @pl.when(t == 0)
    def _():
        h_ref[...] = h0_ref[...]

    e_ref[0:SUBLANE, :] = jnp.where(t == 0, conv8_ref[...], prev_ref[...])
    e_ref[SUBLANE:SUBLANE + tm, :] = xb_ref[...]
    nconv = cw_ref.shape[0]
    first = SUBLANE - (nconv - 1)
    nblk = wr_ref.shape[0]
    bd = width // nblk
    for k in range(nblk):
        cols = slice(k * bd, (k + 1) * bd)
        s = e_ref[pl.ds(first, tm), cols] * cw_ref[0:1, cols]
        for c in range(1, nconv):
            s = s + e_ref[pl.ds(first + c, tm), cols] * cw_ref[c:c + 1, cols]
        xk = cb_ref[:, cols] + s
        xkb = xk.astype(BF16)
        r = jax.nn.sigmoid(jnp.dot(xkb, wr_ref[k], preferred_element_type=F32) + br_ref[:, cols])
        ig = jax.nn.sigmoid(jnp.dot(xkb, wi_ref[k], preferred_element_type=F32) + bi_ref[:, cols])
        log_a = (-LRU_C * r) * jax.nn.softplus(-lam_ref[:, cols])
        a_ref[:, cols] = jnp.exp(log_a)
        one_minus_a2 = -jnp.tanh(log_a) * (jnp.exp(2.0 * log_a) + 1.0)
        u_ref[:, cols] = jnp.sqrt(one_minus_a2) * ig * xk

    row = lax.broadcasted_iota(jnp.int32, (SUBLANE, LRU_SCAN_LANES), 0)
    for sl in range(width // LRU_SCAN_LANES):
        cols = slice(sl * LRU_SCAN_LANES, (sl + 1) * LRU_SCAN_LANES)

        def body(r, h):
            rows = pl.ds(pl.multiple_of(r * SUBLANE, SUBLANE), SUBLANE)
            a = a_ref[rows, cols]
            u = u_ref[rows, cols]
            for sh in (1, 2, 4):
                a_sh = jnp.where(row >= sh, pltpu.roll(a, sh, 0), 1.0)
                u_sh = jnp.where(row >= sh, pltpu.roll(u, sh, 0), 0.0)
                u = a * u_sh + u
                a = a * a_sh
            hs = a * h + u
            y_ref[rows, cols] = hs * _gelu_tanh(gate_ref[rows, cols])
            return hs[SUBLANE - 1:SUBLANE, :]

        h_ref[:, cols] = lax.fori_loop(0, tm // SUBLANE, body, h_ref[:, cols])
    hl_ref[...] = h_ref[...]


def _lru_mix(gx, row0, nb, seq, conv_buf, h0, conv_w, conv_b, w_r, b_r, w_i, b_i, lam):
    width = gx.shape[1] // 2
    tm = _row_tile(seq, 128) if seq % 16 == 0 else seq
    assert seq % tm == 0 and tm % SUBLANE == 0 and row0 % tm == 0 and width % LRU_SCAN_LANES == 0
    nt = seq // tm
    nconv = conv_w.shape[0]
    conv8 = jnp.pad(conv_buf, ((0, 0), (SUBLANE - (nconv - 1), 0), (0, 0)))
    blk0 = row0 // tm
    sub = tm // SUBLANE

    def row_map(b, t):
        return (blk0 + b * nt + t, 0)

    def xb_map(b, t):
        return (blk0 + b * nt + t, 1)

    def prev_map(b, t):
        return (jnp.maximum((blk0 + b * nt + t) * sub - 1, 0), 1)

    full2 = lambda b, t: (0, 0)
    full3 = lambda b, t: (0, 0, 0)
    y, hl = pl.pallas_call(
        functools.partial(_lru_kernel, tm=tm, width=width),
        grid=(nb, nt),
        in_specs=[pl.BlockSpec((tm, width), row_map),
                  pl.BlockSpec((tm, width), xb_map),
                  pl.BlockSpec((SUBLANE, width), prev_map),
                  pl.BlockSpec((None, SUBLANE, width), lambda b, t: (b, 0, 0)),
                  pl.BlockSpec((None, 1, width), lambda b, t: (b, 0, 0)),
                  pl.BlockSpec(conv_w.shape, full2),
                  pl.BlockSpec((1, width), full2),
                  pl.BlockSpec(w_r.shape, full3),
                  pl.BlockSpec((1, width), full2),
                  pl.BlockSpec(w_i.shape, full3),
                  pl.BlockSpec((1, width), full2),
                  pl.BlockSpec((1, width), full2)],
        out_specs=[pl.BlockSpec((tm, width), lambda b, t: (b * nt + t, 0)),
                   pl.BlockSpec((None, 1, width), lambda b, t: (b, 0, 0))],
        out_shape=[jax.ShapeDtypeStruct((nb * seq, width), F32),
                   jax.ShapeDtypeStruct((nb, 1, width), F32)],
        scratch_shapes=[pltpu.VMEM((tm + SUBLANE, width), F32),
                        pltpu.VMEM((tm, width), F32),
                        pltpu.VMEM((tm, width), F32),
                        pltpu.VMEM((1, width), F32)],
        compiler_params=pltpu.CompilerParams(dimension_semantics=("parallel", "arbitrary"),
                                             vmem_limit_bytes=V7X_VMEM_LIMIT_BYTES),
        name="lru_mix",
    )(gx, gx, gx, conv8, h0[:, None, :], conv_w, conv_b[None], w_r.astype(BF16), b_r.reshape(1, width),
      w_i.astype(BF16), b_i.reshape(1, width), lam[None])
    return y, hl[:, 0]


KEY_CHUNK = 128
TINY = 1e-30


def _bucket_of(dist):
    n = jnp.maximum(dist, 0)
    exact = REL_BUCKETS // 2
    log_ratio = jnp.log(jnp.maximum(n, 1).astype(F32) / exact) / math.log(REL_MAX_DIST / exact)
    large = jnp.minimum(exact + (log_ratio * (REL_BUCKETS - exact)).astype(jnp.int32), REL_BUCKETS - 1)
    return jnp.where(n < exact, n, large)


def _bias_from_buckets(bucket, tab_ref, h):
    out = jnp.full(bucket.shape, tab_ref[0, h], F32)
    for k in range(1, REL_BUCKETS):
        out = jnp.where(bucket == k, tab_ref[k, h], out)
    return out


def _head_rmsnorm(x, w_row):
    return x * lax.rsqrt(jnp.mean(x * x, axis=-1, keepdims=True) + NORM_EPS) * w_row


def _bias_tiles_kernel(tab_ref, o_ref, *, tq):
    i = lax.broadcasted_iota(jnp.int32, (tq, KEY_CHUNK), 0)
    j = lax.broadcasted_iota(jnp.int32, (tq, KEY_CHUNK), 1)
    buckets = [_bucket_of(i - j + KEY_CHUNK * k) for k in range(2)]
    nheads = o_ref.shape[1] // tq

    def body(h, carry):
        rows = pl.ds(pl.multiple_of(h * tq, SUBLANE), tq)
        for k in range(2):
            o_ref[k, rows, :] = _bias_from_buckets(buckets[k], tab_ref, h)
        o_ref[2, rows, :] = jnp.full((tq, KEY_CHUNK), tab_ref[REL_BUCKETS - 1, h], F32)
        return carry

    lax.fori_loop(0, nheads, body, 0)


def _bias_tiles(table, tq):
    nheads = table.shape[1]
    return pl.pallas_call(
        functools.partial(_bias_tiles_kernel, tq=tq),
        in_specs=[pl.BlockSpec(memory_space=pltpu.SMEM)],
        out_specs=pl.BlockSpec(memory_space=pltpu.VMEM),
        out_shape=jax.ShapeDtypeStruct((3, nheads * tq, KEY_CHUNK), F32),
        name="nsa_bias_tiles",
    )(table)


def _page_spec(src, lead, width, col_blk):
    if src.ndim == 2:
        return pl.BlockSpec((KEY_CHUNK, width), lambda b, p, t: (t[b, p], col_blk))
    nlead = len(lead)
    assert src.ndim == nlead + 3 and src.shape[-2] == KEY_CHUNK
    return pl.BlockSpec((None,) * (nlead + 1) + (KEY_CHUNK, width),
                        lambda b, p, t: tuple(lead) + (t[b, p], 0, col_blk))


def _compress_ab_kernel(tbl_ref, x_ref, pe_ref, w1_ref, o_ref, slab_ref, *, groups, d):
    half = CMP_LEN // 2
    nchunk = KEY_CHUNK // CMP_STRIDE
    for sg in range(2 * groups):
        slab_ref[sg] = x_ref[:, sg * d:(sg + 1) * d]
    for s in range(2):
        acc = [jnp.zeros((groups * nchunk, d), F32) for _ in range(2)]
        for l in range(half):
            x = jnp.concatenate(
                [slab_ref[s * groups + g, pl.ds(l, nchunk, stride=CMP_STRIDE), :] for g in range(groups)], axis=0)
            for part in range(2):
                ll = part * half + l
                xa = (x + pe_ref[s, ll:ll + 1, :]).astype(BF16)
                acc[part] = acc[part] + jnp.dot(xa, w1_ref[s, ll], preferred_element_type=F32)
        for part in range(2):
            for g in range(groups):
                o_ref[s, part, g] = acc[part][g * nchunk:(g + 1) * nchunk]


def _compress_ab(src, lead, tbl, pe, w1b, groups, d):
    nb, npages = tbl.shape
    nchunk = KEY_CHUNK // CMP_STRIDE
    assert CMP_LEN == 2 * CMP_STRIDE
    grid_spec = pltpu.PrefetchScalarGridSpec(
        num_scalar_prefetch=1, grid=(nb, npages),
        in_specs=[_page_spec(src, lead, 2 * groups * d, 0),
                  pl.BlockSpec(pe.shape, lambda b, p, t: (0, 0, 0)),
                  pl.BlockSpec(w1b.shape, lambda b, p, t: (0, 0, 0, 0))],
        out_specs=pl.BlockSpec((None, 2, 2, groups, nchunk, d), lambda b, p, t: (b, 0, 0, 0, p, 0)),
        scratch_shapes=[pltpu.VMEM((2 * groups, KEY_CHUNK, d), F32)])
    return pl.pallas_call(
        functools.partial(_compress_ab_kernel, groups=groups, d=d),
        grid_spec=grid_spec,
        out_shape=jax.ShapeDtypeStruct((nb, 2, 2, groups, npages * nchunk, d), F32),
        compiler_params=pltpu.CompilerParams(dimension_semantics=("parallel", "arbitrary")),
        name="nsa_compress_ab",
    )(tbl, src, pe, w1b)


def _compress_finish_kernel(ab_ref, w2_ref, kw_ref, o_ref, *, groups, d):
    n = ab_ref.shape[3]
    for s in range(2):
        for g in range(groups):
            first = ab_ref[s, 0, g]
            second = pltpu.roll(ab_ref[s, 1, g], n - 1, 0)
            hid = jax.nn.silu(first + second).astype(BF16)
            out = jnp.dot(hid, w2_ref[s], preferred_element_type=F32)
            if s == 0:
                out = _head_rmsnorm(out, kw_ref[...])
            o_ref[s, :, g * d:(g + 1) * d] = out.astype(o_ref.dtype)


def _compress_finish(ab, w2b, k_norm_row, groups, d):
    nb, _, _, _, n, _ = ab.shape
    return pl.pallas_call(
        functools.partial(_compress_finish_kernel, groups=groups, d=d),
        grid=(nb,),
        in_specs=[pl.BlockSpec((None, 2, 2, groups, n, d), lambda b: (b, 0, 0, 0, 0, 0)),
                  pl.BlockSpec(w2b.shape, lambda b: (0, 0, 0)),
                  pl.BlockSpec((1, d), lambda b: (0, 0))],
        out_specs=pl.BlockSpec((None, 2, n, groups * d), lambda b: (b, 0, 0, 0)),
        out_shape=jax.ShapeDtypeStruct((nb, 2, n, groups * d), BF16),
        compiler_params=pltpu.CompilerParams(dimension_semantics=("parallel",)),
        name="nsa_compress_finish",
    )(ab, w2b, k_norm_row)


def _select_kernel(tab_ref, q_ref, kcv_ref, qw_ref, ocmp_ref, msel_ref, *, tq, groups, hpg, d, q_pos0, n_cmp, n_sel,
                   nselp, scale):
    qi = pl.program_id(1)
    pos0 = q_pos0 + qi * tq
    nck = kcv_ref.shape[1]
    t_c = lax.broadcasted_iota(jnp.int32, (tq, nck), 0)
    n_c = lax.broadcasted_iota(jnp.int32, (tq, nck), 1)
    dist = pos0 + t_c - (n_c * CMP_STRIDE + CMP_LEN - 1)
    cmask = (dist >= 0) & (n_c < n_cmp)
    cmaskf = cmask.astype(F32)
    bucket = _bucket_of(dist)
    n_r = lax.broadcasted_iota(jnp.int32, (nck, nselp), 0)
    j_r = lax.broadcasted_iota(jnp.int32, (nck, nselp), 1)
    cover = ((n_r * CMP_STRIDE < j_r * SEL_LEN + SEL_LEN) & (n_r * CMP_STRIDE + CMP_LEN > j_r * SEL_LEN)
             & (n_r < n_cmp) & (j_r < n_sel)).astype(BF16)
    j_s = lax.broadcasted_iota(jnp.int32, (tq, nselp), 1)
    pos = pos0 + lax.broadcasted_iota(jnp.int32, (tq, nselp), 0)
    cur = pos // SEL_LEN
    visible = (j_s * SEL_LEN <= pos) & (j_s < n_sel)
    forced = ((j_s == 0) | (j_s == cur) | (j_s == cur - 1)).astype(F32)
    for g in range(groups):
        kc = kcv_ref[0, :, g * d:(g + 1) * d]
        vc = kcv_ref[1, :, g * d:(g + 1) * d]
        p_grp = jnp.zeros((tq, nck), F32)
        for e in range(hpg):
            h = g * hpg + e
            cols = slice(h * d, (h + 1) * d)
            qn = _head_rmsnorm(q_ref[:, cols], qw_ref[...]).astype(BF16)
            logits = lax.dot_general(qn, kc, (((1,), (1,)), ((), ())), preferred_element_type=F32) * scale
            logits = jnp.where(cmask, logits + _bias_from_buckets(bucket, tab_ref, h), NEG_INF)
            m = jnp.max(logits, axis=-1, keepdims=True)
            p = jnp.exp(logits - m)
            p = p / jnp.sum(p, axis=-1, keepdims=True) * cmaskf
            ocmp_ref[:, cols] = jnp.dot(p.astype(BF16), vc, preferred_element_type=F32)
            p_grp = p_grp + p
        p_sel = jnp.dot(p_grp.astype(BF16), cover, preferred_element_type=F32)
        score = jnp.where(visible, p_sel + FORCE_BONUS * forced, NEG_INF)
        cnt = jnp.zeros((tq, nselp), jnp.int32)
        for jp in range(n_sel):
            col = score[:, jp:jp + 1]
            beats = (col > score) | ((col == score) & (j_s > jp))
            cnt = cnt + beats.astype(jnp.int32)
        msel_ref[:, g * nselp:(g + 1) * nselp] = ((cnt < SEL_TOPK) & visible).astype(F32)


def _nsa_select(table, qp, row0, kcv, q_norm_row, *, nb, seq, tq, q_pos0, n_cmp, n_sel, groups, hpg, d):
    nq = seq // tq
    n_q = groups * hpg * d
    nselp = -(-n_sel // LANE) * LANE
    nck = kcv.shape[2]
    assert row0 % tq == 0
    blk0 = row0 // tq
    kern = functools.partial(_select_kernel, tq=tq, groups=groups, hpg=hpg, d=d, q_pos0=q_pos0, n_cmp=n_cmp,
                             n_sel=n_sel, nselp=nselp, scale=d ** -0.5)
    return pl.pallas_call(
        kern, grid=(nb, nq),
        in_specs=[pl.BlockSpec(memory_space=pltpu.SMEM),
                  pl.BlockSpec((tq, n_q), lambda b, q: (blk0 + b * nq + q, 0)),
                  pl.BlockSpec((None, 2, nck, groups * d), lambda b, q: (b, 0, 0, 0)),
                  pl.BlockSpec((1, d), lambda b, q: (0, 0))],
        out_specs=[pl.BlockSpec((tq, n_q), lambda b, q: (b * nq + q, 0)),
                   pl.BlockSpec((tq, groups * nselp), lambda b, q: (b * nq + q, 0))],
        out_shape=[jax.ShapeDtypeStruct((nb * seq, n_q), F32),
                   jax.ShapeDtypeStruct((nb * seq, groups * nselp), F32)],
        compiler_params=pltpu.CompilerParams(dimension_semantics=("parallel", "parallel"),
                                             vmem_limit_bytes=V7X_VMEM_LIMIT_BYTES),
        name="nsa_cmp_select",
    )(table, qp, kcv, q_norm_row)


def _kv_pack_kernel(tbl_ref, x_ref, kw_ref, k_ref, v_ref, *, groups, d):
    for g in range(groups):
        k = _head_rmsnorm(x_ref[:, g * d:(g + 1) * d], kw_ref[...])
        k_ref[:, g * d:(g + 1) * d] = k.astype(k_ref.dtype)
    v_ref[...] = x_ref[:, groups * d:].astype(v_ref.dtype)


def _kv_pack(src, lead, tbl, col_blk, k_norm_row, groups, d):
    nb, npages = tbl.shape
    gd = groups * d
    grid_spec = pltpu.PrefetchScalarGridSpec(
        num_scalar_prefetch=1, grid=(nb, npages),
        in_specs=[_page_spec(src, lead, 2 * gd, col_blk),
                  pl.BlockSpec((1, d), lambda b, p, t: (0, 0))],
        out_specs=[pl.BlockSpec((None, KEY_CHUNK, gd), lambda b, p, t: (b, p, 0)),
                   pl.BlockSpec((None, KEY_CHUNK, gd), lambda b, p, t: (b, p, 0))])
    return pl.pallas_call(
        functools.partial(_kv_pack_kernel, groups=groups, d=d),
        grid_spec=grid_spec,
        out_shape=[jax.ShapeDtypeStruct((nb, npages * KEY_CHUNK, gd), BF16)] * 2,
        compiler_params=pltpu.CompilerParams(dimension_semantics=("parallel", "parallel")),
        name="nsa_kv_pack",
    )(tbl, src, k_norm_row)


def _attend_kernel(tiles_ref, q_ref, ocmp_ref, gate_ref, msel_ref, ks_ref, vs_ref, kw_ref, vw_ref, qw_ref, o_ref,
                   qn_ref, ms_ref, ls_ref, as_ref, mw_ref, lw_ref, aw_ref, *, tq, groups, hpg, d, q_pos0,
                   win_chunk0, n_win_steps, nselp, scale):
    qi = pl.program_id(1)
    c = pl.program_id(2)
    pos0 = q_pos0 + qi * tq
    c_hi = (pos0 + tq - 1) // KEY_CHUNK
    rows_g = hpg * tq
    nheads = groups * hpg

    @pl.when(c == 0)
    def _():
        for h in range(nheads):
            qn_ref[h * tq:(h + 1) * tq, :] = _head_rmsnorm(q_ref[:, h * d:(h + 1) * d], qw_ref[...]).astype(BF16)
        for m_ref, l_ref, a_ref in ((ms_ref, ls_ref, as_ref), (mw_ref, lw_ref, aw_ref)):
            m_ref[...] = jnp.full(m_ref.shape, NEG_INF, F32)
            l_ref[...] = jnp.zeros(l_ref.shape, F32)
            a_ref[...] = jnp.zeros(a_ref.shape, F32)

    assert tq & (tq - 1) == 0
    t_idx = lax.broadcasted_iota(jnp.int32, (rows_g, KEY_CHUNK), 0) & (tq - 1)
    j_idx = lax.broadcasted_iota(jnp.int32, (rows_g, KEY_CHUNK), 1)

    def step(g, k_ref, v_ref, m_ref, l_ref, a_ref, chunk, mask):
        rows = slice(g * rows_g, (g + 1) * rows_g)
        delta = pos0 - chunk * KEY_CHUNK
        tile = jnp.clip(delta // KEY_CHUNK, 0, 2)
        q = qn_ref[rows, :]
        k = k_ref[:, g * d:(g + 1) * d]
        s = lax.dot_general(q, k, (((1,), (1,)), ((), ())), preferred_element_type=F32) * scale
        s = s + tiles_ref[tile, pl.ds(g * rows_g, rows_g), :]
        s = jnp.where(mask, s, NEG_INF)
        m_prev = m_ref[rows, :]
        m_new = jnp.maximum(m_prev, jnp.max(s, axis=-1, keepdims=True))
        alpha = jnp.exp(m_prev - m_new)
        p = jnp.where(mask, jnp.exp(s - m_new), 0.0)
        l_ref[rows, :] = alpha * l_ref[rows, :] + jnp.sum(p, axis=-1, keepdims=True)
        a_ref[rows, :] = alpha * a_ref[rows, :] + jnp.dot(p.astype(BF16), v_ref[:, g * d:(g + 1) * d],
                                                          preferred_element_type=F32)
        m_ref[rows, :] = m_new

    @pl.when(c <= c_hi)
    def _():
        dist = pos0 - c * KEY_CHUNK + t_idx - j_idx
        causal = dist >= 0
        blocks_per_chunk = KEY_CHUNK // SEL_LEN
        jr = lax.broadcasted_iota(jnp.int32, (nselp, KEY_CHUNK), 0)
        lr = lax.broadcasted_iota(jnp.int32, (nselp, KEY_CHUNK), 1)
        expand = (jr == c * blocks_per_chunk + lr // SEL_LEN).astype(BF16)
        for g in range(groups):
            ms = msel_ref[:, g * nselp:(g + 1) * nselp].astype(BF16)
            sel = jnp.dot(ms, expand, preferred_element_type=F32)
            sel = jnp.concatenate([sel] * hpg, axis=0) > 0.5
            step(g, ks_ref, vs_ref, ms_ref, ls_ref, as_ref, c, causal & sel)

    wchunk = c_hi - c

    @pl.when((c < n_win_steps) & (wchunk >= win_chunk0))
    def _():
        dist = pos0 - wchunk * KEY_CHUNK + t_idx - j_idx
        valid = (dist >= 0) & (dist < WINDOW)
        for g in range(groups):
            step(g, kw_ref, vw_ref, mw_ref, lw_ref, aw_ref, wchunk, valid)

    @pl.when(c == pl.num_programs(2) - 1)
    def _():
        gates = jax.nn.sigmoid(gate_ref[...])
        for h in range(nheads):
            r = slice(h * tq, (h + 1) * tq)
            cols = slice(h * d, (h + 1) * d)
            o_sel = as_ref[r, :] / jnp.maximum(ls_ref[r, :], TINY)
            o_win = aw_ref[r, :] / jnp.maximum(lw_ref[r, :], TINY)
            o_ref[:, cols] = (gates[:, 3 * h:3 * h + 1] * ocmp_ref[:, cols] + gates[:, 3 * h + 1:3 * h + 2] * o_sel
                              + gates[:, 3 * h + 2:3 * h + 3] * o_win)


def _nsa_attend(tiles, qp, gp, row0, ocmp, msel, ksel, vsel, kwin, vwin, q_norm_row, *, nb, seq, tq, q_pos0,
                win_chunk0, groups, hpg, d):
    nq = seq // tq
    n_q = groups * hpg * d
    gd = groups * d
    nheads = groups * hpg
    nselp = msel.shape[1] // groups
    nc = ksel.shape[1] // KEY_CHUNK
    nwc = kwin.shape[1] // KEY_CHUNK
    n_win_steps = WINDOW // KEY_CHUNK + 1
    ncg = max(nc, n_win_steps)
    assert row0 % tq == 0 and q_pos0 % KEY_CHUNK == 0 and (tq == KEY_CHUNK or nq == 1) and tq <= KEY_CHUNK
    blk0 = row0 // tq

    def c_hi(q):
        return (q_pos0 + q * tq + tq - 1) // KEY_CHUNK

    def sel_map(b, q, c):
        return (b, jnp.minimum(c, c_hi(q)), 0)

    def win_map(b, q, c):
        return (b, jnp.clip(c_hi(q) - jnp.minimum(c, n_win_steps - 1) - win_chunk0, 0, nwc - 1), 0)

    grp_map = lambda b, q, c: (b * nq + q, 0)
    all_map = lambda b, q, c: (blk0 + b * nq + q, 0)
    kern = functools.partial(_attend_kernel, tq=tq, groups=groups, hpg=hpg, d=d, q_pos0=q_pos0,
                             win_chunk0=win_chunk0, n_win_steps=n_win_steps, nselp=nselp, scale=d ** -0.5)
    return pl.pallas_call(
        kern, grid=(nb, nq, ncg),
        in_specs=[pl.BlockSpec(tiles.shape, lambda b, q, c: (0, 0, 0)),
                  pl.BlockSpec((tq, n_q), all_map),
                  pl.BlockSpec((tq, n_q), grp_map),
                  pl.BlockSpec((tq, gp.shape[1]), all_map),
                  pl.BlockSpec((tq, groups * nselp), grp_map),
                  pl.BlockSpec((None, KEY_CHUNK, gd), sel_map),
                  pl.BlockSpec((None, KEY_CHUNK, gd), sel_map),
                  pl.BlockSpec((None, KEY_CHUNK, gd), win_map),
                  pl.BlockSpec((None, KEY_CHUNK, gd), win_map),
                  pl.BlockSpec((1, d), lambda b, q, c: (0, 0))],
        out_specs=pl.BlockSpec((tq, n_q), grp_map),
        out_shape=jax.ShapeDtypeStruct((nb * seq, n_q), F32),
        scratch_shapes=[pltpu.VMEM((nheads * tq, d), BF16),
                        pltpu.VMEM((nheads * tq, 1), F32), pltpu.VMEM((nheads * tq, 1), F32),
                        pltpu.VMEM((nheads * tq, d), F32),
                        pltpu.VMEM((nheads * tq, 1), F32), pltpu.VMEM((nheads * tq, 1), F32),
                        pltpu.VMEM((nheads * tq, d), F32)],
        compiler_params=pltpu.CompilerParams(dimension_semantics=("parallel", "parallel", "arbitrary"),
                                             vmem_limit_bytes=V7X_VMEM_LIMIT_BYTES),
        name="nsa_attend",
    )(tiles, qp, ocmp, gp, msel, ksel, vsel, kwin, vwin, q_norm_row)


def _nsa_group(qp, kvp, gp, row0, nb, seq, q_pos0, past, win_past, w, groups, hpg, d):
    q_norm_w, k_norm_w, cmp_pe, cmp_w1, cmp_w2, table = w
    gd = groups * d
    tq = KEY_CHUNK if seq % KEY_CHUNK == 0 else seq
    assert tq % SUBLANE == 0
    w1b, w2b = cmp_w1.astype(BF16), cmp_w2.astype(BF16)
    if past is None:
        assert seq % KEY_CHUNK == 0 and row0 % KEY_CHUNK == 0
        npg = seq // KEY_CHUNK
        tbl = row0 // KEY_CHUNK + jnp.arange(nb * npg, dtype=jnp.int32).reshape(nb, npg)
        src = kvp
        ab = _compress_ab(src, (), tbl, cmp_pe, w1b, groups, d)
        ksel, vsel = _kv_pack(src, (), tbl, 1, k_norm_w[1:2], groups, d)
        kwin, vwin = _kv_pack(src, (), tbl, 2, k_norm_w[2:3], groups, d)
        tk = seq
        win_chunk0 = 0
    else:
        cache, lead, tbl = past
        npg = tbl.shape[1]
        past_len = npg * KEY_CHUNK
        assert q_pos0 == past_len and seq < CMP_STRIDE and seq <= KEY_CHUNK
        ab = _compress_ab(cache, lead, tbl, cmp_pe, w1b, groups, d)
        ksel_p, vsel_p = _kv_pack(cache, lead, tbl, 1, k_norm_w[1:2], groups, d)
        wb = win_past.shape[1]
        assert wb % KEY_CHUNK == 0 and wb <= WINDOW and (past_len - wb) % KEY_CHUNK == 0
        nwp = wb // KEY_CHUNK
        wtbl = jnp.arange(nb * nwp, dtype=jnp.int32).reshape(nb, nwp)
        kwin_p, vwin_p = _kv_pack(win_past.reshape(nb * nwp, KEY_CHUNK, 2 * gd), (), wtbl, 0, k_norm_w[2:3], groups, d)
        new = kvp[row0:row0 + nb * seq].reshape(nb, seq, kvp.shape[1])
        new = jnp.pad(new, ((0, 0), (0, KEY_CHUNK - seq), (0, 0)))
        ntbl = jnp.arange(nb, dtype=jnp.int32).reshape(nb, 1)
        ksel_n, vsel_n = _kv_pack(new, (), ntbl, 1, k_norm_w[1:2], groups, d)
        kwin_n, vwin_n = _kv_pack(new, (), ntbl, 2, k_norm_w[2:3], groups, d)
        ksel, vsel = jnp.concatenate([ksel_p, ksel_n], axis=1), jnp.concatenate([vsel_p, vsel_n], axis=1)
        kwin, vwin = jnp.concatenate([kwin_p, kwin_n], axis=1), jnp.concatenate([vwin_p, vwin_n], axis=1)
        tk = past_len + seq
        win_chunk0 = (past_len - wb) // KEY_CHUNK
    n_cmp = (tk - CMP_LEN) // CMP_STRIDE + 1
    n_sel = -(-tk // SEL_LEN)
    assert n_cmp <= ab.shape[4] - 1
    kcv = _compress_finish(ab, w2b, k_norm_w[0:1], groups, d)
    ocmp, msel = _nsa_select(table, qp, row0, kcv, q_norm_w[None], nb=nb, seq=seq, tq=tq, q_pos0=q_pos0,
                             n_cmp=n_cmp, n_sel=n_sel, groups=groups, hpg=hpg, d=d)
    tiles = _bias_tiles(table, tq)
    return _nsa_attend(tiles, qp, gp, row0, ocmp, msel, ksel, vsel, kwin, vwin, q_norm_w[None], nb=nb, seq=seq,
                       tq=tq, q_pos0=q_pos0, win_chunk0=win_chunk0, groups=groups, hpg=hpg, d=d)


def kernel(x_prompt, x_sample, state_ssd, state_ssd_conv, state_lru, state_lru_conv, cache_nsa_kv, cache_nsa_win, page_table, ffn_norm, w_ffn_in, w_ffn_out, mix_norm, ssd_w_in, ssd_conv_w, ssd_conv_b, ssd_dt_bias, ssd_a_log, ssd_d, ssd_norm, ssd_w_out, lru_w_in, lru_conv_w, lru_conv_b, lru_w_r, lru_b_r, lru_w_i, lru_b_i, lru_lambda, lru_w_out, nsa_w_in, nsa_q_norm, nsa_k_norm, nsa_cmp_pe, nsa_cmp_w1, nsa_cmp_w2, rel_bias_table, nsa_w_out):
    prm = dict(ffn_norm=ffn_norm, w_ffn_in=w_ffn_in, w_ffn_out=w_ffn_out)
    bp, sp, dm = x_prompt.shape
    bs, ss, _ = x_sample.shape
    mp, ms = bp * sp, bs * ss
    depth = mix_norm.shape[0]
    G = NSA_GROUPS
    E = NSA_HEADS // G
    d_head = dm // NSA_HEADS
    wb = cache_nsa_win.shape[2]
    n_pages = page_table.shape[1]
    page = cache_nsa_kv.shape[2]
    past_len = n_pages * page
    assert page == KEY_CHUNK

    def split(t):
        return t[:mp].reshape(bp, sp, -1), t[mp:].reshape(bs, ss, -1)

    def merge(tp, ts):
        return jnp.concatenate([tp.reshape(mp, -1), ts.reshape(ms, -1)], axis=0)

    x = merge(x_prompt, x_sample)
    outs = {k: [] for k in ('ssd_h_p', 'ssd_h_s', 'ssd_buf_p', 'ssd_buf_s', 'lru_h_p', 'lru_h_s',
                            'lru_buf_p', 'lru_buf_s', 'rows_p', 'rows_s', 'win_p', 'win_s')}
    for i in range(depth):
        kind, j = i % N_MIXERS, i // N_MIXERS
        x = _ffn(x, prm, i, 0)
        hn = _rmsnorm(x, mix_norm[i][None])
        if kind == 0:
            d_inner = ssd_w_out.shape[1]
            conv_dim = ssd_conv_w.shape[2]
            heads = ssd_dt_bias.shape[1]
            z = _matmul(hn, ssd_w_in, (j,), kdim=dm, n0=0, ndim=d_inner, tn=512, name="ssd_in_z")
            xbc = _matmul(hn, ssd_w_in, (j,), kdim=dm, n0=d_inner, ndim=conv_dim, tn=512, name="ssd_in_xbc")
            dt_raw = _matmul(hn, ssd_w_in, (j,), kdim=dm, n0=d_inner + conv_dim, ndim=heads, tn=heads,
                             name="ssd_in_dt")
            (z_p, z_s), (xbc_p, xbc_s), (dt_p, dt_s) = split(z), split(xbc), split(dt_raw)
            w = (ssd_conv_w[j], ssd_conv_b[j], ssd_dt_bias[j], ssd_a_log[j], ssd_d[j], ssd_norm[j])
            y_p, buf_p, st_p = _ssd_core(z_p, xbc_p, dt_p, jnp.zeros((bp,) + state_ssd_conv.shape[2:], F32),
                                         jnp.zeros((bp,) + state_ssd.shape[2:], F32), *w)
            y_s, buf_s, st_s = _ssd_core(z_s, xbc_s, dt_s, state_ssd_conv[j], state_ssd[j], *w)
            outs['ssd_h_p'].append(st_p); outs['ssd_h_s'].append(st_s)
            outs['ssd_buf_p'].append(buf_p); outs['ssd_buf_s'].append(buf_s)
            x = _out_proj(merge(y_p, y_s).astype(BF16), ssd_w_out, (j,), x, 1.0, "ssd_out")
        elif kind == 1:
            width = lru_w_out.shape[1]
            nbuf = lru_conv_w.shape[1] - 1
            gx = _matmul(hn, lru_w_in, (j,), kdim=dm, n0=0, ndim=2 * width, tn=512, name="lru_in")
            w = (lru_conv_w[j], lru_conv_b[j], lru_w_r[j], lru_b_r[j], lru_w_i[j], lru_b_i[j], lru_lambda[j])
            zero_buf = jnp.zeros((bp, nbuf, width), F32)
            y_p, st_p = _lru_mix(gx, 0, bp, sp, zero_buf, jnp.zeros((bp, width), F32), *w)
            y_s, st_s = _lru_mix(gx, mp, bs, ss, state_lru_conv[j], state_lru[j], *w)
            xb_p, xb_s = split(gx[:, width:])
            outs['lru_h_p'].append(st_p); outs['lru_h_s'].append(st_s)
            outs['lru_buf_p'].append(jnp.concatenate([zero_buf, xb_p], axis=1)[:, -nbuf:])
            outs['lru_buf_s'].append(jnp.concatenate([state_lru_conv[j], xb_s], axis=1)[:, -nbuf:])
            x = _out_proj(jnp.concatenate([y_p, y_s], axis=0).astype(BF16), lru_w_out, (j,), x, 1.0, "lru_out")
        else:
            n_q, n_rows, n_kv = NSA_HEADS * d_head, 4 * G * d_head, 6 * G * d_head
            qp = _matmul(hn, nsa_w_in, (j,), kdim=dm, n0=0, ndim=n_q, tn=512, name="nsa_in_q")
            kvp = _matmul(hn, nsa_w_in, (j,), kdim=dm, n0=n_q, ndim=n_kv, tn=512, name="nsa_in_kv")
            n_gate = nsa_w_in.shape[2] - n_q - n_kv
            w_gate = jnp.pad(nsa_w_in[j][None, :, n_q + n_kv:], ((0, 0), (0, 0), (0, LANE - n_gate)))
            gp = _matmul(hn, w_gate, (0,), kdim=dm, n0=0, ndim=LANE, tn=LANE, name="nsa_in_gate")[:, :n_gate]
            w = (nsa_q_norm[j], nsa_k_norm[j], nsa_cmp_pe[j], nsa_cmp_w1[j], nsa_cmp_w2[j], rel_bias_table)
            o_p = _nsa_group(qp, kvp, gp, 0, bp, sp, 0, None, None, w, G, E, d_head)
            cache = cache_nsa_kv.reshape(cache_nsa_kv.shape[:3] + (n_rows,))
            win_past = cache_nsa_win[j].reshape(bs, wb, 2 * G * d_head)
            o_s = _nsa_group(qp, kvp, gp, mp, bs, ss, past_len, (cache, (j,), page_table), win_past, w, G, E, d_head)
            rows_p, rows_s = split(kvp[:, :n_rows])
            kvw_p, kvw_s = split(kvp[:, n_rows:])
            outs['rows_p'].append(rows_p.reshape(bp, sp, 4, G, d_head))
            outs['rows_s'].append(rows_s.reshape(bs, ss, 4, G, d_head))
            win_p = jnp.concatenate([jnp.zeros((bp, wb, 2 * G * d_head), F32), kvw_p], axis=1)[:, -wb:]
            win_s = jnp.concatenate([win_past, kvw_s], axis=1)[:, -wb:]
            outs['win_p'].append(win_p.reshape(bp, wb, 2, G, d_head))
            outs['win_s'].append(win_s.reshape(bs, wb, 2, G, d_head))
            x = _out_proj(jnp.concatenate([o_p, o_s], axis=0).astype(BF16), nsa_w_out, (j,), x, 1.0, "nsa_out")
        x = _ffn(x, prm, i, 1)
    y_prompt, y_sample = x[:mp].reshape(bp, sp, dm), x[mp:].reshape(bs, ss, dm)
    st = {k: jnp.stack(v) for k, v in outs.items()}
    return (y_prompt, y_sample, st['ssd_h_p'], st['ssd_h_s'], st['ssd_buf_p'], st['ssd_buf_s'],
            st['lru_h_p'], st['lru_h_s'], st['lru_buf_p'], st['lru_buf_s'],
            st['rows_p'], st['rows_s'], st['win_p'], st['win_s'])
```

```python
import functools
import math

import jax
import jax.numpy as jnp
from jax import lax
from jax.experimental import pallas as pl
from jax.experimental.pallas import tpu as pltpu

F32 = jnp.float32
BF16 = jnp.bfloat16

NORM_EPS = 1e-6
N_MIXERS = 3
SSD_HEAD_DIM = 64
SSD_STATE = 128
SSD_GROUPS = 8
SSD_CHUNK = 128
LRU_BLOCKS = 16
LRU_C = 8.0
NSA_HEADS = 32
NSA_GROUPS = 4
CMP_LEN = 32
CMP_STRIDE = 16
SEL_LEN = 64
SEL_TOPK = 16
WINDOW = 512
SEL_QBLOCK = 64
WIN_QBLOCK = 128
FORCE_BONUS = 1e4
NEG_INF = -1e30
REL_BUCKETS = 32
REL_MAX_DIST = 128

V7X_VMEM_LIMIT_BYTES = 56 * 1024 * 1024
LANE = 128
SUBLANE = 8


def _row_tile(m, cap):
    best = None
    for t in range(16, min(m, cap) + 1, 16):
        if m % t == 0:
            best = t
    assert best is not None, m
    return best


def _rmsnorm_kernel(x_ref, w_ref, o_ref):
    x = x_ref[...]
    y = x * lax.rsqrt(jnp.mean(x * x, axis=-1, keepdims=True) + NORM_EPS)
    o_ref[...] = (y * w_ref[...]).astype(o_ref.dtype)


def _rmsnorm(x, w_row):
    m, d = x.shape
    tm = _row_tile(m, 512)
    return pl.pallas_call(
        _rmsnorm_kernel,
        grid=(m // tm,),
        in_specs=[pl.BlockSpec((tm, d), lambda i: (i, 0)),
                  pl.BlockSpec((1, d), lambda i: (0, 0))],
        out_specs=pl.BlockSpec((tm, d), lambda i: (i, 0)),
        out_shape=jax.ShapeDtypeStruct((m, d), BF16),
        compiler_params=pltpu.CompilerParams(dimension_semantics=("parallel",)),
        name="rmsnorm",
    )(x, w_row)


def _mm_kernel(a_ref, w_ref, o_ref):
    acc = jnp.dot(a_ref[...], w_ref[...].astype(BF16), preferred_element_type=F32)
    o_ref[...] = acc.astype(o_ref.dtype)


def _mm_res_kernel(a_ref, w_ref, r_ref, o_ref, *, scale):
    acc = jnp.dot(a_ref[...], w_ref[...].astype(BF16), preferred_element_type=F32)
    o_ref[...] = r_ref[...] + scale * acc


def _mm_swiglu_kernel(a_ref, wg_ref, wu_ref, o_ref):
    a = a_ref[...]
    g = jnp.dot(a, wg_ref[...].astype(BF16), preferred_element_type=F32)
    u = jnp.dot(a, wu_ref[...].astype(BF16), preferred_element_type=F32)
    o_ref[...] = (jax.nn.silu(g) * u).astype(o_ref.dtype)


def _w_spec(w, lead, kdim, tn, k_blk, n_blk0):
    nlead = len(lead)
    return pl.BlockSpec((None,) * nlead + (kdim, tn),
                        lambda i, j: tuple(lead) + (k_blk, n_blk0 + j))


def _matmul(a, w, lead, *, a_k0=0, w_k0=0, kdim, n0, ndim, tn, tm_cap=1408,
            res=None, scale=1.0, out_dtype=F32, name="matmul"):
    m = a.shape[0]
    tm = _row_tile(m, tm_cap)
    assert a_k0 % kdim == 0 and w_k0 % kdim == 0 and n0 % tn == 0 and ndim % tn == 0
    a_spec = pl.BlockSpec((tm, kdim), lambda i, j: (i, a_k0 // kdim))
    w_spec = _w_spec(w, lead, kdim, tn, w_k0 // kdim, n0 // tn)
    o_spec = pl.BlockSpec((tm, tn), lambda i, j: (i, j))
    grid = (m // tm, ndim // tn)
    params = pltpu.CompilerParams(dimension_semantics=("parallel", "parallel"),
                                  vmem_limit_bytes=V7X_VMEM_LIMIT_BYTES)
    if res is None:
        return pl.pallas_call(
            _mm_kernel, grid=grid, in_specs=[a_spec, w_spec], out_specs=o_spec,
            out_shape=jax.ShapeDtypeStruct((m, ndim), out_dtype),
            compiler_params=params, name=name)(a, w)
    return pl.pallas_call(
        functools.partial(_mm_res_kernel, scale=scale), grid=grid,
        in_specs=[a_spec, w_spec, o_spec], out_specs=o_spec,
        out_shape=jax.ShapeDtypeStruct((m, ndim), F32),
        compiler_params=params, name=name)(a, w, res)


def _matmul_swiglu(a, w, lead, *, tn=256, tm_cap=1408):
    m, k = a.shape
    f = w.shape[-1] // 2
    tm = _row_tile(m, tm_cap)
    assert f % tn == 0
    a_spec = pl.BlockSpec((tm, k), lambda i, j: (i, 0))
    wg_spec = _w_spec(w, lead, k, tn, 0, 0)
    wu_spec = _w_spec(w, lead, k, tn, 0, f // tn)
    return pl.pallas_call(
        _mm_swiglu_kernel, grid=(m // tm, f // tn),
        in_specs=[a_spec, wg_spec, wu_spec],
        out_specs=pl.BlockSpec((tm, tn), lambda i, j: (i, j)),
        out_shape=jax.ShapeDtypeStruct((m, f), BF16),
        compiler_params=pltpu.CompilerParams(dimension_semantics=("parallel", "parallel"),
                                             vmem_limit_bytes=V7X_VMEM_LIMIT_BYTES),
        name="ffn_in_swiglu")(a, w, w)


def _out_proj(a, w, lead, res, scale, name):
    k = a.shape[1]
    n = w.shape[-1]
    nsplit = 1
    while (k // nsplit) > 5632 or k % nsplit:
        nsplit += 1
    kdim = k // nsplit
    assert kdim % LANE == 0
    out = res
    for s in range(nsplit):
        out = _matmul(a, w, lead, a_k0=s * kdim, w_k0=s * kdim, kdim=kdim, n0=0, ndim=n, tn=256,
                      res=out, scale=scale, name=name)
    return out


def _ffn(x, prm, i, which):
    xn = _rmsnorm(x, prm['ffn_norm'][i, which][None])
    h = _matmul_swiglu(xn, prm['w_ffn_in'], (i, which))
    return _out_proj(h, prm['w_ffn_out'], (i, which), x, 0.5, "ffn_out")


def _split3(v):
    hi = v.astype(BF16)
    r1 = v - hi.astype(F32)
    mid = r1.astype(BF16)
    lo = (r1 - mid.astype(F32)).astype(BF16)
    return hi, mid, lo


def _expand(parts, e):
    out = jnp.dot(parts[0], e, preferred_element_type=F32)
    for p in parts[1:]:
        out = out + jnp.dot(p, e, preferred_element_type=F32)
    return out


def _ssd_kernel(xbc_ref, prev_ref, conv8_ref, z_ref, dtr_ref, h0_ref, cw_ref, cb_ref, dtb_ref, alog_ref, dx_ref,
                nw_ref, ex_ref, y_ref, h_ref, buf_ref, xdt_ref, xw_ref, yoff_ref, yg_ref, *, q, d_inner,
                groups, nstate, hd, valid):
    c = pl.program_id(1)
    gw = d_inner // groups
    hpg = gw // hd
    conv_dim = buf_ref.shape[1]

    @pl.when(c == 0)
    def _():
        h_ref[...] = h0_ref[...]

    buf_ref[0:SUBLANE, :] = jnp.where(c == 0, conv8_ref[...], prev_ref[...])
    buf_ref[SUBLANE:SUBLANE + q, :] = xbc_ref[...]
    nconv = cw_ref.shape[0]
    first = SUBLANE - (nconv - 1)
    cblk = 512
    for cbk in range(conv_dim // cblk):
        cols = slice(cbk * cblk, (cbk + 1) * cblk)
        s = buf_ref[pl.ds(first, q), cols] * cw_ref[0:1, cols]
        for k in range(1, nconv):
            s = s + buf_ref[pl.ds(first + k, q), cols] * cw_ref[k:k + 1, cols]
        buf_ref[SUBLANE:SUBLANE + q, cols] = jax.nn.silu(cb_ref[:, cols] + s)

    heads = dtr_ref.shape[1]
    row = lax.broadcasted_iota(jnp.int32, (q, heads), 0)
    dt = jax.nn.softplus(dtr_ref[...] + dtb_ref[...])
    if valid < q:
        dt = jnp.where(row < valid, dt, 0.0)
    cs = dt * (-jnp.exp(alog_ref[...]))
    sh = 1
    while sh < q:
        cs = cs + jnp.where(row >= sh, pltpu.roll(cs, sh, 0), 0.0)
        sh *= 2
    cs_t = cs.T
    cs_last = cs[q - 1:q, :]
    to_end = jnp.exp(cs_last - cs)
    dt_parts = _split3(dt)
    dtw_parts = _split3(dt * to_end)
    ecs_parts = _split3(jnp.exp(cs))
    dec_parts = _split3(jnp.broadcast_to(jnp.exp(cs_last), (SUBLANE, heads)))
    li = lax.broadcasted_iota(jnp.int32, (q, q), 0)
    si = lax.broadcasted_iota(jnp.int32, (q, q), 1)
    causal = li >= si
    lane = lax.broadcasted_iota(jnp.int32, (q, 2 * hd), 1)
    xs0 = SUBLANE
    for g in range(groups):
        gc = slice(g * gw, (g + 1) * gw)
        eg = ex_ref[:, gc]
        xs = buf_ref[xs0:xs0 + q, gc]
        xdt_ref[...] = (xs * _expand(dt_parts, eg)).astype(BF16)
        xw_ref[...] = (xs * _expand(dtw_parts, eg)).astype(BF16)
        bcol = d_inner + g * nstate
        ccol = d_inner + groups * nstate + g * nstate
        b_f = buf_ref[xs0:xs0 + q, bcol:bcol + nstate]
        bg = b_f.astype(BF16)
        bg_t = b_f.T.astype(BF16)
        cg = buf_ref[xs0:xs0 + q, ccol:ccol + nstate].astype(BF16)
        cbm = lax.dot_general(cg, bg, (((1,), (1,)), ((), ())), preferred_element_type=F32)
        h_t = h_ref[g]
        yoff_ref[...] = jnp.dot(cg, h_t.astype(BF16), preferred_element_type=F32) * _expand(ecs_parts, eg)
        st = jnp.dot(bg_t, xw_ref[...], preferred_element_type=F32)
        h_ref[g] = _expand(dec_parts, eg)[0:1, :] * h_t + st
        for p in range(hpg // 2):
            pc = slice(p * 2 * hd, (p + 1) * 2 * hd)
            ac = slice(g * gw + p * 2 * hd, g * gw + (p + 1) * 2 * hd)
            x_pair = xdt_ref[:, pc]
            ys = []
            for e in range(2):
                h = g * hpg + 2 * p + e
                seg = cs[:, h:h + 1] - cs_t[h:h + 1, :]
                m = (cbm * jnp.exp(jnp.where(causal, seg, NEG_INF))).astype(BF16)
                ys.append(jnp.dot(m, x_pair, preferred_element_type=F32))
            y = jnp.where(lane < hd, ys[0], ys[1]) + yoff_ref[:, pc]
            y = y + dx_ref[:, ac] * buf_ref[xs0:xs0 + q, ac]
            zz = z_ref[:, ac]
            yg_ref[:, pc] = y * (zz * jax.nn.sigmoid(zz))
        yg = yg_ref[...]
        yn = yg * lax.rsqrt(jnp.mean(yg * yg, axis=-1, keepdims=True) + NORM_EPS) * nw_ref[:, gc]
        y_ref[:, gc] = yn.astype(y_ref.dtype)


def _ssd_mix(xbc, z, dt_raw, row0, nb, nchunk, valid, conv_buf, h0t, conv_w, conv_b, dt_bias, a_log, d_skip, norm_w,
             groups, nstate, hd):
    q = SSD_CHUNK
    conv_dim = xbc.shape[1]
    d_inner = z.shape[1]
    heads = dt_raw.shape[1]
    gw = d_inner // groups
    assert row0 % q == 0 and conv_dim == d_inner + 2 * groups * nstate and heads * hd == d_inner
    assert (gw // hd) % 2 == 0 and 2 * hd == LANE and nstate == LANE and conv_dim % 512 == 0
    blk0 = row0 // q
    sub = q // SUBLANE
    nconv = conv_w.shape[0]
    conv8 = jnp.pad(conv_buf, ((0, 0), (SUBLANE - (nconv - 1), 0), (0, 0)))
    expand = (jnp.arange(heads)[:, None] == jnp.arange(d_inner)[None, :] // hd).astype(BF16)
    dx = jnp.repeat(d_skip, hd)[None]
    row_map = lambda b, c: (blk0 + b * nchunk + c, 0)
    full2 = lambda b, c: (0, 0)
    kern = functools.partial(_ssd_kernel, q=q, d_inner=d_inner, groups=groups, nstate=nstate, hd=hd, valid=valid)
    y, hl = pl.pallas_call(
        kern, grid=(nb, nchunk),
        in_specs=[pl.BlockSpec((q, conv_dim), row_map),
                  pl.BlockSpec((SUBLANE, conv_dim), lambda b, c: (jnp.maximum((blk0 + b * nchunk + c) * sub - 1, 0), 0)),
                  pl.BlockSpec((None, SUBLANE, conv_dim), lambda b, c: (b, 0, 0)),
                  pl.BlockSpec((q, d_inner), row_map),
                  pl.BlockSpec((q, heads), row_map),
                  pl.BlockSpec((None, groups, nstate, gw), lambda b, c: (b, 0, 0, 0), pipeline_mode=pl.Buffered(1)),
                  pl.BlockSpec(conv_w.shape, full2),
                  pl.BlockSpec((1, conv_dim), full2),
                  pl.BlockSpec((1, heads), full2),
                  pl.BlockSpec((1, heads), full2),
                  pl.BlockSpec((1, d_inner), full2),
                  pl.BlockSpec((1, d_inner), full2),
                  pl.BlockSpec((heads, d_inner), full2, pipeline_mode=pl.Buffered(1))],
        out_specs=[pl.BlockSpec((q, d_inner), lambda b, c: (b * nchunk + c, 0)),
                   pl.BlockSpec((None, groups, nstate, gw), lambda b, c: (b, 0, 0, 0))],
        out_shape=[jax.ShapeDtypeStruct((nb * nchunk * q, d_inner), BF16),
                   jax.ShapeDtypeStruct((nb, groups, nstate, gw), F32)],
        scratch_shapes=[pltpu.VMEM((q + SUBLANE, conv_dim), F32),
                        pltpu.VMEM((q, gw), BF16), pltpu.VMEM((q, gw), BF16),
                        pltpu.VMEM((q, gw), F32), pltpu.VMEM((q, gw), F32)],
        compiler_params=pltpu.CompilerParams(dimension_semantics=("parallel", "arbitrary"),
                                             vmem_limit_bytes=V7X_VMEM_LIMIT_BYTES),
        name="ssd_mix",
    )(xbc, xbc, conv8, z, dt_raw, h0t, conv_w, conv_b[None], dt_bias[None], a_log[None], dx, norm_w[None], expand)
    return y, hl


def _ssd_state_to_lanes(state, groups):
    nb, heads, hd, n = state.shape
    return state.reshape(nb, groups, heads // groups, hd, n).transpose(0, 1, 4, 2, 3).reshape(nb, groups, n, -1)


def _ssd_state_from_lanes(ht, heads, hd):
    nb, groups, n, gw = ht.shape
    return ht.reshape(nb, groups, n, heads // groups, hd).transpose(0, 1, 3, 4, 2).reshape(nb, heads, hd, n)


LRU_SCAN_LANES = 512


def _gelu_tanh(x):
    return 0.5 * x * (1.0 + jnp.tanh(math.sqrt(2.0 / math.pi) * (x + 0.044715 * (x * x * x))))


def _lru_kernel(gate_ref, xb_ref, prev_ref, conv8_ref, h0_ref, cw_ref, cb_ref, wr_ref, br_ref, wi_ref, bi_ref,
                lam_ref, y_ref, hl_ref, e_ref, a_ref, u_ref, h_ref, *, tm, width):
    t = pl.program_id(1)

    @pl.when(t == 0)
    def _():
        h_ref[...] = h0_ref[...]

    e_ref[0:SUBLANE, :] = jnp.where(t == 0, conv8_ref[...], prev_ref[...])
    e_ref[SUBLANE:SUBLANE + tm, :] = xb_ref[...]
    nconv = cw_ref.shape[0]
    first = SUBLANE - (nconv - 1)
    nblk = wr_ref.shape[0]
    bd = width // nblk
    for k in range(nblk):
        cols = slice(k * bd, (k + 1) * bd)
        s = e_ref[pl.ds(first, tm), cols] * cw_ref[0:1, cols]
        for c in range(1, nconv):
            s = s + e_ref[pl.ds(first + c, tm), cols] * cw_ref[c:c + 1, cols]
        xk = cb_ref[:, cols] + s
        xkb = xk.astype(BF16)
        r = jax.nn.sigmoid(jnp.dot(xkb, wr_ref[k], preferred_element_type=F32) + br_ref[:, cols])
        ig = jax.nn.sigmoid(jnp.dot(xkb, wi_ref[k], preferred_element_type=F32) + bi_ref[:, cols])
        log_a = (-LRU_C * r) * jax.nn.softplus(-lam_ref[:, cols])
        a_ref[:, cols] = jnp.exp(log_a)
        one_minus_a2 = -jnp.tanh(log_a) * (jnp.exp(2.0 * log_a) + 1.0)
        u_ref[:, cols] = jnp.sqrt(one_minus_a2) * ig * xk

    row = lax.broadcasted_iota(jnp.int32, (SUBLANE, LRU_SCAN_LANES), 0)
    for sl in range(width // LRU_SCAN_LANES):
        cols = slice(sl * LRU_SCAN_LANES, (sl + 1) * LRU_SCAN_LANES)

        def body(r, h):
            rows = pl.ds(pl.multiple_of(r * SUBLANE, SUBLANE), SUBLANE)
            a = a_ref[rows, cols]
            u = u_ref[rows, cols]
            for sh in (1, 2, 4):
                a_sh = jnp.where(row >= sh, pltpu.roll(a, sh, 0), 1.0)
                u_sh = jnp.where(row >= sh, pltpu.roll(u, sh, 0), 0.0)
                u = a * u_sh + u
                a = a * a_sh
            hs = a * h + u
            y_ref[rows, cols] = hs * _gelu_tanh(gate_ref[rows, cols])
            return hs[SUBLANE - 1:SUBLANE, :]

        h_ref[:, cols] = lax.fori_loop(0, tm // SUBLANE, body, h_ref[:, cols])
    hl_ref[...] = h_ref[...]


def _lru_mix(gx, row0, nb, seq, conv_buf, h0, conv_w, conv_b, w_r, b_r, w_i, b_i, lam):
    width = gx.shape[1] // 2
    tm = _row_tile(seq, 128) if seq % 16 == 0 else seq
    assert seq % tm == 0 and tm % SUBLANE == 0 and row0 % tm == 0 and width % LRU_SCAN_LANES == 0
    nt = seq // tm
    nconv = conv_w.shape[0]
    conv8 = jnp.pad(conv_buf, ((0, 0), (SUBLANE - (nconv - 1), 0), (0, 0)))
    blk0 = row0 // tm
    sub = tm // SUBLANE

    def row_map(b, t):
        return (blk0 + b * nt + t, 0)

    def xb_map(b, t):
        return (blk0 + b * nt + t, 1)

    def prev_map(b, t):
        return (jnp.maximum((blk0 + b * nt + t) * sub - 1, 0), 1)

    full2 = lambda b, t: (0, 0)
    full3 = lambda b, t: (0, 0, 0)
    y, hl = pl.pallas_call(
        functools.partial(_lru_kernel, tm=tm, width=width),
        grid=(nb, nt),
        in_specs=[pl.BlockSpec((tm, width), row_map),
                  pl.BlockSpec((tm, width), xb_map),
                  pl.BlockSpec((SUBLANE, width), prev_map),
                  pl.BlockSpec((None, SUBLANE, width), lambda b, t: (b, 0, 0)),
                  pl.BlockSpec((None, 1, width), lambda b, t: (b, 0, 0)),
                  pl.BlockSpec(conv_w.shape, full2),
                  pl.BlockSpec((1, width), full2),
                  pl.BlockSpec(w_r.shape, full3),
                  pl.BlockSpec((1, width), full2),
                  pl.BlockSpec(w_i.shape, full3),
                  pl.BlockSpec((1, width), full2),
                  pl.BlockSpec((1, width), full2)],
        out_specs=[pl.BlockSpec((tm, width), lambda b, t: (b * nt + t, 0)),
                   pl.BlockSpec((None, 1, width), lambda b, t: (b, 0, 0))],
        out_shape=[jax.ShapeDtypeStruct((nb * seq, width), F32),
                   jax.ShapeDtypeStruct((nb, 1, width), F32)],
        scratch_shapes=[pltpu.VMEM((tm + SUBLANE, width), F32),
                        pltpu.VMEM((tm, width), F32),
                        pltpu.VMEM((tm, width), F32),
                        pltpu.VMEM((1, width), F32)],
        compiler_params=pltpu.CompilerParams(dimension_semantics=("parallel", "arbitrary"),
                                             vmem_limit_bytes=V7X_VMEM_LIMIT_BYTES),
        name="lru_mix",
    )(gx, gx, gx, conv8, h0[:, None, :], conv_w, conv_b[None], w_r.astype(BF16), b_r.reshape(1, width),
      w_i.astype(BF16), b_i.reshape(1, width), lam[None])
    return y, hl[:, 0]


KEY_CHUNK = 128
TINY = 1e-30


def _bucket_of(dist):
    n = jnp.maximum(dist, 0)
    exact = REL_BUCKETS // 2
    log_ratio = jnp.log(jnp.maximum(n, 1).astype(F32) / exact) / math.log(REL_MAX_DIST / exact)
    large = jnp.minimum(exact + (log_ratio * (REL_BUCKETS - exact)).astype(jnp.int32), REL_BUCKETS - 1)
    return jnp.where(n < exact, n, large)


def _bias_from_buckets(bucket, tab_ref, h):
    out = jnp.full(bucket.shape, tab_ref[0, h], F32)
    for k in range(1, REL_BUCKETS):
        out = jnp.where(bucket == k, tab_ref[k, h], out)
    return out


def _head_rmsnorm(x, w_row):
    return x * lax.rsqrt(jnp.mean(x * x, axis=-1, keepdims=True) + NORM_EPS) * w_row


def _bias_tiles_kernel(tab_ref, o_ref, *, tq):
    i = lax.broadcasted_iota(jnp.int32, (tq, KEY_CHUNK), 0)
    j = lax.broadcasted_iota(jnp.int32, (tq, KEY_CHUNK), 1)
    buckets = [_bucket_of(i - j + KEY_CHUNK * k) for k in range(2)]
    nheads = o_ref.shape[1] // tq

    def body(h, carry):
        rows = pl.ds(pl.multiple_of(h * tq, SUBLANE), tq)
        for k in range(2):
            o_ref[k, rows, :] = _bias_from_buckets(buckets[k], tab_ref, h)
        o_ref[2, rows, :] = jnp.full((tq, KEY_CHUNK), tab_ref[REL_BUCKETS - 1, h], F32)
        return carry

    lax.fori_loop(0, nheads, body, 0)


def _bias_tiles(table, tq):
    nheads = table.shape[1]
    return pl.pallas_call(
        functools.partial(_bias_tiles_kernel, tq=tq),
        in_specs=[pl.BlockSpec(memory_space=pltpu.SMEM)],
        out_specs=pl.BlockSpec(memory_space=pltpu.VMEM),
        out_shape=jax.ShapeDtypeStruct((3, nheads * tq, KEY_CHUNK), F32),
        name="nsa_bias_tiles",
    )(table)


def _page_spec(src, lead, width, col_blk):
    if src.ndim == 2:
        return pl.BlockSpec((KEY_CHUNK, width), lambda b, p, t: (t[b, p], col_blk))
    nlead = len(lead)
    assert src.ndim == nlead + 3 and src.shape[-2] == KEY_CHUNK
    return pl.BlockSpec((None,) * (nlead + 1) + (KEY_CHUNK, width),
                        lambda b, p, t: tuple(lead) + (t[b, p], 0, col_blk))


def _compress_ab_kernel(tbl_ref, x_ref, pe_ref, w1_ref, o_ref, slab_ref, *, groups, d):
    half = CMP_LEN // 2
    nchunk = KEY_CHUNK // CMP_STRIDE
    for sg in range(2 * groups):
        slab_ref[sg] = x_ref[:, sg * d:(sg + 1) * d]
    for s in range(2):
        acc = [jnp.zeros((groups * nchunk, d), F32) for _ in range(2)]
        for l in range(half):
            x = jnp.concatenate(
                [slab_ref[s * groups + g, pl.ds(l, nchunk, stride=CMP_STRIDE), :] for g in range(groups)], axis=0)
            for part in range(2):
                ll = part * half + l
                xa = (x + pe_ref[s, ll:ll + 1, :]).astype(BF16)
                acc[part] = acc[part] + jnp.dot(xa, w1_ref[s, ll], preferred_element_type=F32)
        for part in range(2):
            for g in range(groups):
                o_ref[s, part, g] = acc[part][g * nchunk:(g + 1) * nchunk]


def _compress_ab(src, lead, tbl, pe, w1b, groups, d):
    nb, npages = tbl.shape
    nchunk = KEY_CHUNK // CMP_STRIDE
    assert CMP_LEN == 2 * CMP_STRIDE
    grid_spec = pltpu.PrefetchScalarGridSpec(
        num_scalar_prefetch=1, grid=(nb, npages),
        in_specs=[_page_spec(src, lead, 2 * groups * d, 0),
                  pl.BlockSpec(pe.shape, lambda b, p, t: (0, 0, 0)),
                  pl.BlockSpec(w1b.shape, lambda b, p, t: (0, 0, 0, 0))],
        out_specs=pl.BlockSpec((None, 2, 2, groups, nchunk, d), lambda b, p, t: (b, 0, 0, 0, p, 0)),
        scratch_shapes=[pltpu.VMEM((2 * groups, KEY_CHUNK, d), F32)])
    return pl.pallas_call(
        functools.partial(_compress_ab_kernel, groups=groups, d=d),
        grid_spec=grid_spec,
        out_shape=jax.ShapeDtypeStruct((nb, 2, 2, groups, npages * nchunk, d), F32),
        compiler_params=pltpu.CompilerParams(dimension_semantics=("parallel", "arbitrary")),
        name="nsa_compress_ab",
    )(tbl, src, pe, w1b)


def _compress_finish_kernel(ab_ref, w2_ref, kw_ref, o_ref, *, groups, d):
    n = ab_ref.shape[3]
    for s in range(2):
        for g in range(groups):
            first = ab_ref[s, 0, g]
            second = pltpu.roll(ab_ref[s, 1, g], n - 1, 0)
            hid = jax.nn.silu(first + second).astype(BF16)
            out = jnp.dot(hid, w2_ref[s], preferred_element_type=F32)
            if s == 0:
                out = _head_rmsnorm(out, kw_ref[...])
            o_ref[s, :, g * d:(g + 1) * d] = out.astype(o_ref.dtype)


def _compress_finish(ab, w2b, k_norm_row, groups, d):
    nb, _, _, _, n, _ = ab.shape
    return pl.pallas_call(
        functools.partial(_compress_finish_kernel, groups=groups, d=d),
        grid=(nb,),
        in_specs=[pl.BlockSpec((None, 2, 2, groups, n, d), lambda b: (b, 0, 0, 0, 0, 0)),
                  pl.BlockSpec(w2b.shape, lambda b: (0, 0, 0)),
                  pl.BlockSpec((1, d), lambda b: (0, 0))],
        out_specs=pl.BlockSpec((None, 2, n, groups * d), lambda b: (b, 0, 0, 0)),
        out_shape=jax.ShapeDtypeStruct((nb, 2, n, groups * d), BF16),
        compiler_params=pltpu.CompilerParams(dimension_semantics=("parallel",)),
        name="nsa_compress_finish",
    )(ab, w2b, k_norm_row)


def _select_kernel(tab_ref, q_ref, kcv_ref, qw_ref, ocmp_ref, msel_ref, *, tq, groups, hpg, d, q_pos0, n_cmp, n_sel,
                   nselp, scale):
    qi = pl.program_id(1)
    pos0 = q_pos0 + qi * tq
    nck = kcv_ref.shape[1]
    t_c = lax.broadcasted_iota(jnp.int32, (tq, nck), 0)
    n_c = lax.broadcasted_iota(jnp.int32, (tq, nck), 1)
    dist = pos0 + t_c - (n_c * CMP_STRIDE + CMP_LEN - 1)
    cmask = (dist >= 0) & (n_c < n_cmp)
    cmaskf = cmask.astype(F32)
    bucket = _bucket_of(dist)
    n_r = lax.broadcasted_iota(jnp.int32, (nck, nselp), 0)
    j_r = lax.broadcasted_iota(jnp.int32, (nck, nselp), 1)
    cover = ((n_r * CMP_STRIDE < j_r * SEL_LEN + SEL_LEN) & (n_r * CMP_STRIDE + CMP_LEN > j_r * SEL_LEN)
             & (n_r < n_cmp) & (j_r < n_sel)).astype(BF16)
    j_s = lax.broadcasted_iota(jnp.int32, (tq, nselp), 1)
    pos = pos0 + lax.broadcasted_iota(jnp.int32, (tq, nselp), 0)
    cur = pos // SEL_LEN
    visible = (j_s * SEL_LEN <= pos) & (j_s < n_sel)
    forced = ((j_s == 0) | (j_s == cur) | (j_s == cur - 1)).astype(F32)
    for g in range(groups):
        kc = kcv_ref[0, :, g * d:(g + 1) * d]
        vc = kcv_ref[1, :, g * d:(g + 1) * d]
        p_grp = jnp.zeros((tq, nck), F32)
        for e in range(hpg):
            h = g * hpg + e
            cols = slice(h * d, (h + 1) * d)
            qn = _head_rmsnorm(q_ref[:, cols], qw_ref[...]).astype(BF16)
            logits = lax.dot_general(qn, kc, (((1,), (1,)), ((), ())), preferred_element_type=F32) * scale
            logits = jnp.where(cmask, logits + _bias_from_buckets(bucket, tab_ref, h), NEG_INF)
            m = jnp.max(logits, axis=-1, keepdims=True)
            p = jnp.exp(logits - m)
            p = p / jnp.sum(p, axis=-1, keepdims=True) * cmaskf
            ocmp_ref[:, cols] = jnp.dot(p.astype(BF16), vc, preferred_element_type=F32)
            p_grp = p_grp + p
        p_sel = jnp.dot(p_grp.astype(BF16), cover, preferred_element_type=F32)
        score = jnp.where(visible, p_sel + FORCE_BONUS * forced, NEG_INF)
        cnt = jnp.zeros((tq, nselp), jnp.int32)
        for jp in range(n_sel):
            col = score[:, jp:jp + 1]
            beats = (col > score) | ((col == score) & (j_s > jp))
            cnt = cnt + beats.astype(jnp.int32)
        msel_ref[:, g * nselp:(g + 1) * nselp] = ((cnt < SEL_TOPK) & visible).astype(F32)


def _nsa_select(table, qp, row0, kcv, q_norm_row, *, nb, seq, tq, q_pos0, n_cmp, n_sel, groups, hpg, d):
    nq = seq // tq
    n_q = groups * hpg * d
    nselp = -(-n_sel // LANE) * LANE
    nck = kcv.shape[2]
    assert row0 % tq == 0
    blk0 = row0 // tq
    kern = functools.partial(_select_kernel, tq=tq, groups=groups, hpg=hpg, d=d, q_pos0=q_pos0, n_cmp=n_cmp,
                             n_sel=n_sel, nselp=nselp, scale=d ** -0.5)
    return pl.pallas_call(
        kern, grid=(nb, nq),
        in_specs=[pl.BlockSpec(memory_space=pltpu.SMEM),
                  pl.BlockSpec((tq, n_q), lambda b, q: (blk0 + b * nq + q, 0)),
                  pl.BlockSpec((None, 2, nck, groups * d), lambda b, q: (b, 0, 0, 0)),
                  pl.BlockSpec((1, d), lambda b, q: (0, 0))],
        out_specs=[pl.BlockSpec((tq, n_q), lambda b, q: (b * nq + q, 0)),
                   pl.BlockSpec((tq, groups * nselp), lambda b, q: (b * nq + q, 0))],
        out_shape=[jax.ShapeDtypeStruct((nb * seq, n_q), F32),
                   jax.ShapeDtypeStruct((nb * seq, groups * nselp), F32)],
        compiler_params=pltpu.CompilerParams(dimension_semantics=("parallel", "parallel"),
                                             vmem_limit_bytes=V7X_VMEM_LIMIT_BYTES),
        name="nsa_cmp_select",
    )(table, qp, kcv, q_norm_row)


def _kv_pack_kernel(tbl_ref, x_ref, kw_ref, k_ref, v_ref, *, groups, d):
    for g in range(groups):
        k = _head_rmsnorm(x_ref[:, g * d:(g + 1) * d], kw_ref[...])
        k_ref[:, g * d:(g + 1) * d] = k.astype(k_ref.dtype)
    v_ref[...] = x_ref[:, groups * d:].astype(v_ref.dtype)


def _kv_pack(src, lead, tbl, col_blk, k_norm_row, groups, d):
    nb, npages = tbl.shape
    gd = groups * d
    grid_spec = pltpu.PrefetchScalarGridSpec(
        num_scalar_prefetch=1, grid=(nb, npages),
        in_specs=[_page_spec(src, lead, 2 * gd, col_blk),
                  pl.BlockSpec((1, d), lambda b, p, t: (0, 0))],
        out_specs=[pl.BlockSpec((None, KEY_CHUNK, gd), lambda b, p, t: (b, p, 0)),
                   pl.BlockSpec((None, KEY_CHUNK, gd), lambda b, p, t: (b, p, 0))])
    return pl.pallas_call(
        functools.partial(_kv_pack_kernel, groups=groups, d=d),
        grid_spec=grid_spec,
        out_shape=[jax.ShapeDtypeStruct((nb, npages * KEY_CHUNK, gd), BF16)] * 2,
        compiler_params=pltpu.CompilerParams(dimension_semantics=("parallel", "parallel")),
        name="nsa_kv_pack",
    )(tbl, src, k_norm_row)


def _attend_kernel(tiles_ref, q_ref, ocmp_ref, gate_ref, msel_ref, ks_ref, vs_ref, kw_ref, vw_ref, qw_ref, o_ref,
                   qn_ref, ms_ref, ls_ref, as_ref, mw_ref, lw_ref, aw_ref, *, tq, groups, hpg, d, q_pos0,
                   win_chunk0, n_win_steps, nselp, scale):
    qi = pl.program_id(1)
    c = pl.program_id(2)
    pos0 = q_pos0 + qi * tq
    c_hi = (pos0 + tq - 1) // KEY_CHUNK
    rows_g = hpg * tq
    nheads = groups * hpg

    @pl.when(c == 0)
    def _():
        for h in range(nheads):
            qn_ref[h * tq:(h + 1) * tq, :] = _head_rmsnorm(q_ref[:, h * d:(h + 1) * d], qw_ref[...]).astype(BF16)
        for m_ref, l_ref, a_ref in ((ms_ref, ls_ref, as_ref), (mw_ref, lw_ref, aw_ref)):
            m_ref[...] = jnp.full(m_ref.shape, NEG_INF, F32)
            l_ref[...] = jnp.zeros(l_ref.shape, F32)
            a_ref[...] = jnp.zeros(a_ref.shape, F32)

    assert tq & (tq - 1) == 0 and d == LANE
    t_idx = lax.broadcasted_iota(jnp.int32, (rows_g, KEY_CHUNK), 0) & (tq - 1)
    j_idx = lax.broadcasted_iota(jnp.int32, (rows_g, KEY_CHUNK), 1)

    ones_v = jnp.ones((KEY_CHUNK, d), BF16)

    def step(g, k_ref, v_ref, m_ref, l_ref, a_ref, chunk, mask):
        rows = slice(g * rows_g, (g + 1) * rows_g)
        delta = pos0 - chunk * KEY_CHUNK
        tile = jnp.clip(delta // KEY_CHUNK, 0, 2)
        q = qn_ref[rows, :]
        k = k_ref[:, g * d:(g + 1) * d]
        s = lax.dot_general(q, k, (((1,), (1,)), ((), ())), preferred_element_type=F32) * scale
        s = s + tiles_ref[tile, pl.ds(g * rows_g, rows_g), :]
        s = jnp.where(mask, s, NEG_INF)
        m_prev = m_ref[rows, :]
        m_new = jnp.maximum(m_prev, jnp.max(s, axis=-1, keepdims=True))
        alpha = jnp.exp(m_prev - m_new)
        p = jnp.where(mask, jnp.exp(s - m_new), 0.0).astype(BF16)
        v_ext = jnp.concatenate([v_ref[:, g * d:(g + 1) * d], ones_v], axis=1)
        pv = jnp.dot(p, v_ext, preferred_element_type=F32)
        l_ref[rows, :] = alpha * l_ref[rows, :] + pv[:, d:]
        a_ref[rows, :] = alpha * a_ref[rows, :] + pv[:, :d]
        m_ref[rows, :] = m_new

    @pl.when(c <= c_hi)
    def _():
        dist = pos0 - c * KEY_CHUNK + t_idx - j_idx
        causal = dist >= 0
        blocks_per_chunk = KEY_CHUNK // SEL_LEN
        jr = lax.broadcasted_iota(jnp.int32, (nselp, KEY_CHUNK), 0)
        lr = lax.broadcasted_iota(jnp.int32, (nselp, KEY_CHUNK), 1)
        expand = (jr == c * blocks_per_chunk + lr // SEL_LEN).astype(BF16)
        for g in range(groups):
            ms = msel_ref[:, g * nselp:(g + 1) * nselp].astype(BF16)
            sel = jnp.dot(ms, expand, preferred_element_type=F32)
            sel = jnp.concatenate([sel] * hpg, axis=0) > 0.5
            step(g, ks_ref, vs_ref, ms_ref, ls_ref, as_ref, c, causal & sel)

    wchunk = c_hi - c

    @pl.when((c < n_win_steps) & (wchunk >= win_chunk0))
    def _():
        dist = pos0 - wchunk * KEY_CHUNK + t_idx - j_idx
        valid = (dist >= 0) & (dist < WINDOW)
        for g in range(groups):
            step(g, kw_ref, vw_ref, mw_ref, lw_ref, aw_ref, wchunk, valid)

    @pl.when(c == pl.num_programs(2) - 1)
    def _():
        gates = jax.nn.sigmoid(gate_ref[...])
        for h in range(nheads):
            r = slice(h * tq, (h + 1) * tq)
            cols = slice(h * d, (h + 1) * d)
            o_sel = as_ref[r, :] / jnp.maximum(ls_ref[r, :], TINY)
            o_win = aw_ref[r, :] / jnp.maximum(lw_ref[r, :], TINY)
            o_ref[:, cols] = (gates[:, 3 * h:3 * h + 1] * ocmp_ref[:, cols] + gates[:, 3 * h + 1:3 * h + 2] * o_sel
                              + gates[:, 3 * h + 2:3 * h + 3] * o_win)


def _nsa_attend(tiles, qp, gp, row0, ocmp, msel, ksel, vsel, kwin, vwin, q_norm_row, *, nb, seq, tq, q_pos0,
                win_chunk0, groups, hpg, d):
    nq = seq // tq
    n_q = groups * hpg * d
    gd = groups * d
    nheads = groups * hpg
    nselp = msel.shape[1] // groups
    nc = ksel.shape[1] // KEY_CHUNK
    nwc = kwin.shape[1] // KEY_CHUNK
    n_win_steps = WINDOW // KEY_CHUNK + 1
    ncg = max(nc, n_win_steps)
    assert row0 % tq == 0 and q_pos0 % KEY_CHUNK == 0 and (tq == KEY_CHUNK or nq == 1) and tq <= KEY_CHUNK
    blk0 = row0 // tq

    def c_hi(q):
        return (q_pos0 + q * tq + tq - 1) // KEY_CHUNK

    def sel_map(b, q, c):
        return (b, jnp.minimum(c, c_hi(q)), 0)

    def win_map(b, q, c):
        return (b, jnp.clip(c_hi(q) - jnp.minimum(c, n_win_steps - 1) - win_chunk0, 0, nwc - 1), 0)

    grp_map = lambda b, q, c: (b * nq + q, 0)
    all_map = lambda b, q, c: (blk0 + b * nq + q, 0)
    kern = functools.partial(_attend_kernel, tq=tq, groups=groups, hpg=hpg, d=d, q_pos0=q_pos0,
                             win_chunk0=win_chunk0, n_win_steps=n_win_steps, nselp=nselp, scale=d ** -0.5)
    return pl.pallas_call(
        kern, grid=(nb, nq, ncg),
        in_specs=[pl.BlockSpec(tiles.shape, lambda b, q, c: (0, 0, 0)),
                  pl.BlockSpec((tq, n_q), all_map),
                  pl.BlockSpec((tq, n_q), grp_map),
                  pl.BlockSpec((tq, gp.shape[1]), all_map),
                  pl.BlockSpec((tq, groups * nselp), grp_map),
                  pl.BlockSpec((None, KEY_CHUNK, gd), sel_map),
                  pl.BlockSpec((None, KEY_CHUNK, gd), sel_map),
                  pl.BlockSpec((None, KEY_CHUNK, gd), win_map),
                  pl.BlockSpec((None, KEY_CHUNK, gd), win_map),
                  pl.BlockSpec((1, d), lambda b, q, c: (0, 0))],
        out_specs=pl.BlockSpec((tq, n_q), grp_map),
        out_shape=jax.ShapeDtypeStruct((nb * seq, n_q), F32),
        scratch_shapes=[pltpu.VMEM((nheads * tq, d), BF16),
                        pltpu.VMEM((nheads * tq, d), F32), pltpu.VMEM((nheads * tq, d), F32),
                        pltpu.VMEM((nheads * tq, d), F32),
                        pltpu.VMEM((nheads * tq, d), F32), pltpu.VMEM((nheads * tq, d), F32),
                        pltpu.VMEM((nheads * tq, d), F32)],
        compiler_params=pltpu.CompilerParams(dimension_semantics=("parallel", "parallel", "arbitrary"),
                                             vmem_limit_bytes=V7X_VMEM_LIMIT_BYTES),
        name="nsa_attend",
    )(tiles, qp, ocmp, gp, msel, ksel, vsel, kwin, vwin, q_norm_row)


def _nsa_group(qp, kvp, gp, row0, nb, seq, q_pos0, past, win_past, w, groups, hpg, d):
    q_norm_w, k_norm_w, cmp_pe, cmp_w1, cmp_w2, table = w
    gd = groups * d
    tq = KEY_CHUNK if seq % KEY_CHUNK == 0 else seq
    assert tq % SUBLANE == 0
    w1b, w2b = cmp_w1.astype(BF16), cmp_w2.astype(BF16)
    if past is None:
        assert seq % KEY_CHUNK == 0 and row0 % KEY_CHUNK == 0
        npg = seq // KEY_CHUNK
        tbl = row0 // KEY_CHUNK + jnp.arange(nb * npg, dtype=jnp.int32).reshape(nb, npg)
        src = kvp
        ab = _compress_ab(src, (), tbl, cmp_pe, w1b, groups, d)
        ksel, vsel = _kv_pack(src, (), tbl, 1, k_norm_w[1:2], groups, d)
        kwin, vwin = _kv_pack(src, (), tbl, 2, k_norm_w[2:3], groups, d)
        tk = seq
        win_chunk0 = 0
    else:
        cache, lead, tbl = past
        npg = tbl.shape[1]
        past_len = npg * KEY_CHUNK
        assert q_pos0 == past_len and seq < CMP_STRIDE and seq <= KEY_CHUNK
        ab = _compress_ab(cache, lead, tbl, cmp_pe, w1b, groups, d)
        ksel_p, vsel_p = _kv_pack(cache, lead, tbl, 1, k_norm_w[1:2], groups, d)
        wb = win_past.shape[1]
        assert wb % KEY_CHUNK == 0 and wb <= WINDOW and (past_len - wb) % KEY_CHUNK == 0
        nwp = wb // KEY_CHUNK
        wtbl = jnp.arange(nb * nwp, dtype=jnp.int32).reshape(nb, nwp)
        kwin_p, vwin_p = _kv_pack(win_past.reshape(nb * nwp, KEY_CHUNK, 2 * gd), (), wtbl, 0, k_norm_w[2:3], groups, d)
        new = kvp[row0:row0 + nb * seq].reshape(nb, seq, kvp.shape[1])
        new = jnp.pad(new, ((0, 0), (0, KEY_CHUNK - seq), (0, 0)))
        ntbl = jnp.arange(nb, dtype=jnp.int32).reshape(nb, 1)
        ksel_n, vsel_n = _kv_pack(new, (), ntbl, 1, k_norm_w[1:2], groups, d)
        kwin_n, vwin_n = _kv_pack(new, (), ntbl, 2, k_norm_w[2:3], groups, d)
        ksel, vsel = jnp.concatenate([ksel_p, ksel_n], axis=1), jnp.concatenate([vsel_p, vsel_n], axis=1)
        kwin, vwin = jnp.concatenate([kwin_p, kwin_n], axis=1), jnp.concatenate([vwin_p, vwin_n], axis=1)
        tk = past_len + seq
        win_chunk0 = (past_len - wb) // KEY_CHUNK
    n_cmp = (tk - CMP_LEN) // CMP_STRIDE + 1
    n_sel = -(-tk // SEL_LEN)
    assert n_cmp <= ab.shape[4] - 1
    kcv = _compress_finish(ab, w2b, k_norm_w[0:1], groups, d)
    ocmp, msel = _nsa_select(table, qp, row0, kcv, q_norm_w[None], nb=nb, seq=seq, tq=tq, q_pos0=q_pos0,
                             n_cmp=n_cmp, n_sel=n_sel, groups=groups, hpg=hpg, d=d)
    tiles = _bias_tiles(table, tq)
    return _nsa_attend(tiles, qp, gp, row0, ocmp, msel, ksel, vsel, kwin, vwin, q_norm_w[None], nb=nb, seq=seq,
                       tq=tq, q_pos0=q_pos0, win_chunk0=win_chunk0, groups=groups, hpg=hpg, d=d)


def kernel(x_prompt, x_sample, state_ssd, state_ssd_conv, state_lru, state_lru_conv, cache_nsa_kv, cache_nsa_win, page_table, ffn_norm, w_ffn_in, w_ffn_out, mix_norm, ssd_w_in, ssd_conv_w, ssd_conv_b, ssd_dt_bias, ssd_a_log, ssd_d, ssd_norm, ssd_w_out, lru_w_in, lru_conv_w, lru_conv_b, lru_w_r, lru_b_r, lru_w_i, lru_b_i, lru_lambda, lru_w_out, nsa_w_in, nsa_q_norm, nsa_k_norm, nsa_cmp_pe, nsa_cmp_w1, nsa_cmp_w2, rel_bias_table, nsa_w_out):
    prm = dict(ffn_norm=ffn_norm, w_ffn_in=w_ffn_in, w_ffn_out=w_ffn_out)
    bp, sp, dm = x_prompt.shape
    bs, ss, _ = x_sample.shape
    mp, ms = bp * sp, bs * ss
    depth = mix_norm.shape[0]
    G = NSA_GROUPS
    E = NSA_HEADS // G
    d_head = dm // NSA_HEADS
    wb = cache_nsa_win.shape[2]
    n_pages = page_table.shape[1]
    page = cache_nsa_kv.shape[2]
    past_len = n_pages * page
    assert page == KEY_CHUNK

    def split(t):
        return t[:mp].reshape(bp, sp, -1), t[mp:].reshape(bs, ss, -1)

    def merge(tp, ts):
        return jnp.concatenate([tp.reshape(mp, -1), ts.reshape(ms, -1)], axis=0)

    x = merge(x_prompt, x_sample)
    outs = {k: [] for k in ('ssd_h_p', 'ssd_h_s', 'ssd_buf_p', 'ssd_buf_s', 'lru_h_p', 'lru_h_s',
                            'lru_buf_p', 'lru_buf_s', 'rows_p', 'rows_s', 'win_p', 'win_s')}
    for i in range(depth):
        kind, j = i % N_MIXERS, i // N_MIXERS
        x = _ffn(x, prm, i, 0)
        hn = _rmsnorm(x, mix_norm[i][None])
        if kind == 0:
            d_inner = ssd_w_out.shape[1]
            conv_dim = ssd_conv_w.shape[2]
            heads = ssd_dt_bias.shape[1]
            z = _matmul(hn, ssd_w_in, (j,), kdim=dm, n0=0, ndim=d_inner, tn=512, name="ssd_in_z")
            xbc = _matmul(hn, ssd_w_in, (j,), kdim=dm, n0=d_inner, ndim=conv_dim, tn=512, name="ssd_in_xbc")
            dt_raw = _matmul(hn, ssd_w_in, (j,), kdim=dm, n0=d_inner + conv_dim, ndim=heads, tn=heads,
                             name="ssd_in_dt")
            w = (ssd_conv_w[j], ssd_conv_b[j], ssd_dt_bias[j], ssd_a_log[j], ssd_d[j], ssd_norm[j])
            q = SSD_CHUNK
            assert sp % q == 0 and ss <= q
            nbuf = ssd_conv_w.shape[1] - 1
            hd, nstate = d_inner // heads, state_ssd.shape[-1]
            shape = (SSD_GROUPS, nstate, hd)
            zero_buf = jnp.zeros((bp, nbuf, conv_dim), F32)
            zero_h = jnp.zeros((bp, SSD_GROUPS, nstate, d_inner // SSD_GROUPS), F32)
            y_p, ht_p = _ssd_mix(xbc, z, dt_raw, 0, bp, sp // q, q, zero_buf, zero_h, *w, *shape)

            def pad_s(t):
                return jnp.pad(t[mp:].reshape(bs, ss, -1), ((0, 0), (0, q - ss), (0, 0))).reshape(bs * q, -1)

            y_s, ht_s = _ssd_mix(pad_s(xbc), pad_s(z), pad_s(dt_raw), 0, bs, 1, ss, state_ssd_conv[j],
                                 _ssd_state_to_lanes(state_ssd[j], SSD_GROUPS), *w, *shape)
            y_s = y_s.reshape(bs, q, d_inner)[:, :ss].reshape(ms, d_inner)
            xbc_p, xbc_s = split(xbc)
            outs['ssd_h_p'].append(_ssd_state_from_lanes(ht_p, heads, hd))
            outs['ssd_h_s'].append(_ssd_state_from_lanes(ht_s, heads, hd))
            outs['ssd_buf_p'].append(jnp.concatenate([zero_buf, xbc_p], axis=1)[:, -nbuf:])
            outs['ssd_buf_s'].append(jnp.concatenate([state_ssd_conv[j], xbc_s], axis=1)[:, -nbuf:])
            x = _out_proj(jnp.concatenate([y_p, y_s], axis=0), ssd_w_out, (j,), x, 1.0, "ssd_out")
        elif kind == 1:
            width = lru_w_out.shape[1]
            nbuf = lru_conv_w.shape[1] - 1
            gx = _matmul(hn, lru_w_in, (j,), kdim=dm, n0=0, ndim=2 * width, tn=512, name="lru_in")
            w = (lru_conv_w[j], lru_conv_b[j], lru_w_r[j], lru_b_r[j], lru_w_i[j], lru_b_i[j], lru_lambda[j])
            zero_buf = jnp.zeros((bp, nbuf, width), F32)
            y_p, st_p = _lru_mix(gx, 0, bp, sp, zero_buf, jnp.zeros((bp, width), F32), *w)
            y_s, st_s = _lru_mix(gx, mp, bs, ss, state_lru_conv[j], state_lru[j], *w)
            xb_p, xb_s = split(gx[:, width:])
            outs['lru_h_p'].append(st_p); outs['lru_h_s'].append(st_s)
            outs['lru_buf_p'].append(jnp.concatenate([zero_buf, xb_p], axis=1)[:, -nbuf:])
            outs['lru_buf_s'].append(jnp.concatenate([state_lru_conv[j], xb_s], axis=1)[:, -nbuf:])
            x = _out_proj(jnp.concatenate([y_p, y_s], axis=0).astype(BF16), lru_w_out, (j,), x, 1.0, "lru_out")
        else:
            n_q, n_rows, n_kv = NSA_HEADS * d_head, 4 * G * d_head, 6 * G * d_head
            qp = _matmul(hn, nsa_w_in, (j,), kdim=dm, n0=0, ndim=n_q, tn=512, name="nsa_in_q")
            kvp = _matmul(hn, nsa_w_in, (j,), kdim=dm, n0=n_q, ndim=n_kv, tn=512, name="nsa_in_kv")
            n_gate = nsa_w_in.shape[2] - n_q - n_kv
            w_gate = jnp.pad(nsa_w_in[j][None, :, n_q + n_kv:], ((0, 0), (0, 0), (0, LANE - n_gate)))
            gp = _matmul(hn, w_gate, (0,), kdim=dm, n0=0, ndim=LANE, tn=LANE, name="nsa_in_gate")[:, :n_gate]
            w = (nsa_q_norm[j], nsa_k_norm[j], nsa_cmp_pe[j], nsa_cmp_w1[j], nsa_cmp_w2[j], rel_bias_table)
            o_p = _nsa_group(qp, kvp, gp, 0, bp, sp, 0, None, None, w, G, E, d_head)
            cache = cache_nsa_kv.reshape(cache_nsa_kv.shape[:3] + (n_rows,))
            win_past = cache_nsa_win[j].reshape(bs, wb, 2 * G * d_head)
            o_s = _nsa_group(qp, kvp, gp, mp, bs, ss, past_len, (cache, (j,), page_table), win_past, w, G, E, d_head)
            rows_p, rows_s = split(kvp[:, :n_rows])
            kvw_p, kvw_s = split(kvp[:, n_rows:])
            outs['rows_p'].append(rows_p.reshape(bp, sp, 4, G, d_head))
            outs['rows_s'].append(rows_s.reshape(bs, ss, 4, G, d_head))
            win_p = jnp.concatenate([jnp.zeros((bp, wb, 2 * G * d_head), F32), kvw_p], axis=1)[:, -wb:]
            win_s = jnp.concatenate([win_past, kvw_s], axis=1)[:, -wb:]
            outs['win_p'].append(win_p.reshape(bp, wb, 2, G, d_head))
            outs['win_s'].append(win_s.reshape(bs, wb, 2, G, d_head))
            x = _out_proj(jnp.concatenate([o_p, o_s], axis=0).astype(BF16), nsa_w_out, (j,), x, 1.0, "nsa_out")
        x = _ffn(x, prm, i, 1)
    y_prompt, y_sample = x[:mp].reshape(bp, sp, dm), x[mp:].reshape(bs, ss, dm)
    st = {k: jnp.stack(v) for k, v in outs.items()}
    return (y_prompt, y_sample, st['ssd_h_p'], st['ssd_h_s'], st['ssd_buf_p'], st['ssd_buf_s'],
            st['lru_h_p'], st['lru_h_s'], st['lru_buf_p'], st['lru_buf_s'],
            st['rows_p'], st['rows_s'], st['win_p'], st['win_s'])
```

```python
import functools
import math

import jax
import jax.numpy as jnp
from jax import lax
from jax.experimental import pallas as pl
from jax.experimental.pallas import tpu as pltpu

F32 = jnp.float32
BF16 = jnp.bfloat16

NORM_EPS = 1e-6
N_MIXERS = 3
SSD_HEAD_DIM = 64
SSD_STATE = 128
SSD_GROUPS = 8
SSD_CHUNK = 128
LRU_BLOCKS = 16
LRU_C = 8.0
NSA_HEADS = 32
NSA_GROUPS = 4
CMP_LEN = 32
CMP_STRIDE = 16
SEL_LEN = 64
SEL_TOPK = 16
WINDOW = 512
SEL_QBLOCK = 64
WIN_QBLOCK = 128
FORCE_BONUS = 1e4
NEG_INF = -1e30
REL_BUCKETS = 32
REL_MAX_DIST = 128

V7X_VMEM_LIMIT_BYTES = 56 * 1024 * 1024
LANE = 128
SUBLANE = 8


def _row_tile(m, cap):
    best = None
    for t in range(16, min(m, cap) + 1, 16):
        if m % t == 0:
            best = t
    assert best is not None, m
    return best


def _rmsnorm_kernel(x_ref, w_ref, o_ref):
    x = x_ref[...]
    y = x * lax.rsqrt(jnp.mean(x * x, axis=-1, keepdims=True) + NORM_EPS)
    o_ref[...] = (y * w_ref[...]).astype(o_ref.dtype)


def _rmsnorm(x, w_row):
    m, d = x.shape
    tm = _row_tile(m, 512)
    return pl.pallas_call(
        _rmsnorm_kernel,
        grid=(m // tm,),
        in_specs=[pl.BlockSpec((tm, d), lambda i: (i, 0)),
                  pl.BlockSpec((1, d), lambda i: (0, 0))],
        out_specs=pl.BlockSpec((tm, d), lambda i: (i, 0)),
        out_shape=jax.ShapeDtypeStruct((m, d), BF16),
        compiler_params=pltpu.CompilerParams(dimension_semantics=("parallel",)),
        name="rmsnorm",
    )(x, w_row)


def _mm_kernel(a_ref, w_ref, o_ref):
    acc = jnp.dot(a_ref[...], w_ref[...].astype(BF16), preferred_element_type=F32)
    o_ref[...] = acc.astype(o_ref.dtype)


def _mm_res_kernel(a_ref, w_ref, r_ref, o_ref, *, scale):
    acc = jnp.dot(a_ref[...], w_ref[...].astype(BF16), preferred_element_type=F32)
    o_ref[...] = r_ref[...] + scale * acc


def _mm_swiglu_kernel(a_ref, wg_ref, wu_ref, o_ref):
    a = a_ref[...]
    g = jnp.dot(a, wg_ref[...].astype(BF16), preferred_element_type=F32)
    u = jnp.dot(a, wu_ref[...].astype(BF16), preferred_element_type=F32)
    o_ref[...] = (jax.nn.silu(g) * u).astype(o_ref.dtype)


def _w_spec(w, lead, kdim, tn, k_blk, n_blk0):
    nlead = len(lead)
    return pl.BlockSpec((None,) * nlead + (kdim, tn),
                        lambda i, j: tuple(lead) + (k_blk, n_blk0 + j))


def _matmul(a, w, lead, *, a_k0=0, w_k0=0, kdim, n0, ndim, tn, tm_cap=1408,
            res=None, scale=1.0, out_dtype=F32, name="matmul"):
    m = a.shape[0]
    tm = _row_tile(m, tm_cap)
    assert a_k0 % kdim == 0 and w_k0 % kdim == 0 and n0 % tn == 0 and ndim % tn == 0
    a_spec = pl.BlockSpec((tm, kdim), lambda i, j: (i, a_k0 // kdim))
    w_spec = _w_spec(w, lead, kdim, tn, w_k0 // kdim, n0 // tn)
    o_spec = pl.BlockSpec((tm, tn), lambda i, j: (i, j))
    grid = (m // tm, ndim // tn)
    params = pltpu.CompilerParams(dimension_semantics=("parallel", "parallel"),
                                  vmem_limit_bytes=V7X_VMEM_LIMIT_BYTES)
    if res is None:
        return pl.pallas_call(
            _mm_kernel, grid=grid, in_specs=[a_spec, w_spec], out_specs=o_spec,
            out_shape=jax.ShapeDtypeStruct((m, ndim), out_dtype),
            compiler_params=params, name=name)(a, w)
    return pl.pallas_call(
        functools.partial(_mm_res_kernel, scale=scale), grid=grid,
        in_specs=[a_spec, w_spec, o_spec], out_specs=o_spec,
        out_shape=jax.ShapeDtypeStruct((m, ndim), F32),
        compiler_params=params, name=name)(a, w, res)


def _matmul_swiglu(a, w, lead, *, tn=256, tm_cap=1408):
    m, k = a.shape
    f = w.shape[-1] // 2
    tm = _row_tile(m, tm_cap)
    assert f % tn == 0
    a_spec = pl.BlockSpec((tm, k), lambda i, j: (i, 0))
    wg_spec = _w_spec(w, lead, k, tn, 0, 0)
    wu_spec = _w_spec(w, lead, k, tn, 0, f // tn)
    return pl.pallas_call(
        _mm_swiglu_kernel, grid=(m // tm, f // tn),
        in_specs=[a_spec, wg_spec, wu_spec],
        out_specs=pl.BlockSpec((tm, tn), lambda i, j: (i, j)),
        out_shape=jax.ShapeDtypeStruct((m, f), BF16),
        compiler_params=pltpu.CompilerParams(dimension_semantics=("parallel", "parallel"),
                                             vmem_limit_bytes=V7X_VMEM_LIMIT_BYTES),
        name="ffn_in_swiglu")(a, w, w)


def _out_proj(a, w, lead, res, scale, name):
    k = a.shape[1]
    n = w.shape[-1]
    nsplit = 1
    while (k // nsplit) > 5632 or k % nsplit:
        nsplit += 1
    kdim = k // nsplit
    assert kdim % LANE == 0
    out = res
    for s in range(nsplit):
        out = _matmul(a, w, lead, a_k0=s * kdim, w_k0=s * kdim, kdim=kdim, n0=0, ndim=n, tn=256,
                      res=out, scale=scale, name=name)
    return out


def _ffn(x, prm, i, which):
    xn = _rmsnorm(x, prm['ffn_norm'][i, which][None])
    h = _matmul_swiglu(xn, prm['w_ffn_in'], (i, which))
    return _out_proj(h, prm['w_ffn_out'], (i, which), x, 0.5, "ffn_out")


def _split3(v):
    hi = v.astype(BF16)
    r1 = v - hi.astype(F32)
    mid = r1.astype(BF16)
    lo = (r1 - mid.astype(F32)).astype(BF16)
    return hi, mid, lo


def _expand(parts, e):
    out = jnp.dot(parts[0], e, preferred_element_type=F32)
    for p in parts[1:]:
        out = out + jnp.dot(p, e, preferred_element_type=F32)
    return out


def _ssd_kernel(xbc_ref, prev_ref, conv8_ref, z_ref, dtr_ref, h0_ref, cw_ref, cb_ref, dtb_ref, alog_ref, dx_ref,
                nw_ref, ex_ref, y_ref, hl_ref, buf_ref, h_ref, xdt_ref, xw_ref, yoff_ref, yg_ref, *, q, d_inner,
                groups, nstate, hd, valid):
    c = pl.program_id(1)
    gw = d_inner // groups
    hpg = gw // hd
    conv_dim = buf_ref.shape[1]

    ppg = gw // LANE

    @pl.when(c == 0)
    def _():
        for g in range(groups):
            for p in range(ppg):
                r0 = (g * ppg + p) * LANE
                h_ref[g, :, p * LANE:(p + 1) * LANE] = h0_ref[r0:r0 + LANE, :].T

    buf_ref[0:SUBLANE, :] = jnp.where(c == 0, conv8_ref[...], prev_ref[...])
    buf_ref[SUBLANE:SUBLANE + q, :] = xbc_ref[...]
    nconv = cw_ref.shape[0]
    first = SUBLANE - (nconv - 1)
    cblk = 512
    for cbk in range(conv_dim // cblk):
        cols = slice(cbk * cblk, (cbk + 1) * cblk)
        s = buf_ref[pl.ds(first, q), cols] * cw_ref[0:1, cols]
        for k in range(1, nconv):
            s = s + buf_ref[pl.ds(first + k, q), cols] * cw_ref[k:k + 1, cols]
        buf_ref[SUBLANE:SUBLANE + q, cols] = jax.nn.silu(cb_ref[:, cols] + s)

    heads = dtr_ref.shape[1]
    row = lax.broadcasted_iota(jnp.int32, (q, heads), 0)
    dt = jax.nn.softplus(dtr_ref[...] + dtb_ref[...])
    if valid < q:
        dt = jnp.where(row < valid, dt, 0.0)
    cs = dt * (-jnp.exp(alog_ref[...]))
    sh = 1
    while sh < q:
        cs = cs + jnp.where(row >= sh, pltpu.roll(cs, sh, 0), 0.0)
        sh *= 2
    cs_t = cs.T
    cs_last = cs[q - 1:q, :]
    to_end = jnp.exp(cs_last - cs)
    dt_parts = _split3(dt)
    dtw_parts = _split3(dt * to_end)
    ecs_parts = _split3(jnp.exp(cs))
    dec_parts = _split3(jnp.broadcast_to(jnp.exp(cs_last), (SUBLANE, heads)))
    li = lax.broadcasted_iota(jnp.int32, (q, q), 0)
    si = lax.broadcasted_iota(jnp.int32, (q, q), 1)
    causal = li >= si
    lane = lax.broadcasted_iota(jnp.int32, (q, 2 * hd), 1)
    xs0 = SUBLANE
    for g in range(groups):
        gc = slice(g * gw, (g + 1) * gw)
        eg = ex_ref[:, gc]
        xs = buf_ref[xs0:xs0 + q, gc]
        xdt_ref[...] = (xs * _expand(dt_parts, eg)).astype(BF16)
        xw_ref[...] = (xs * _expand(dtw_parts, eg)).astype(BF16)
        bcol = d_inner + g * nstate
        ccol = d_inner + groups * nstate + g * nstate
        b_f = buf_ref[xs0:xs0 + q, bcol:bcol + nstate]
        bg = b_f.astype(BF16)
        bg_t = b_f.T.astype(BF16)
        cg = buf_ref[xs0:xs0 + q, ccol:ccol + nstate].astype(BF16)
        cbm = lax.dot_general(cg, bg, (((1,), (1,)), ((), ())), preferred_element_type=F32)
        h_t = h_ref[g]
        yoff_ref[...] = jnp.dot(cg, h_t.astype(BF16), preferred_element_type=F32) * _expand(ecs_parts, eg)
        st = jnp.dot(bg_t, xw_ref[...], preferred_element_type=F32)
        h_ref[g] = _expand(dec_parts, eg)[0:1, :] * h_t + st
        for p in range(hpg // 2):
            pc = slice(p * 2 * hd, (p + 1) * 2 * hd)
            ac = slice(g * gw + p * 2 * hd, g * gw + (p + 1) * 2 * hd)
            x_pair = xdt_ref[:, pc]
            ys = []
            for e in range(2):
                h = g * hpg + 2 * p + e
                seg = cs[:, h:h + 1] - cs_t[h:h + 1, :]
                m = (cbm * jnp.exp(jnp.where(causal, seg, NEG_INF))).astype(BF16)
                ys.append(jnp.dot(m, x_pair, preferred_element_type=F32))
            y = jnp.where(lane < hd, ys[0], ys[1]) + yoff_ref[:, pc]
            y = y + dx_ref[:, ac] * buf_ref[xs0:xs0 + q, ac]
            zz = z_ref[:, ac]
            yg_ref[:, pc] = y * (zz * jax.nn.sigmoid(zz))
        yg = yg_ref[...]
        yn = yg * lax.rsqrt(jnp.mean(yg * yg, axis=-1, keepdims=True) + NORM_EPS) * nw_ref[:, gc]
        y_ref[:, gc] = yn.astype(y_ref.dtype)

    @pl.when(c == pl.num_programs(1) - 1)
    def _():
        for g in range(groups):
            for p in range(ppg):
                r0 = (g * ppg + p) * LANE
                hl_ref[r0:r0 + LANE, :] = h_ref[g, :, p * LANE:(p + 1) * LANE].T


def _ssd_mix(xbc, z, dt_raw, row0, nb, nchunk, valid, conv_buf, h0, conv_w, conv_b, dt_bias, a_log, d_skip, norm_w,
             groups, nstate, hd):
    q = SSD_CHUNK
    conv_dim = xbc.shape[1]
    d_inner = z.shape[1]
    heads = dt_raw.shape[1]
    gw = d_inner // groups
    assert row0 % q == 0 and conv_dim == d_inner + 2 * groups * nstate and heads * hd == d_inner
    assert gw % LANE == 0 and 2 * hd == LANE and nstate == LANE and conv_dim % 512 == 0
    blk0 = row0 // q
    sub = q // SUBLANE
    nconv = conv_w.shape[0]
    conv8 = jnp.pad(conv_buf, ((0, 0), (SUBLANE - (nconv - 1), 0), (0, 0)))
    expand = (jnp.arange(heads)[:, None] == jnp.arange(d_inner)[None, :] // hd).astype(BF16)
    dx = jnp.repeat(d_skip, hd)[None]
    row_map = lambda b, c: (blk0 + b * nchunk + c, 0)
    full2 = lambda b, c: (0, 0)
    kern = functools.partial(_ssd_kernel, q=q, d_inner=d_inner, groups=groups, nstate=nstate, hd=hd, valid=valid)
    y, hl = pl.pallas_call(
        kern, grid=(nb, nchunk),
        in_specs=[pl.BlockSpec((q, conv_dim), row_map),
                  pl.BlockSpec((SUBLANE, conv_dim), lambda b, c: (jnp.maximum((blk0 + b * nchunk + c) * sub - 1, 0), 0)),
                  pl.BlockSpec((None, SUBLANE, conv_dim), lambda b, c: (b, 0, 0)),
                  pl.BlockSpec((q, d_inner), row_map),
                  pl.BlockSpec((q, heads), row_map),
                  pl.BlockSpec((None, heads * hd, nstate), lambda b, c: (b, 0, 0), pipeline_mode=pl.Buffered(1)),
                  pl.BlockSpec(conv_w.shape, full2),
                  pl.BlockSpec((1, conv_dim), full2),
                  pl.BlockSpec((1, heads), full2),
                  pl.BlockSpec((1, heads), full2),
                  pl.BlockSpec((1, d_inner), full2),
                  pl.BlockSpec((1, d_inner), full2),
                  pl.BlockSpec((heads, d_inner), full2, pipeline_mode=pl.Buffered(1))],
        out_specs=[pl.BlockSpec((q, d_inner), lambda b, c: (b * nchunk + c, 0)),
                   pl.BlockSpec((None, heads * hd, nstate), lambda b, c: (b, 0, 0))],
        out_shape=[jax.ShapeDtypeStruct((nb * nchunk * q, d_inner), BF16),
                   jax.ShapeDtypeStruct((nb, heads * hd, nstate), F32)],
        scratch_shapes=[pltpu.VMEM((q + SUBLANE, conv_dim), F32),
                        pltpu.VMEM((groups, nstate, gw), F32),
                        pltpu.VMEM((q, gw), BF16), pltpu.VMEM((q, gw), BF16),
                        pltpu.VMEM((q, gw), F32), pltpu.VMEM((q, gw), F32)],
        compiler_params=pltpu.CompilerParams(dimension_semantics=("parallel", "arbitrary"),
                                             vmem_limit_bytes=V7X_VMEM_LIMIT_BYTES),
        name="ssd_mix",
    )(xbc, xbc, conv8, z, dt_raw, h0.reshape(nb, heads * hd, nstate), conv_w, conv_b[None], dt_bias[None],
      a_log[None], dx, norm_w[None], expand)
    return y, hl.reshape(nb, heads, hd, nstate)


LRU_SCAN_LANES = 512


def _gelu_tanh(x):
    return 0.5 * x * (1.0 + jnp.tanh(math.sqrt(2.0 / math.pi) * (x + 0.044715 * (x * x * x))))


def _lru_kernel(gate_ref, xb_ref, prev_ref, conv8_ref, h0_ref, cw_ref, cb_ref, wr_ref, br_ref, wi_ref, bi_ref,
                lam_ref, y_ref, hl_ref, e_ref, a_ref, u_ref, h_ref, *, tm, width):
    t = pl.program_id(1)

    @pl.when(t == 0)
    def _():
        h_ref[...] = h0_ref[...]

    e_ref[0:SUBLANE, :] = jnp.where(t == 0, conv8_ref[...], prev_ref[...])
    e_ref[SUBLANE:SUBLANE + tm, :] = xb_ref[...]
    nconv = cw_ref.shape[0]
    first = SUBLANE - (nconv - 1)
    nblk = wr_ref.shape[0]
    bd = width // nblk
    for k in range(nblk):
        cols = slice(k * bd, (k + 1) * bd)
        s = e_ref[pl.ds(first, tm), cols] * cw_ref[0:1, cols]
        for c in range(1, nconv):
            s = s + e_ref[pl.ds(first + c, tm), cols] * cw_ref[c:c + 1, cols]
        xk = cb_ref[:, cols] + s
        xkb = xk.astype(BF16)
        r = jax.nn.sigmoid(jnp.dot(xkb, wr_ref[k], preferred_element_type=F32) + br_ref[:, cols])
        ig = jax.nn.sigmoid(jnp.dot(xkb, wi_ref[k], preferred_element_type=F32) + bi_ref[:, cols])
        log_a = (-LRU_C * r) * jax.nn.softplus(-lam_ref[:, cols])
        a_ref[:, cols] = jnp.exp(log_a)
        one_minus_a2 = -jnp.tanh(log_a) * (jnp.exp(2.0 * log_a) + 1.0)
        u_ref[:, cols] = jnp.sqrt(one_minus_a2) * ig * xk

    row = lax.broadcasted_iota(jnp.int32, (SUBLANE, LRU_SCAN_LANES), 0)
    for sl in range(width // LRU_SCAN_LANES):
        cols = slice(sl * LRU_SCAN_LANES, (sl + 1) * LRU_SCAN_LANES)

        def body(r, h):
            rows = pl.ds(pl.multiple_of(r * SUBLANE, SUBLANE), SUBLANE)
            a = a_ref[rows, cols]
            u = u_ref[rows, cols]
            for sh in (1, 2, 4):
                a_sh = jnp.where(row >= sh, pltpu.roll(a, sh, 0), 1.0)
                u_sh = jnp.where(row >= sh, pltpu.roll(u, sh, 0), 0.0)
                u = a * u_sh + u
                a = a * a_sh
            hs = a * h + u
            y_ref[rows, cols] = hs * _gelu_tanh(gate_ref[rows, cols])
            return hs[SUBLANE - 1:SUBLANE, :]

        h_ref[:, cols] = lax.fori_loop(0, tm // SUBLANE, body, h_ref[:, cols])
    hl_ref[...] = h_ref[...]


def _lru_mix(gx, row0, nb, seq, conv_buf, h0, conv_w, conv_b, w_r, b_r, w_i, b_i, lam):
    width = gx.shape[1] // 2
    tm = _row_tile(seq, 128) if seq % 16 == 0 else seq
    assert seq % tm == 0 and tm % SUBLANE == 0 and row0 % tm == 0 and width % LRU_SCAN_LANES == 0
    nt = seq // tm
    nconv = conv_w.shape[0]
    conv8 = jnp.pad(conv_buf, ((0, 0), (SUBLANE - (nconv - 1), 0), (0, 0)))
    blk0 = row0 // tm
    sub = tm // SUBLANE

    def row_map(b, t):
        return (blk0 + b * nt + t, 0)

    def xb_map(b, t):
        return (blk0 + b * nt + t, 1)

    def prev_map(b, t):
        return (jnp.maximum((blk0 + b * nt + t) * sub - 1, 0), 1)

    full2 = lambda b, t: (0, 0)
    full3 = lambda b, t: (0, 0, 0)
    y, hl = pl.pallas_call(
        functools.partial(_lru_kernel, tm=tm, width=width),
        grid=(nb, nt),
        in_specs=[pl.BlockSpec((tm, width), row_map),
                  pl.BlockSpec((tm, width), xb_map),
                  pl.BlockSpec((SUBLANE, width), prev_map),
                  pl.BlockSpec((None, SUBLANE, width), lambda b, t: (b, 0, 0)),
                  pl.BlockSpec((None, 1, width), lambda b, t: (b, 0, 0)),
                  pl.BlockSpec(conv_w.shape, full2),
                  pl.BlockSpec((1, width), full2),
                  pl.BlockSpec(w_r.shape, full3),
                  pl.BlockSpec((1, width), full2),
                  pl.BlockSpec(w_i.shape, full3),
                  pl.BlockSpec((1, width), full2),
                  pl.BlockSpec((1, width), full2)],
        out_specs=[pl.BlockSpec((tm, width), lambda b, t: (b * nt + t, 0)),
                   pl.BlockSpec((None, 1, width), lambda b, t: (b, 0, 0))],
        out_shape=[jax.ShapeDtypeStruct((nb * seq, width), F32),
                   jax.ShapeDtypeStruct((nb, 1, width), F32)],
        scratch_shapes=[pltpu.VMEM((tm + SUBLANE, width), F32),
                        pltpu.VMEM((tm, width), F32),
                        pltpu.VMEM((tm, width), F32),
                        pltpu.VMEM((1, width), F32)],
        compiler_params=pltpu.CompilerParams(dimension_semantics=("parallel", "arbitrary"),
                                             vmem_limit_bytes=V7X_VMEM_LIMIT_BYTES),
        name="lru_mix",
    )(gx, gx, gx, conv8, h0[:, None, :], conv_w, conv_b[None], w_r.astype(BF16), b_r.reshape(1, width),
      w_i.astype(BF16), b_i.reshape(1, width), lam[None])
    return y, hl[:, 0]


KEY_CHUNK = 128
TINY = 1e-30
ATT_CHUNKS = 2


def _bucket_of(dist):
    n = jnp.maximum(dist, 0)
    exact = REL_BUCKETS // 2
    log_ratio = jnp.log(jnp.maximum(n, 1).astype(F32) / exact) / math.log(REL_MAX_DIST / exact)
    large = jnp.minimum(exact + (log_ratio * (REL_BUCKETS - exact)).astype(jnp.int32), REL_BUCKETS - 1)
    return jnp.where(n < exact, n, large)


def _bias_from_buckets(bucket, tab_ref, h):
    out = jnp.full(bucket.shape, tab_ref[0, h], F32)
    for k in range(1, REL_BUCKETS):
        out = jnp.where(bucket == k, tab_ref[k, h], out)
    return out


def _head_rmsnorm(x, w_row):
    return x * lax.rsqrt(jnp.mean(x * x, axis=-1, keepdims=True) + NORM_EPS) * w_row


def _bias_tiles_kernel(tab_ref, o_ref, *, tq):
    i = lax.broadcasted_iota(jnp.int32, (tq, KEY_CHUNK), 0)
    j = lax.broadcasted_iota(jnp.int32, (tq, KEY_CHUNK), 1)
    buckets = [_bucket_of(i - j + KEY_CHUNK * k) for k in range(2)]
    nheads = o_ref.shape[1] // tq

    def body(h, carry):
        rows = pl.ds(pl.multiple_of(h * tq, SUBLANE), tq)
        for k in range(2):
            o_ref[k, rows, :] = _bias_from_buckets(buckets[k], tab_ref, h)
        o_ref[2, rows, :] = jnp.full((tq, KEY_CHUNK), tab_ref[REL_BUCKETS - 1, h], F32)
        return carry

    lax.fori_loop(0, nheads, body, 0)


def _bias_tiles(table, tq):
    nheads = table.shape[1]
    return pl.pallas_call(
        functools.partial(_bias_tiles_kernel, tq=tq),
        in_specs=[pl.BlockSpec(memory_space=pltpu.SMEM)],
        out_specs=pl.BlockSpec(memory_space=pltpu.VMEM),
        out_shape=jax.ShapeDtypeStruct((3, nheads * tq, KEY_CHUNK), F32),
        name="nsa_bias_tiles",
    )(table)


def _page_spec(src, lead, width, col_blk):
    if src.ndim == 2:
        return pl.BlockSpec((KEY_CHUNK, width), lambda b, p, t: (t[b, p], col_blk))
    nlead = len(lead)
    assert src.ndim == nlead + 3 and src.shape[-2] == KEY_CHUNK
    return pl.BlockSpec((None,) * (nlead + 1) + (KEY_CHUNK, width),
                        lambda b, p, t: tuple(lead) + (t[b, p], 0, col_blk))


def _compress_ab_kernel(tbl_ref, x_ref, pe_ref, w1_ref, o_ref, slab_ref, *, groups, d):
    half = CMP_LEN // 2
    nchunk = KEY_CHUNK // CMP_STRIDE
    for sg in range(2 * groups):
        slab_ref[sg] = x_ref[:, sg * d:(sg + 1) * d]
    for s in range(2):
        acc = [jnp.zeros((groups * nchunk, d), F32) for _ in range(2)]
        for l in range(half):
            x = jnp.concatenate(
                [slab_ref[s * groups + g, pl.ds(l, nchunk, stride=CMP_STRIDE), :] for g in range(groups)], axis=0)
            for part in range(2):
                ll = part * half + l
                xa = (x + pe_ref[s, ll:ll + 1, :]).astype(BF16)
                acc[part] = acc[part] + jnp.dot(xa, w1_ref[s, ll], preferred_element_type=F32)
        for part in range(2):
            for g in range(groups):
                o_ref[s, part, g] = acc[part][g * nchunk:(g + 1) * nchunk]


def _compress_ab(src, lead, tbl, pe, w1b, groups, d):
    nb, npages = tbl.shape
    nchunk = KEY_CHUNK // CMP_STRIDE
    assert CMP_LEN == 2 * CMP_STRIDE
    grid_spec = pltpu.PrefetchScalarGridSpec(
        num_scalar_prefetch=1, grid=(nb, npages),
        in_specs=[_page_spec(src, lead, 2 * groups * d, 0),
                  pl.BlockSpec(pe.shape, lambda b, p, t: (0, 0, 0)),
                  pl.BlockSpec(w1b.shape, lambda b, p, t: (0, 0, 0, 0))],
        out_specs=pl.BlockSpec((None, 2, 2, groups, nchunk, d), lambda b, p, t: (b, 0, 0, 0, p, 0)),
        scratch_shapes=[pltpu.VMEM((2 * groups, KEY_CHUNK, d), F32)])
    return pl.pallas_call(
        functools.partial(_compress_ab_kernel, groups=groups, d=d),
        grid_spec=grid_spec,
        out_shape=jax.ShapeDtypeStruct((nb, 2, 2, groups, npages * nchunk, d), F32),
        compiler_params=pltpu.CompilerParams(dimension_semantics=("parallel", "arbitrary")),
        name="nsa_compress_ab",
    )(tbl, src, pe, w1b)


def _compress_finish_kernel(ab_ref, w2_ref, kw_ref, o_ref, *, groups, d):
    n = ab_ref.shape[3]
    for s in range(2):
        for g in range(groups):
            first = ab_ref[s, 0, g]
            second = pltpu.roll(ab_ref[s, 1, g], n - 1, 0)
            hid = jax.nn.silu(first + second).astype(BF16)
            out = jnp.dot(hid, w2_ref[s], preferred_element_type=F32)
            if s == 0:
                out = _head_rmsnorm(out, kw_ref[...])
            o_ref[s, :, g * d:(g + 1) * d] = out.astype(o_ref.dtype)


def _compress_finish(ab, w2b, k_norm_row, groups, d):
    nb, _, _, _, n, _ = ab.shape
    return pl.pallas_call(
        functools.partial(_compress_finish_kernel, groups=groups, d=d),
        grid=(nb,),
        in_specs=[pl.BlockSpec((None, 2, 2, groups, n, d), lambda b: (b, 0, 0, 0, 0, 0)),
                  pl.BlockSpec(w2b.shape, lambda b: (0, 0, 0)),
                  pl.BlockSpec((1, d), lambda b: (0, 0))],
        out_specs=pl.BlockSpec((None, 2, n, groups * d), lambda b: (b, 0, 0, 0)),
        out_shape=jax.ShapeDtypeStruct((nb, 2, n, groups * d), BF16),
        compiler_params=pltpu.CompilerParams(dimension_semantics=("parallel",)),
        name="nsa_compress_finish",
    )(ab, w2b, k_norm_row)


def _select_kernel(tab_ref, q_ref, kcv_ref, qw_ref, ocmp_ref, msel_ref, *, tq, groups, hpg, d, q_pos0, n_cmp, n_sel,
                   nselp, scale):
    qi = pl.program_id(1)
    pos0 = q_pos0 + qi * tq
    nck = kcv_ref.shape[1]
    t_c = lax.broadcasted_iota(jnp.int32, (tq, nck), 0)
    n_c = lax.broadcasted_iota(jnp.int32, (tq, nck), 1)
    dist = pos0 + t_c - (n_c * CMP_STRIDE + CMP_LEN - 1)
    cmask = (dist >= 0) & (n_c < n_cmp)
    cmaskf = cmask.astype(F32)
    bucket = _bucket_of(dist)
    n_r = lax.broadcasted_iota(jnp.int32, (nck, nselp), 0)
    j_r = lax.broadcasted_iota(jnp.int32, (nck, nselp), 1)
    cover = ((n_r * CMP_STRIDE < j_r * SEL_LEN + SEL_LEN) & (n_r * CMP_STRIDE + CMP_LEN > j_r * SEL_LEN)
             & (n_r < n_cmp) & (j_r < n_sel)).astype(BF16)
    j_s = lax.broadcasted_iota(jnp.int32, (tq, nselp), 1)
    pos = pos0 + lax.broadcasted_iota(jnp.int32, (tq, nselp), 0)
    cur = pos // SEL_LEN
    visible = (j_s * SEL_LEN <= pos) & (j_s < n_sel)
    forced = ((j_s == 0) | (j_s == cur) | (j_s == cur - 1)).astype(F32)
    for g in range(groups):
        kc = kcv_ref[0, :, g * d:(g + 1) * d]
        vc = kcv_ref[1, :, g * d:(g + 1) * d]
        p_grp = jnp.zeros((tq, nck), F32)
        for e in range(hpg):
            h = g * hpg + e
            cols = slice(h * d, (h + 1) * d)
            qn = _head_rmsnorm(q_ref[:, cols], qw_ref[...]).astype(BF16)
            logits = lax.dot_general(qn, kc, (((1,), (1,)), ((), ())), preferred_element_type=F32) * scale
            logits = jnp.where(cmask, logits + _bias_from_buckets(bucket, tab_ref, h), NEG_INF)
            m = jnp.max(logits, axis=-1, keepdims=True)
            p = jnp.exp(logits - m)
            p = p / jnp.sum(p, axis=-1, keepdims=True) * cmaskf
            ocmp_ref[:, cols] = jnp.dot(p.astype(BF16), vc, preferred_element_type=F32)
            p_grp = p_grp + p
        p_sel = jnp.dot(p_grp.astype(BF16), cover, preferred_element_type=F32)
        score = jnp.where(visible, p_sel + FORCE_BONUS * forced, NEG_INF)
        cnt = jnp.zeros((tq, nselp), jnp.int32)
        for jp in range(n_sel):
            col = score[:, jp:jp + 1]
            beats = (col > score) | ((col == score) & (j_s > jp))
            cnt = cnt + beats.astype(jnp.int32)
        msel_ref[:, g * nselp:(g + 1) * nselp] = ((cnt < SEL_TOPK) & visible).astype(F32)


def _nsa_select(table, qp, row0, kcv, q_norm_row, *, nb, seq, tq, q_pos0, n_cmp, n_sel, groups, hpg, d):
    nq = seq // tq
    n_q = groups * hpg * d
    nselp = -(-n_sel // LANE) * LANE
    nck = kcv.shape[2]
    assert row0 % tq == 0
    blk0 = row0 // tq
    kern = functools.partial(_select_kernel, tq=tq, groups=groups, hpg=hpg, d=d, q_pos0=q_pos0, n_cmp=n_cmp,
                             n_sel=n_sel, nselp=nselp, scale=d ** -0.5)
    return pl.pallas_call(
        kern, grid=(nb, nq),
        in_specs=[pl.BlockSpec(memory_space=pltpu.SMEM),
                  pl.BlockSpec((tq, n_q), lambda b, q: (blk0 + b * nq + q, 0)),
                  pl.BlockSpec((None, 2, nck, groups * d), lambda b, q: (b, 0, 0, 0)),
                  pl.BlockSpec((1, d), lambda b, q: (0, 0))],
        out_specs=[pl.BlockSpec((tq, n_q), lambda b, q: (b * nq + q, 0)),
                   pl.BlockSpec((tq, groups * nselp), lambda b, q: (b * nq + q, 0))],
        out_shape=[jax.ShapeDtypeStruct((nb * seq, n_q), F32),
                   jax.ShapeDtypeStruct((nb * seq, groups * nselp), F32)],
        compiler_params=pltpu.CompilerParams(dimension_semantics=("parallel", "parallel"),
                                             vmem_limit_bytes=V7X_VMEM_LIMIT_BYTES),
        name="nsa_cmp_select",
    )(table, qp, kcv, q_norm_row)


def _kv_pack_kernel(tbl_ref, x_ref, kw_ref, k_ref, v_ref, *, groups, d):
    for g in range(groups):
        k = _head_rmsnorm(x_ref[:, g * d:(g + 1) * d], kw_ref[...])
        k_ref[:, g * d:(g + 1) * d] = k.astype(k_ref.dtype)
    v_ref[...] = x_ref[:, groups * d:].astype(v_ref.dtype)


def _kv_pack(src, lead, tbl, col_blk, k_norm_row, groups, d):
    nb, npages = tbl.shape
    gd = groups * d
    grid_spec = pltpu.PrefetchScalarGridSpec(
        num_scalar_prefetch=1, grid=(nb, npages),
        in_specs=[_page_spec(src, lead, 2 * gd, col_blk),
                  pl.BlockSpec((1, d), lambda b, p, t: (0, 0))],
        out_specs=[pl.BlockSpec((None, KEY_CHUNK, gd), lambda b, p, t: (b, p, 0)),
                   pl.BlockSpec((None, KEY_CHUNK, gd), lambda b, p, t: (b, p, 0))])
    return pl.pallas_call(
        functools.partial(_kv_pack_kernel, groups=groups, d=d),
        grid_spec=grid_spec,
        out_shape=[jax.ShapeDtypeStruct((nb, npages * KEY_CHUNK, gd), BF16)] * 2,
        compiler_params=pltpu.CompilerParams(dimension_semantics=("parallel", "parallel")),
        name="nsa_kv_pack",
    )(tbl, src, k_norm_row)


def _attend_kernel(tiles_ref, q_ref, ocmp_ref, gate_ref, msel_ref, ks_ref, vs_ref, kw_ref, vw_ref, qw_ref, o_ref,
                   qn_ref, ms_ref, ls_ref, as_ref, mw_ref, lw_ref, aw_ref, *, tq, groups, hpg, d, q_pos0,
                   win_chunk0, n_win_steps, nselp, scale):
    qi = pl.program_id(1)
    c = pl.program_id(2)
    pos0 = q_pos0 + qi * tq
    p_hi = (pos0 + tq - 1) // KEY_CHUNK // ATT_CHUNKS
    rows_g = hpg * tq
    nheads = groups * hpg

    @pl.when(c == 0)
    def _():
        for h in range(nheads):
            qn_ref[h * tq:(h + 1) * tq, :] = _head_rmsnorm(q_ref[:, h * d:(h + 1) * d], qw_ref[...]).astype(BF16)
        for m_ref, l_ref, a_ref in ((ms_ref, ls_ref, as_ref), (mw_ref, lw_ref, aw_ref)):
            m_ref[...] = jnp.full(m_ref.shape, NEG_INF, F32)
            l_ref[...] = jnp.zeros(l_ref.shape, F32)
            a_ref[...] = jnp.zeros(a_ref.shape, F32)

    assert tq & (tq - 1) == 0 and d == LANE
    t_idx = lax.broadcasted_iota(jnp.int32, (rows_g, KEY_CHUNK), 0) & (tq - 1)
    j_idx = lax.broadcasted_iota(jnp.int32, (rows_g, KEY_CHUNK), 1)
    ones_v = jnp.ones((KEY_CHUNK, d), BF16)

    def step(g, k_ref, v_ref, m_ref, l_ref, a_ref, chunk0, masks):
        rows = slice(g * rows_g, (g + 1) * rows_g)
        q = qn_ref[rows, :]
        logits = []
        for u, mask in enumerate(masks):
            delta = pos0 - (chunk0 + u) * KEY_CHUNK
            tile = jnp.clip(delta // KEY_CHUNK, 0, 2)
            k = k_ref[u * KEY_CHUNK:(u + 1) * KEY_CHUNK, g * d:(g + 1) * d]
            s = lax.dot_general(q, k, (((1,), (1,)), ((), ())), preferred_element_type=F32) * scale
            s = s + tiles_ref[tile, pl.ds(g * rows_g, rows_g), :]
            logits.append(jnp.where(mask, s, NEG_INF))
        m_prev = m_ref[rows, :]
        m_new = m_prev
        for s in logits:
            m_new = jnp.maximum(m_new, jnp.max(s, axis=-1, keepdims=True))
        alpha = jnp.exp(m_prev - m_new)
        pv = None
        for u, (s, mask) in enumerate(zip(logits, masks)):
            p = jnp.where(mask, jnp.exp(s - m_new), 0.0).astype(BF16)
            v_ext = jnp.concatenate([v_ref[u * KEY_CHUNK:(u + 1) * KEY_CHUNK, g * d:(g + 1) * d], ones_v], axis=1)
            part = jnp.dot(p, v_ext, preferred_element_type=F32)
            pv = part if pv is None else pv + part
        l_ref[rows, :] = alpha * l_ref[rows, :] + pv[:, d:]
        a_ref[rows, :] = alpha * a_ref[rows, :] + pv[:, :d]
        m_ref[rows, :] = m_new

    @pl.when(c <= p_hi)
    def _():
        chunk0 = c * ATT_CHUNKS
        blocks_per_chunk = KEY_CHUNK // SEL_LEN
        jr = lax.broadcasted_iota(jnp.int32, (nselp, KEY_CHUNK), 0)
        lr = lax.broadcasted_iota(jnp.int32, (nselp, KEY_CHUNK), 1)
        causal, expand = [], []
        for u in range(ATT_CHUNKS):
            causal.append(pos0 - (chunk0 + u) * KEY_CHUNK + t_idx - j_idx >= 0)
            expand.append((jr == (chunk0 + u) * blocks_per_chunk + lr // SEL_LEN).astype(BF16))
        for g in range(groups):
            ms = msel_ref[:, g * nselp:(g + 1) * nselp].astype(BF16)
            masks = []
            for u in range(ATT_CHUNKS):
                sel = jnp.dot(ms, expand[u], preferred_element_type=F32)
                masks.append(causal[u] & (jnp.concatenate([sel] * hpg, axis=0) > 0.5))
            step(g, ks_ref, vs_ref, ms_ref, ls_ref, as_ref, chunk0, masks)

    wpair = p_hi - c

    @pl.when((c < n_win_steps) & (wpair * ATT_CHUNKS >= win_chunk0))
    def _():
        chunk0 = wpair * ATT_CHUNKS
        masks = []
        for u in range(ATT_CHUNKS):
            dist = pos0 - (chunk0 + u) * KEY_CHUNK + t_idx - j_idx
            masks.append((dist >= 0) & (dist < WINDOW))
        for g in range(groups):
            step(g, kw_ref, vw_ref, mw_ref, lw_ref, aw_ref, chunk0, masks)

    @pl.when(c == pl.num_programs(2) - 1)
    def _():
        gates = jax.nn.sigmoid(gate_ref[...])
        for h in range(nheads):
            r = slice(h * tq, (h + 1) * tq)
            cols = slice(h * d, (h + 1) * d)
            o_sel = as_ref[r, :] / jnp.maximum(ls_ref[r, :], TINY)
            o_win = aw_ref[r, :] / jnp.maximum(lw_ref[r, :], TINY)
            o_ref[:, cols] = (gates[:, 3 * h:3 * h + 1] * ocmp_ref[:, cols] + gates[:, 3 * h + 1:3 * h + 2] * o_sel
                              + gates[:, 3 * h + 2:3 * h + 3] * o_win)


def _nsa_attend(tiles, qp, gp, row0, ocmp, msel, ksel, vsel, kwin, vwin, q_norm_row, *, nb, seq, tq, q_pos0,
                win_chunk0, groups, hpg, d):
    nq = seq // tq
    n_q = groups * hpg * d
    gd = groups * d
    nheads = groups * hpg
    nselp = msel.shape[1] // groups
    pair = ATT_CHUNKS * KEY_CHUNK
    assert ksel.shape[1] % pair == 0 and kwin.shape[1] % pair == 0 and win_chunk0 % ATT_CHUNKS == 0
    npair = ksel.shape[1] // pair
    nwpair = kwin.shape[1] // pair
    wpair0 = win_chunk0 // ATT_CHUNKS
    n_win_steps = (WINDOW // KEY_CHUNK + ATT_CHUNKS - 1) // ATT_CHUNKS + 1
    ncg = max(npair, n_win_steps)
    assert row0 % tq == 0 and q_pos0 % KEY_CHUNK == 0 and (tq == KEY_CHUNK or nq == 1) and tq <= KEY_CHUNK
    blk0 = row0 // tq

    def p_hi(q):
        return (q_pos0 + q * tq + tq - 1) // KEY_CHUNK // ATT_CHUNKS

    def sel_map(b, q, c):
        return (b, jnp.minimum(c, p_hi(q)), 0)

    def win_map(b, q, c):
        return (b, jnp.clip(p_hi(q) - jnp.minimum(c, n_win_steps - 1) - wpair0, 0, nwpair - 1), 0)

    grp_map = lambda b, q, c: (b * nq + q, 0)
    all_map = lambda b, q, c: (blk0 + b * nq + q, 0)
    kern = functools.partial(_attend_kernel, tq=tq, groups=groups, hpg=hpg, d=d, q_pos0=q_pos0,
                             win_chunk0=win_chunk0, n_win_steps=n_win_steps, nselp=nselp, scale=d ** -0.5)
    return pl.pallas_call(
        kern, grid=(nb, nq, ncg),
        in_specs=[pl.BlockSpec(tiles.shape, lambda b, q, c: (0, 0, 0)),
                  pl.BlockSpec((tq, n_q), all_map),
                  pl.BlockSpec((tq, n_q), grp_map),
                  pl.BlockSpec((tq, gp.shape[1]), all_map),
                  pl.BlockSpec((tq, groups * nselp), grp_map),
                  pl.BlockSpec((None, pair, gd), sel_map),
                  pl.BlockSpec((None, pair, gd), sel_map),
                  pl.BlockSpec((None, pair, gd), win_map),
                  pl.BlockSpec((None, pair, gd), win_map),
                  pl.BlockSpec((1, d), lambda b, q, c: (0, 0))],
        out_specs=pl.BlockSpec((tq, n_q), grp_map),
        out_shape=jax.ShapeDtypeStruct((nb * seq, n_q), F32),
        scratch_shapes=[pltpu.VMEM((nheads * tq, d), BF16),
                        pltpu.VMEM((nheads * tq, d), F32), pltpu.VMEM((nheads * tq, d), F32),
                        pltpu.VMEM((nheads * tq, d), F32),
                        pltpu.VMEM((nheads * tq, d), F32), pltpu.VMEM((nheads * tq, d), F32),
                        pltpu.VMEM((nheads * tq, d), F32)],
        compiler_params=pltpu.CompilerParams(dimension_semantics=("parallel", "parallel", "arbitrary"),
                                             vmem_limit_bytes=V7X_VMEM_LIMIT_BYTES),
        name="nsa_attend",
    )(tiles, qp, ocmp, gp, msel, ksel, vsel, kwin, vwin, q_norm_row)


def _nsa_group(qp, kvp, gp, row0, nb, seq, q_pos0, past, win_past, w, groups, hpg, d):
    q_norm_w, k_norm_w, cmp_pe, cmp_w1, cmp_w2, table = w
    gd = groups * d
    tq = KEY_CHUNK if seq % KEY_CHUNK == 0 else seq
    assert tq % SUBLANE == 0
    w1b, w2b = cmp_w1.astype(BF16), cmp_w2.astype(BF16)
    if past is None:
        assert seq % KEY_CHUNK == 0 and row0 % KEY_CHUNK == 0
        npg = seq // KEY_CHUNK
        tbl = row0 // KEY_CHUNK + jnp.arange(nb * npg, dtype=jnp.int32).reshape(nb, npg)
        src = kvp
        ab = _compress_ab(src, (), tbl, cmp_pe, w1b, groups, d)
        ksel, vsel = _kv_pack(src, (), tbl, 1, k_norm_w[1:2], groups, d)
        kwin, vwin = _kv_pack(src, (), tbl, 2, k_norm_w[2:3], groups, d)
        tk = seq
        win_chunk0 = 0
    else:
        cache, lead, tbl = past
        npg = tbl.shape[1]
        past_len = npg * KEY_CHUNK
        assert q_pos0 == past_len and seq < CMP_STRIDE and seq <= KEY_CHUNK
        ab = _compress_ab(cache, lead, tbl, cmp_pe, w1b, groups, d)
        ksel_p, vsel_p = _kv_pack(cache, lead, tbl, 1, k_norm_w[1:2], groups, d)
        wb = win_past.shape[1]
        assert wb % KEY_CHUNK == 0 and wb <= WINDOW and (past_len - wb) % KEY_CHUNK == 0
        nwp = wb // KEY_CHUNK
        wtbl = jnp.arange(nb * nwp, dtype=jnp.int32).reshape(nb, nwp)
        kwin_p, vwin_p = _kv_pack(win_past.reshape(nb * nwp, KEY_CHUNK, 2 * gd), (), wtbl, 0, k_norm_w[2:3], groups, d)
        new = kvp[row0:row0 + nb * seq].reshape(nb, seq, kvp.shape[1])
        new = jnp.pad(new, ((0, 0), (0, KEY_CHUNK - seq), (0, 0)))
        ntbl = jnp.arange(nb, dtype=jnp.int32).reshape(nb, 1)
        ksel_n, vsel_n = _kv_pack(new, (), ntbl, 1, k_norm_w[1:2], groups, d)
        kwin_n, vwin_n = _kv_pack(new, (), ntbl, 2, k_norm_w[2:3], groups, d)

        def join(past_part, new_part):
            rows = past_part.shape[1] + new_part.shape[1]
            fill = jnp.zeros((nb, -rows % (ATT_CHUNKS * KEY_CHUNK), gd), past_part.dtype)
            return jnp.concatenate([past_part, new_part, fill], axis=1)

        ksel, vsel = join(ksel_p, ksel_n), join(vsel_p, vsel_n)
        kwin, vwin = join(kwin_p, kwin_n), join(vwin_p, vwin_n)
        tk = past_len + seq
        win_chunk0 = (past_len - wb) // KEY_CHUNK
    n_cmp = (tk - CMP_LEN) // CMP_STRIDE + 1
    n_sel = -(-tk // SEL_LEN)
    assert n_cmp <= ab.shape[4] - 1
    kcv = _compress_finish(ab, w2b, k_norm_w[0:1], groups, d)
    ocmp, msel = _nsa_select(table, qp, row0, kcv, q_norm_w[None], nb=nb, seq=seq, tq=tq, q_pos0=q_pos0,
                             n_cmp=n_cmp, n_sel=n_sel, groups=groups, hpg=hpg, d=d)
    tiles = _bias_tiles(table, tq)
    return _nsa_attend(tiles, qp, gp, row0, ocmp, msel, ksel, vsel, kwin, vwin, q_norm_w[None], nb=nb, seq=seq,
                       tq=tq, q_pos0=q_pos0, win_chunk0=win_chunk0, groups=groups, hpg=hpg, d=d)


def kernel(x_prompt, x_sample, state_ssd, state_ssd_conv, state_lru, state_lru_conv, cache_nsa_kv, cache_nsa_win, page_table, ffn_norm, w_ffn_in, w_ffn_out, mix_norm, ssd_w_in, ssd_conv_w, ssd_conv_b, ssd_dt_bias, ssd_a_log, ssd_d, ssd_norm, ssd_w_out, lru_w_in, lru_conv_w, lru_conv_b, lru_w_r, lru_b_r, lru_w_i, lru_b_i, lru_lambda, lru_w_out, nsa_w_in, nsa_q_norm, nsa_k_norm, nsa_cmp_pe, nsa_cmp_w1, nsa_cmp_w2, rel_bias_table, nsa_w_out):
    prm = dict(ffn_norm=ffn_norm, w_ffn_in=w_ffn_in, w_ffn_out=w_ffn_out)
    bp, sp, dm = x_prompt.shape
    bs, ss, _ = x_sample.shape
    mp, ms = bp * sp, bs * ss
    depth = mix_norm.shape[0]
    G = NSA_GROUPS
    E = NSA_HEADS // G
    d_head = dm // NSA_HEADS
    wb = cache_nsa_win.shape[2]
    n_pages = page_table.shape[1]
    page = cache_nsa_kv.shape[2]
    past_len = n_pages * page
    assert page == KEY_CHUNK

    def merge(tp, ts):
        return jnp.concatenate([tp.reshape(mp, -1), ts.reshape(ms, -1)], axis=0)

    def tail(buf, rows, prompt, c0, c1):
        n = buf.shape[1]
        nb_, seq, r0 = (bp, sp, 0) if prompt else (bs, ss, mp)
        k = min(n, seq)
        new = rows[r0:r0 + nb_ * seq].reshape(nb_, seq, -1)[:, seq - k:, c0:c1]
        return jnp.concatenate([buf, new], axis=1)[:, -n:]

    x = merge(x_prompt, x_sample)
    outs = {k: [] for k in ('ssd_h_p', 'ssd_h_s', 'ssd_buf_p', 'ssd_buf_s', 'lru_h_p', 'lru_h_s',
                            'lru_buf_p', 'lru_buf_s', 'rows_p', 'rows_s', 'win_p', 'win_s')}
    for i in range(depth):
        kind, j = i % N_MIXERS, i // N_MIXERS
        x = _ffn(x, prm, i, 0)
        hn = _rmsnorm(x, mix_norm[i][None])
        if kind == 0:
            d_inner = ssd_w_out.shape[1]
            conv_dim = ssd_conv_w.shape[2]
            heads = ssd_dt_bias.shape[1]
            z = _matmul(hn, ssd_w_in, (j,), kdim=dm, n0=0, ndim=d_inner, tn=512, name="ssd_in_z")
            xbc = _matmul(hn, ssd_w_in, (j,), kdim=dm, n0=d_inner, ndim=conv_dim, tn=512, name="ssd_in_xbc")
            dt_raw = _matmul(hn, ssd_w_in, (j,), kdim=dm, n0=d_inner + conv_dim, ndim=heads, tn=heads,
                             name="ssd_in_dt")
            w = (ssd_conv_w[j], ssd_conv_b[j], ssd_dt_bias[j], ssd_a_log[j], ssd_d[j], ssd_norm[j])
            q = SSD_CHUNK
            assert sp % q == 0 and ss <= q
            nbuf = ssd_conv_w.shape[1] - 1
            hd, nstate = d_inner // heads, state_ssd.shape[-1]
            shape = (SSD_GROUPS, nstate, hd)
            zero_buf = jnp.zeros((bp, nbuf, conv_dim), F32)
            zero_h = jnp.zeros((bp,) + state_ssd.shape[2:], F32)
            y_p, st_p = _ssd_mix(xbc, z, dt_raw, 0, bp, sp // q, q, zero_buf, zero_h, *w, *shape)

            def pad_s(t):
                return jnp.pad(t[mp:].reshape(bs, ss, -1), ((0, 0), (0, q - ss), (0, 0))).reshape(bs * q, -1)

            y_s, st_s = _ssd_mix(pad_s(xbc), pad_s(z), pad_s(dt_raw), 0, bs, 1, ss, state_ssd_conv[j], state_ssd[j],
                                 *w, *shape)
            y_s = y_s.reshape(bs, q, d_inner)[:, :ss].reshape(ms, d_inner)
            outs['ssd_h_p'].append(st_p); outs['ssd_h_s'].append(st_s)
            outs['ssd_buf_p'].append(tail(zero_buf, xbc, True, 0, conv_dim))
            outs['ssd_buf_s'].append(tail(state_ssd_conv[j], xbc, False, 0, conv_dim))
            x = _out_proj(jnp.concatenate([y_p, y_s], axis=0), ssd_w_out, (j,), x, 1.0, "ssd_out")
        elif kind == 1:
            width = lru_w_out.shape[1]
            nbuf = lru_conv_w.shape[1] - 1
            gx = _matmul(hn, lru_w_in, (j,), kdim=dm, n0=0, ndim=2 * width, tn=512, name="lru_in")
            w = (lru_conv_w[j], lru_conv_b[j], lru_w_r[j], lru_b_r[j], lru_w_i[j], lru_b_i[j], lru_lambda[j])
            zero_buf = jnp.zeros((bp, nbuf, width), F32)
            y_p, st_p = _lru_mix(gx, 0, bp, sp, zero_buf, jnp.zeros((bp, width), F32), *w)
            y_s, st_s = _lru_mix(gx, mp, bs, ss, state_lru_conv[j], state_lru[j], *w)
            outs['lru_h_p'].append(st_p); outs['lru_h_s'].append(st_s)
            outs['lru_buf_p'].append(tail(zero_buf, gx, True, width, 2 * width))
            outs['lru_buf_s'].append(tail(state_lru_conv[j], gx, False, width, 2 * width))
            x = _out_proj(jnp.concatenate([y_p, y_s], axis=0).astype(BF16), lru_w_out, (j,), x, 1.0, "lru_out")
        else:
            n_q, n_rows, n_kv = NSA_HEADS * d_head, 4 * G * d_head, 6 * G * d_head
            qp = _matmul(hn, nsa_w_in, (j,), kdim=dm, n0=0, ndim=n_q, tn=512, name="nsa_in_q")
            kvp = _matmul(hn, nsa_w_in, (j,), kdim=dm, n0=n_q, ndim=n_kv, tn=512, name="nsa_in_kv")
            n_gate = nsa_w_in.shape[2] - n_q - n_kv
            w_gate = jnp.pad(nsa_w_in[j][None, :, n_q + n_kv:], ((0, 0), (0, 0), (0, LANE - n_gate)))
            gp = _matmul(hn, w_gate, (0,), kdim=dm, n0=0, ndim=LANE, tn=LANE, name="nsa_in_gate")[:, :n_gate]
            w = (nsa_q_norm[j], nsa_k_norm[j], nsa_cmp_pe[j], nsa_cmp_w1[j], nsa_cmp_w2[j], rel_bias_table)
            o_p = _nsa_group(qp, kvp, gp, 0, bp, sp, 0, None, None, w, G, E, d_head)
            cache = cache_nsa_kv.reshape(cache_nsa_kv.shape[:3] + (n_rows,))
            win_past = cache_nsa_win[j].reshape(bs, wb, 2 * G * d_head)
            o_s = _nsa_group(qp, kvp, gp, mp, bs, ss, past_len, (cache, (j,), page_table), win_past, w, G, E, d_head)
            outs['rows_p'].append(kvp[:mp, :n_rows].reshape(bp, sp, 4, G, d_head))
            outs['rows_s'].append(kvp[mp:, :n_rows].reshape(bs, ss, 4, G, d_head))
            win_p = tail(jnp.zeros((bp, wb, 2 * G * d_head), F32), kvp, True, n_rows, n_kv)
            win_s = tail(win_past, kvp, False, n_rows, n_kv)
            outs['win_p'].append(win_p.reshape(bp, wb, 2, G, d_head))
            outs['win_s'].append(win_s.reshape(bs, wb, 2, G, d_head))
            x = _out_proj(jnp.concatenate([o_p, o_s], axis=0).astype(BF16), nsa_w_out, (j,), x, 1.0, "nsa_out")
        x = _ffn(x, prm, i, 1)
    y_prompt, y_sample = x[:mp].reshape(bp, sp, dm), x[mp:].reshape(bs, ss, dm)
    st = {k: jnp.stack(v) for k, v in outs.items()}
    return (y_prompt, y_sample, st['ssd_h_p'], st['ssd_h_s'], st['ssd_buf_p'], st['ssd_buf_s'],
            st['lru_h_p'], st['lru_h_s'], st['lru_buf_p'], st['lru_buf_s'],
            st['rows_p'], st['rows_s'], st['win_p'], st['win_s'])
```

```python
import functools
import math

import jax
import jax.numpy as jnp
from jax import lax
from jax.experimental import pallas as pl
from jax.experimental.pallas import tpu as pltpu

F32 = jnp.float32
BF16 = jnp.bfloat16

NORM_EPS = 1e-6
N_MIXERS = 3
SSD_HEAD_DIM = 64
SSD_STATE = 128
SSD_GROUPS = 8
SSD_CHUNK = 128
LRU_BLOCKS = 16
LRU_C = 8.0
NSA_HEADS = 32
NSA_GROUPS = 4
CMP_LEN = 32
CMP_STRIDE = 16
SEL_LEN = 64
SEL_TOPK = 16
WINDOW = 512
SEL_QBLOCK = 64
WIN_QBLOCK = 128
FORCE_BONUS = 1e4
NEG_INF = -1e30
REL_BUCKETS = 32
REL_MAX_DIST = 128

V7X_VMEM_LIMIT_BYTES = 56 * 1024 * 1024
LANE = 128
SUBLANE = 8


def _row_tile(m, cap):
    best = None
    for t in range(16, min(m, cap) + 1, 16):
        if m % t == 0:
            best = t
    assert best is not None, m
    return best


def _rmsnorm_kernel(x_ref, w_ref, o_ref):
    x = x_ref[...]
    y = x * lax.rsqrt(jnp.mean(x * x, axis=-1, keepdims=True) + NORM_EPS)
    o_ref[...] = (y * w_ref[...]).astype(o_ref.dtype)


def _rmsnorm(x, w_row):
    m, d = x.shape
    tm = _row_tile(m, 512)
    return pl.pallas_call(
        _rmsnorm_kernel,
        grid=(m // tm,),
        in_specs=[pl.BlockSpec((tm, d), lambda i: (i, 0)),
                  pl.BlockSpec((1, d), lambda i: (0, 0))],
        out_specs=pl.BlockSpec((tm, d), lambda i: (i, 0)),
        out_shape=jax.ShapeDtypeStruct((m, d), BF16),
        compiler_params=pltpu.CompilerParams(dimension_semantics=("parallel",)),
        name="rmsnorm",
    )(x, w_row)


def _mm_kernel(a_ref, w_ref, o_ref):
    acc = jnp.dot(a_ref[...], w_ref[...].astype(BF16), preferred_element_type=F32)
    o_ref[...] = acc.astype(o_ref.dtype)


def _mm_res_kernel(a_ref, w_ref, r_ref, o_ref, *, scale):
    acc = jnp.dot(a_ref[...], w_ref[...].astype(BF16), preferred_element_type=F32)
    o_ref[...] = r_ref[...] + scale * acc


def _mm_swiglu_kernel(a_ref, wg_ref, wu_ref, o_ref):
    a = a_ref[...]
    g = jnp.dot(a, wg_ref[...].astype(BF16), preferred_element_type=F32)
    u = jnp.dot(a, wu_ref[...].astype(BF16), preferred_element_type=F32)
    o_ref[...] = (jax.nn.silu(g) * u).astype(o_ref.dtype)


def _w_spec(w, lead, kdim, tn, k_blk, n_blk0):
    nlead = len(lead)
    return pl.BlockSpec((None,) * nlead + (kdim, tn),
                        lambda i, j: tuple(lead) + (k_blk, n_blk0 + j))


def _matmul(a, w, lead, *, a_k0=0, w_k0=0, kdim, n0, ndim, tn, tm_cap=1408,
            res=None, scale=1.0, out_dtype=F32, name="matmul"):
    m = a.shape[0]
    tm = _row_tile(m, tm_cap)
    assert a_k0 % kdim == 0 and w_k0 % kdim == 0 and n0 % tn == 0 and ndim % tn == 0
    a_spec = pl.BlockSpec((tm, kdim), lambda i, j: (i, a_k0 // kdim))
    w_spec = _w_spec(w, lead, kdim, tn, w_k0 // kdim, n0 // tn)
    o_spec = pl.BlockSpec((tm, tn), lambda i, j: (i, j))
    grid = (m // tm, ndim // tn)
    params = pltpu.CompilerParams(dimension_semantics=("parallel", "parallel"),
                                  vmem_limit_bytes=V7X_VMEM_LIMIT_BYTES)
    if res is None:
        return pl.pallas_call(
            _mm_kernel, grid=grid, in_specs=[a_spec, w_spec], out_specs=o_spec,
            out_shape=jax.ShapeDtypeStruct((m, ndim), out_dtype),
            compiler_params=params, name=name)(a, w)
    return pl.pallas_call(
        functools.partial(_mm_res_kernel, scale=scale), grid=grid,
        in_specs=[a_spec, w_spec, o_spec], out_specs=o_spec,
        out_shape=jax.ShapeDtypeStruct((m, ndim), F32),
        compiler_params=params, name=name)(a, w, res)


def _matmul_swiglu(a, w, lead, *, tn=256, tm_cap=1408):
    m, k = a.shape
    f = w.shape[-1] // 2
    tm = _row_tile(m, tm_cap)
    assert f % tn == 0
    a_spec = pl.BlockSpec((tm, k), lambda i, j: (i, 0))
    wg_spec = _w_spec(w, lead, k, tn, 0, 0)
    wu_spec = _w_spec(w, lead, k, tn, 0, f // tn)
    return pl.pallas_call(
        _mm_swiglu_kernel, grid=(m // tm, f // tn),
        in_specs=[a_spec, wg_spec, wu_spec],
        out_specs=pl.BlockSpec((tm, tn), lambda i, j: (i, j)),
        out_shape=jax.ShapeDtypeStruct((m, f), BF16),
        compiler_params=pltpu.CompilerParams(dimension_semantics=("parallel", "parallel"),
                                             vmem_limit_bytes=V7X_VMEM_LIMIT_BYTES),
        name="ffn_in_swiglu")(a, w, w)


def _out_proj(a, w, lead, res, scale, name):
    k = a.shape[1]
    n = w.shape[-1]
    nsplit = 1
    while (k // nsplit) > 5632 or k % nsplit:
        nsplit += 1
    kdim = k // nsplit
    assert kdim % LANE == 0
    out = res
    for s in range(nsplit):
        out = _matmul(a, w, lead, a_k0=s * kdim, w_k0=s * kdim, kdim=kdim, n0=0, ndim=n, tn=256,
                      res=out, scale=scale, name=name)
    return out


def _ffn(x, prm, i, which):
    xn = _rmsnorm(x, prm['ffn_norm'][i, which][None])
    h = _matmul_swiglu(xn, prm['w_ffn_in'], (i, which))
    return _out_proj(h, prm['w_ffn_out'], (i, which), x, 0.5, "ffn_out")


def _split3(v):
    hi = v.astype(BF16)
    r1 = v - hi.astype(F32)
    mid = r1.astype(BF16)
    lo = (r1 - mid.astype(F32)).astype(BF16)
    return hi, mid, lo


def _expand(parts, e):
    out = jnp.dot(parts[0], e, preferred_element_type=F32)
    for p in parts[1:]:
        out = out + jnp.dot(p, e, preferred_element_type=F32)
    return out


def _ssd_kernel(xbc_ref, prev_ref, conv8_ref, z_ref, dtr_ref, h0_ref, cw_ref, cb_ref, dtb_ref, alog_ref, dx_ref,
                nw_ref, ex_ref, y_ref, hl_ref, buf_ref, h_ref, xdt_ref, xw_ref, yoff_ref, yg_ref, *, q, d_inner,
                groups, nstate, hd, valid):
    c = pl.program_id(1)
    gw = d_inner // groups
    hpg = gw // hd
    conv_dim = buf_ref.shape[1]

    ppg = gw // LANE

    @pl.when(c == 0)
    def _():
        for g in range(groups):
            for p in range(ppg):
                r0 = (g * ppg + p) * LANE
                h_ref[g, :, p * LANE:(p + 1) * LANE] = h0_ref[r0:r0 + LANE, :].T

    buf_ref[0:SUBLANE, :] = jnp.where(c == 0, conv8_ref[...], prev_ref[...])
    buf_ref[SUBLANE:SUBLANE + q, :] = xbc_ref[...]
    nconv = cw_ref.shape[0]
    first = SUBLANE - (nconv - 1)
    cblk = 512
    for cbk in range(conv_dim // cblk):
        cols = slice(cbk * cblk, (cbk + 1) * cblk)
        s = buf_ref[pl.ds(first, q), cols] * cw_ref[0:1, cols]
        for k in range(1, nconv):
            s = s + buf_ref[pl.ds(first + k, q), cols] * cw_ref[k:k + 1, cols]
        buf_ref[SUBLANE:SUBLANE + q, cols] = jax.nn.silu(cb_ref[:, cols] + s)

    heads = dtr_ref.shape[1]
    row = lax.broadcasted_iota(jnp.int32, (q, heads), 0)
    dt = jax.nn.softplus(dtr_ref[...] + dtb_ref[...])
    if valid < q:
        dt = jnp.where(row < valid, dt, 0.0)
    cs = dt * (-jnp.exp(alog_ref[...]))
    sh = 1
    while sh < q:
        cs = cs + jnp.where(row >= sh, pltpu.roll(cs, sh, 0), 0.0)
        sh *= 2
    cs_t = cs.T
    cs_last = cs[q - 1:q, :]
    to_end = jnp.exp(cs_last - cs)
    dt_parts = _split3(dt)
    dtw_parts = _split3(dt * to_end)
    ecs_parts = _split3(jnp.exp(cs))
    dec_parts = _split3(jnp.broadcast_to(jnp.exp(cs_last), (SUBLANE, heads)))
    li = lax.broadcasted_iota(jnp.int32, (q, q), 0)
    si = lax.broadcasted_iota(jnp.int32, (q, q), 1)
    causal = li >= si
    lane = lax.broadcasted_iota(jnp.int32, (q, 2 * hd), 1)
    xs0 = SUBLANE
    for g in range(groups):
        gc = slice(g * gw, (g + 1) * gw)
        eg = ex_ref[:, gc]
        xs = buf_ref[xs0:xs0 + q, gc]
        xdt_ref[...] = (xs * _expand(dt_parts, eg)).astype(BF16)
        xw_ref[...] = (xs * _expand(dtw_parts, eg)).astype(BF16)
        bcol = d_inner + g * nstate
        ccol = d_inner + groups * nstate + g * nstate
        b_f = buf_ref[xs0:xs0 + q, bcol:bcol + nstate]
        bg = b_f.astype(BF16)
        bg_t = b_f.T.astype(BF16)
        cg = buf_ref[xs0:xs0 + q, ccol:ccol + nstate].astype(BF16)
        cbm = lax.dot_general(cg, bg, (((1,), (1,)), ((), ())), preferred_element_type=F32)
        h_t = h_ref[g]
        yoff_ref[...] = jnp.dot(cg, h_t.astype(BF16), preferred_element_type=F32) * _expand(ecs_parts, eg)
        st = jnp.dot(bg_t, xw_ref[...], preferred_element_type=F32)
        h_ref[g] = _expand(dec_parts, eg)[0:1, :] * h_t + st
        for p in range(hpg // 2):
            pc = slice(p * 2 * hd, (p + 1) * 2 * hd)
            ac = slice(g * gw + p * 2 * hd, g * gw + (p + 1) * 2 * hd)
            x_pair = xdt_ref[:, pc]
            ys = []
            for e in range(2):
                h = g * hpg + 2 * p + e
                seg = cs[:, h:h + 1] - cs_t[h:h + 1, :]
                m = (cbm * jnp.exp(jnp.where(causal, seg, NEG_INF))).astype(BF16)
                ys.append(jnp.dot(m, x_pair, preferred_element_type=F32))
            y = jnp.where(lane < hd, ys[0], ys[1]) + yoff_ref[:, pc]
            y = y + dx_ref[:, ac] * buf_ref[xs0:xs0 + q, ac]
            zz = z_ref[:, ac]
            yg_ref[:, pc] = y * (zz * jax.nn.sigmoid(zz))
        yg = yg_ref[...]
        yn = yg * lax.rsqrt(jnp.mean(yg * yg, axis=-1, keepdims=True) + NORM_EPS) * nw_ref[:, gc]
        y_ref[:, gc] = yn.astype(y_ref.dtype)

    @pl.when(c == pl.num_programs(1) - 1)
    def _():
        for g in range(groups):
            for p in range(ppg):
                r0 = (g * ppg + p) * LANE
                hl_ref[r0:r0 + LANE, :] = h_ref[g, :, p * LANE:(p + 1) * LANE].T


def _ssd_mix(xbc, z, dt_raw, row0, nb, nchunk, valid, conv_buf, h0, conv_w, conv_b, dt_bias, a_log, d_skip, norm_w,
             groups, nstate, hd):
    q = SSD_CHUNK
    conv_dim = xbc.shape[1]
    d_inner = z.shape[1]
    heads = dt_raw.shape[1]
    gw = d_inner // groups
    assert row0 % q == 0 and conv_dim == d_inner + 2 * groups * nstate and heads * hd == d_inner
    assert gw % LANE == 0 and 2 * hd == LANE and nstate == LANE and conv_dim % 512 == 0
    blk0 = row0 // q
    sub = q // SUBLANE
    nconv = conv_w.shape[0]
    conv8 = jnp.pad(conv_buf, ((0, 0), (SUBLANE - (nconv - 1), 0), (0, 0)))
    expand = (jnp.arange(heads)[:, None] == jnp.arange(d_inner)[None, :] // hd).astype(BF16)
    dx = jnp.repeat(d_skip, hd)[None]
    row_map = lambda b, c: (blk0 + b * nchunk + c, 0)
    full2 = lambda b, c: (0, 0)
    kern = functools.partial(_ssd_kernel, q=q, d_inner=d_inner, groups=groups, nstate=nstate, hd=hd, valid=valid)
    y, hl = pl.pallas_call(
        kern, grid=(nb, nchunk),
        in_specs=[pl.BlockSpec((q, conv_dim), row_map),
                  pl.BlockSpec((SUBLANE, conv_dim), lambda b, c: (jnp.maximum((blk0 + b * nchunk + c) * sub - 1, 0), 0)),
                  pl.BlockSpec((None, SUBLANE, conv_dim), lambda b, c: (b, 0, 0)),
                  pl.BlockSpec((q, d_inner), row_map),
                  pl.BlockSpec((q, heads), row_map),
                  pl.BlockSpec((None, heads * hd, nstate), lambda b, c: (b, 0, 0), pipeline_mode=pl.Buffered(1)),
                  pl.BlockSpec(conv_w.shape, full2),
                  pl.BlockSpec((1, conv_dim), full2),
                  pl.BlockSpec((1, heads), full2),
                  pl.BlockSpec((1, heads), full2),
                  pl.BlockSpec((1, d_inner), full2),
                  pl.BlockSpec((1, d_inner), full2),
                  pl.BlockSpec((heads, d_inner), full2, pipeline_mode=pl.Buffered(1))],
        out_specs=[pl.BlockSpec((q, d_inner), lambda b, c: (b * nchunk + c, 0)),
                   pl.BlockSpec((None, heads * hd, nstate), lambda b, c: (b, 0, 0))],
        out_shape=[jax.ShapeDtypeStruct((nb * nchunk * q, d_inner), BF16),
                   jax.ShapeDtypeStruct((nb, heads * hd, nstate), F32)],
        scratch_shapes=[pltpu.VMEM((q + SUBLANE, conv_dim), F32),
                        pltpu.VMEM((groups, nstate, gw), F32),
                        pltpu.VMEM((q, gw), BF16), pltpu.VMEM((q, gw), BF16),
                        pltpu.VMEM((q, gw), F32), pltpu.VMEM((q, gw), F32)],
        compiler_params=pltpu.CompilerParams(dimension_semantics=("parallel", "arbitrary"),
                                             vmem_limit_bytes=V7X_VMEM_LIMIT_BYTES),
        name="ssd_mix",
    )(xbc, xbc, conv8, z, dt_raw, h0.reshape(nb, heads * hd, nstate), conv_w, conv_b[None], dt_bias[None],
      a_log[None], dx, norm_w[None], expand)
    return y, hl.reshape(nb, heads, hd, nstate)


LRU_SCAN_LANES = 512


def _gelu_tanh(x):
    return 0.5 * x * (1.0 + jnp.tanh(math.sqrt(2.0 / math.pi) * (x + 0.044715 * (x * x * x))))


def _lru_kernel(gate_ref, xb_ref, prev_ref, conv8_ref, h0_ref, cw_ref, cb_ref, wr_ref, br_ref, wi_ref, bi_ref,
                lam_ref, y_ref, hl_ref, e_ref, a_ref, u_ref, h_ref, *, tm, width):
    t = pl.program_id(1)

    @pl.when(t == 0)
    def _():
        h_ref[...] = h0_ref[...]

    e_ref[0:SUBLANE, :] = jnp.where(t == 0, conv8_ref[...], prev_ref[...])
    e_ref[SUBLANE:SUBLANE + tm, :] = xb_ref[...]
    nconv = cw_ref.shape[0]
    first = SUBLANE - (nconv - 1)
    nblk = wr_ref.shape[0]
    bd = width // nblk
    for k in range(nblk):
        cols = slice(k * bd, (k + 1) * bd)
        s = e_ref[pl.ds(first, tm), cols] * cw_ref[0:1, cols]
        for c in range(1, nconv):
            s = s + e_ref[pl.ds(first + c, tm), cols] * cw_ref[c:c + 1, cols]
        xk = cb_ref[:, cols] + s
        xkb = xk.astype(BF16)
        r = jax.nn.sigmoid(jnp.dot(xkb, wr_ref[k], preferred_element_type=F32) + br_ref[:, cols])
        ig = jax.nn.sigmoid(jnp.dot(xkb, wi_ref[k], preferred_element_type=F32) + bi_ref[:, cols])
        log_a = (-LRU_C * r) * jax.nn.softplus(-lam_ref[:, cols])
        a_ref[:, cols] = jnp.exp(log_a)
        one_minus_a2 = -jnp.tanh(log_a) * (jnp.exp(2.0 * log_a) + 1.0)
        u_ref[:, cols] = jnp.sqrt(one_minus_a2) * ig * xk

    row = lax.broadcasted_iota(jnp.int32, (SUBLANE, LRU_SCAN_LANES), 0)
    for sl in range(width // LRU_SCAN_LANES):
        cols = slice(sl * LRU_SCAN_LANES, (sl + 1) * LRU_SCAN_LANES)

        def body(r, h):
            rows = pl.ds(pl.multiple_of(r * SUBLANE, SUBLANE), SUBLANE)
            a = a_ref[rows, cols]
            u = u_ref[rows, cols]
            for sh in (1, 2, 4):
                a_sh = jnp.where(row >= sh, pltpu.roll(a, sh, 0), 1.0)
                u_sh = jnp.where(row >= sh, pltpu.roll(u, sh, 0), 0.0)
                u = a * u_sh + u
                a = a * a_sh
            hs = a * h + u
            y_ref[rows, cols] = hs * _gelu_tanh(gate_ref[rows, cols])
            return hs[SUBLANE - 1:SUBLANE, :]

        h_ref[:, cols] = lax.fori_loop(0, tm // SUBLANE, body, h_ref[:, cols])
    hl_ref[...] = h_ref[...]


def _lru_mix(gx, row0, nb, seq, conv_buf, h0, conv_w, conv_b, w_r, b_r, w_i, b_i, lam):
    width = gx.shape[1] // 2
    tm = _row_tile(seq, 128) if seq % 16 == 0 else seq
    assert seq % tm == 0 and tm % SUBLANE == 0 and row0 % tm == 0 and width % LRU_SCAN_LANES == 0
    nt = seq // tm
    nconv = conv_w.shape[0]
    conv8 = jnp.pad(conv_buf, ((0, 0), (SUBLANE - (nconv - 1), 0), (0, 0)))
    blk0 = row0 // tm
    sub = tm // SUBLANE

    def row_map(b, t):
        return (blk0 + b * nt + t, 0)

    def xb_map(b, t):
        return (blk0 + b * nt + t, 1)

    def prev_map(b, t):
        return (jnp.maximum((blk0 + b * nt + t) * sub - 1, 0), 1)

    full2 = lambda b, t: (0, 0)
    full3 = lambda b, t: (0, 0, 0)
    y, hl = pl.pallas_call(
        functools.partial(_lru_kernel, tm=tm, width=width),
        grid=(nb, nt),
        in_specs=[pl.BlockSpec((tm, width), row_map),
                  pl.BlockSpec((tm, width), xb_map),
                  pl.BlockSpec((SUBLANE, width), prev_map),
                  pl.BlockSpec((None, SUBLANE, width), lambda b, t: (b, 0, 0)),
                  pl.BlockSpec((None, 1, width), lambda b, t: (b, 0, 0)),
                  pl.BlockSpec(conv_w.shape, full2),
                  pl.BlockSpec((1, width), full2),
                  pl.BlockSpec(w_r.shape, full3),
                  pl.BlockSpec((1, width), full2),
                  pl.BlockSpec(w_i.shape, full3),
                  pl.BlockSpec((1, width), full2),
                  pl.BlockSpec((1, width), full2)],
        out_specs=[pl.BlockSpec((tm, width), lambda b, t: (b * nt + t, 0)),
                   pl.BlockSpec((None, 1, width), lambda b, t: (b, 0, 0))],
        out_shape=[jax.ShapeDtypeStruct((nb * seq, width), F32),
                   jax.ShapeDtypeStruct((nb, 1, width), F32)],
        scratch_shapes=[pltpu.VMEM((tm + SUBLANE, width), F32),
                        pltpu.VMEM((tm, width), F32),
                        pltpu.VMEM((tm, width), F32),
                        pltpu.VMEM((1, width), F32)],
        compiler_params=pltpu.CompilerParams(dimension_semantics=("parallel", "arbitrary"),
                                             vmem_limit_bytes=V7X_VMEM_LIMIT_BYTES),
        name="lru_mix",
    )(gx, gx, gx, conv8, h0[:, None, :], conv_w, conv_b[None], w_r.astype(BF16), b_r.reshape(1, width),
      w_i.astype(BF16), b_i.reshape(1, width), lam[None])
    return y, hl[:, 0]


KEY_CHUNK = 128
TINY = 1e-30
ATT_CHUNKS = 2


def _bucket_of(dist):
    n = jnp.maximum(dist, 0)
    exact = REL_BUCKETS // 2
    log_ratio = jnp.log(jnp.maximum(n, 1).astype(F32) / exact) / math.log(REL_MAX_DIST / exact)
    large = jnp.minimum(exact + (log_ratio * (REL_BUCKETS - exact)).astype(jnp.int32), REL_BUCKETS - 1)
    return jnp.where(n < exact, n, large)


def _bias_from_buckets(bucket, tab_ref, h):
    out = jnp.full(bucket.shape, tab_ref[0, h], F32)
    for k in range(1, REL_BUCKETS):
        out = jnp.where(bucket == k, tab_ref[k, h], out)
    return out


def _head_rmsnorm(x, w_row):
    return x * lax.rsqrt(jnp.mean(x * x, axis=-1, keepdims=True) + NORM_EPS) * w_row


def _bias_tiles_kernel(tab_ref, o_ref, *, tq):
    i = lax.broadcasted_iota(jnp.int32, (tq, KEY_CHUNK), 0)
    j = lax.broadcasted_iota(jnp.int32, (tq, KEY_CHUNK), 1)
    buckets = [_bucket_of(i - j + KEY_CHUNK * k) for k in range(2)]
    nheads = o_ref.shape[1] // tq

    def body(h, carry):
        rows = pl.ds(pl.multiple_of(h * tq, SUBLANE), tq)
        for k in range(2):
            o_ref[k, rows, :] = _bias_from_buckets(buckets[k], tab_ref, h)
        o_ref[2, rows, :] = jnp.full((tq, KEY_CHUNK), tab_ref[REL_BUCKETS - 1, h], F32)
        return carry

    lax.fori_loop(0, nheads, body, 0)


def _bias_tiles(table, tq):
    nheads = table.shape[1]
    return pl.pallas_call(
        functools.partial(_bias_tiles_kernel, tq=tq),
        in_specs=[pl.BlockSpec(memory_space=pltpu.SMEM)],
        out_specs=pl.BlockSpec(memory_space=pltpu.VMEM),
        out_shape=jax.ShapeDtypeStruct((3, nheads * tq, KEY_CHUNK), F32),
        name="nsa_bias_tiles",
    )(table)


def _slot_spec(src, lead, slot, groups, d):
    gd = groups * d
    if src.shape[-2:] == (groups, d):
        nlead = len(lead)
        assert src.ndim == nlead + 5 and src.shape[-4] == KEY_CHUNK
        return pl.BlockSpec((None,) * (nlead + 1) + (KEY_CHUNK, None, groups, d),
                            lambda b, p, t: tuple(lead) + (t[b, p], 0, slot, 0, 0))
    assert not lead
    if src.ndim == 2:
        return pl.BlockSpec((KEY_CHUNK, gd), lambda b, p, t: (t[b, p], slot))
    assert src.ndim == 3 and src.shape[1] == KEY_CHUNK
    return pl.BlockSpec((None, KEY_CHUNK, gd), lambda b, p, t: (t[b, p], 0, slot))


def _slab(ref, g, d):
    return ref[:, g, :] if len(ref.shape) == 3 else ref[:, g * d:(g + 1) * d]


def _compress_ab_kernel(tbl_ref, x0_ref, x1_ref, pe_ref, w1_ref, o_ref, slab_ref, *, groups, d):
    half = CMP_LEN // 2
    nchunk = KEY_CHUNK // CMP_STRIDE
    for s, x_ref in enumerate((x0_ref, x1_ref)):
        for g in range(groups):
            slab_ref[s * groups + g] = _slab(x_ref, g, d)
    for s in range(2):
        acc = [jnp.zeros((groups * nchunk, d), F32) for _ in range(2)]
        for l in range(half):
            x = jnp.concatenate(
                [slab_ref[s * groups + g, pl.ds(l, nchunk, stride=CMP_STRIDE), :] for g in range(groups)], axis=0)
            for part in range(2):
                ll = part * half + l
                xa = (x + pe_ref[s, ll:ll + 1, :]).astype(BF16)
                acc[part] = acc[part] + jnp.dot(xa, w1_ref[s, ll], preferred_element_type=F32)
        for part in range(2):
            for g in range(groups):
                o_ref[s, part, g] = acc[part][g * nchunk:(g + 1) * nchunk]


def _compress_ab(src, lead, tbl, pe, w1b, groups, d):
    nb, npages = tbl.shape
    nchunk = KEY_CHUNK // CMP_STRIDE
    assert CMP_LEN == 2 * CMP_STRIDE
    grid_spec = pltpu.PrefetchScalarGridSpec(
        num_scalar_prefetch=1, grid=(nb, npages),
        in_specs=[_slot_spec(src, lead, 0, groups, d), _slot_spec(src, lead, 1, groups, d),
                  pl.BlockSpec(pe.shape, lambda b, p, t: (0, 0, 0)),
                  pl.BlockSpec(w1b.shape, lambda b, p, t: (0, 0, 0, 0))],
        out_specs=pl.BlockSpec((None, 2, 2, groups, nchunk, d), lambda b, p, t: (b, 0, 0, 0, p, 0)),
        scratch_shapes=[pltpu.VMEM((2 * groups, KEY_CHUNK, d), F32)])
    return pl.pallas_call(
        functools.partial(_compress_ab_kernel, groups=groups, d=d),
        grid_spec=grid_spec,
        out_shape=jax.ShapeDtypeStruct((nb, 2, 2, groups, npages * nchunk, d), F32),
        compiler_params=pltpu.CompilerParams(dimension_semantics=("parallel", "arbitrary")),
        name="nsa_compress_ab",
    )(tbl, src, src, pe, w1b)


def _compress_finish_kernel(ab_ref, w2_ref, kw_ref, o_ref, *, groups, d):
    n = ab_ref.shape[3]
    for s in range(2):
        for g in range(groups):
            first = ab_ref[s, 0, g]
            second = pltpu.roll(ab_ref[s, 1, g], n - 1, 0)
            hid = jax.nn.silu(first + second).astype(BF16)
            out = jnp.dot(hid, w2_ref[s], preferred_element_type=F32)
            if s == 0:
                out = _head_rmsnorm(out, kw_ref[...])
            o_ref[s, :, g * d:(g + 1) * d] = out.astype(o_ref.dtype)


def _compress_finish(ab, w2b, k_norm_row, groups, d):
    nb, _, _, _, n, _ = ab.shape
    return pl.pallas_call(
        functools.partial(_compress_finish_kernel, groups=groups, d=d),
        grid=(nb,),
        in_specs=[pl.BlockSpec((None, 2, 2, groups, n, d), lambda b: (b, 0, 0, 0, 0, 0)),
                  pl.BlockSpec(w2b.shape, lambda b: (0, 0, 0)),
                  pl.BlockSpec((1, d), lambda b: (0, 0))],
        out_specs=pl.BlockSpec((None, 2, n, groups * d), lambda b: (b, 0, 0, 0)),
        out_shape=jax.ShapeDtypeStruct((nb, 2, n, groups * d), BF16),
        compiler_params=pltpu.CompilerParams(dimension_semantics=("parallel",)),
        name="nsa_compress_finish",
    )(ab, w2b, k_norm_row)


def _select_kernel(tab_ref, q_ref, kcv_ref, qw_ref, ocmp_ref, msel_ref, *, tq, groups, hpg, d, q_pos0, n_cmp, n_sel,
                   nselp, scale):
    qi = pl.program_id(1)
    pos0 = q_pos0 + qi * tq
    nck = kcv_ref.shape[1]
    t_c = lax.broadcasted_iota(jnp.int32, (tq, nck), 0)
    n_c = lax.broadcasted_iota(jnp.int32, (tq, nck), 1)
    dist = pos0 + t_c - (n_c * CMP_STRIDE + CMP_LEN - 1)
    cmask = (dist >= 0) & (n_c < n_cmp)
    cmaskf = cmask.astype(F32)
    bucket = _bucket_of(dist)
    n_r = lax.broadcasted_iota(jnp.int32, (nck, nselp), 0)
    j_r = lax.broadcasted_iota(jnp.int32, (nck, nselp), 1)
    cover = ((n_r * CMP_STRIDE < j_r * SEL_LEN + SEL_LEN) & (n_r * CMP_STRIDE + CMP_LEN > j_r * SEL_LEN)
             & (n_r < n_cmp) & (j_r < n_sel)).astype(BF16)
    j_s = lax.broadcasted_iota(jnp.int32, (tq, nselp), 1)
    pos = pos0 + lax.broadcasted_iota(jnp.int32, (tq, nselp), 0)
    cur = pos // SEL_LEN
    visible = (j_s * SEL_LEN <= pos) & (j_s < n_sel)
    forced = ((j_s == 0) | (j_s == cur) | (j_s == cur - 1)).astype(F32)
    for g in range(groups):
        kc = kcv_ref[0, :, g * d:(g + 1) * d]
        vc = kcv_ref[1, :, g * d:(g + 1) * d]
        p_grp = jnp.zeros((tq, nck), F32)
        for e in range(hpg):
            h = g * hpg + e
            cols = slice(h * d, (h + 1) * d)
            qn = _head_rmsnorm(q_ref[:, cols], qw_ref[...]).astype(BF16)
            logits = lax.dot_general(qn, kc, (((1,), (1,)), ((), ())), preferred_element_type=F32) * scale
            logits = jnp.where(cmask, logits + _bias_from_buckets(bucket, tab_ref, h), NEG_INF)
            m = jnp.max(logits, axis=-1, keepdims=True)
            p = jnp.exp(logits - m)
            p = p / jnp.sum(p, axis=-1, keepdims=True) * cmaskf
            ocmp_ref[:, cols] = jnp.dot(p.astype(BF16), vc, preferred_element_type=F32)
            p_grp = p_grp + p
        p_sel = jnp.dot(p_grp.astype(BF16), cover, preferred_element_type=F32)
        score = jnp.where(visible, p_sel + FORCE_BONUS * forced, NEG_INF)
        cnt = jnp.zeros((tq, nselp), jnp.int32)
        for jp in range(n_sel):
            col = score[:, jp:jp + 1]
            beats = (col > score) | ((col == score) & (j_s > jp))
            cnt = cnt + beats.astype(jnp.int32)
        msel_ref[:, g * nselp:(g + 1) * nselp] = ((cnt < SEL_TOPK) & visible).astype(F32)


def _nsa_select(table, qp, row0, kcv, q_norm_row, *, nb, seq, tq, q_pos0, n_cmp, n_sel, groups, hpg, d):
    nq = seq // tq
    n_q = groups * hpg * d
    nselp = -(-n_sel // LANE) * LANE
    nck = kcv.shape[2]
    assert row0 % tq == 0
    blk0 = row0 // tq
    kern = functools.partial(_select_kernel, tq=tq, groups=groups, hpg=hpg, d=d, q_pos0=q_pos0, n_cmp=n_cmp,
                             n_sel=n_sel, nselp=nselp, scale=d ** -0.5)
    return pl.pallas_call(
        kern, grid=(nb, nq),
        in_specs=[pl.BlockSpec(memory_space=pltpu.SMEM),
                  pl.BlockSpec((tq, n_q), lambda b, q: (blk0 + b * nq + q, 0)),
                  pl.BlockSpec((None, 2, nck, groups * d), lambda b, q: (b, 0, 0, 0)),
                  pl.BlockSpec((1, d), lambda b, q: (0, 0))],
        out_specs=[pl.BlockSpec((tq, n_q), lambda b, q: (b * nq + q, 0)),
                   pl.BlockSpec((tq, groups * nselp), lambda b, q: (b * nq + q, 0))],
        out_shape=[jax.ShapeDtypeStruct((nb * seq, n_q), F32),
                   jax.ShapeDtypeStruct((nb * seq, groups * nselp), F32)],
        compiler_params=pltpu.CompilerParams(dimension_semantics=("parallel", "parallel"),
                                             vmem_limit_bytes=V7X_VMEM_LIMIT_BYTES),
        name="nsa_cmp_select",
    )(table, qp, kcv, q_norm_row)


def _kv_pack_kernel(tbl_ref, xk_ref, xv_ref, kw_ref, k_ref, v_ref, *, groups, d):
    for g in range(groups):
        cols = slice(g * d, (g + 1) * d)
        k_ref[:, cols] = _head_rmsnorm(_slab(xk_ref, g, d), kw_ref[...]).astype(k_ref.dtype)
        v_ref[:, cols] = _slab(xv_ref, g, d).astype(v_ref.dtype)


def _kv_pack(src, lead, tbl, k_slot, k_norm_row, groups, d):
    nb, npages = tbl.shape
    gd = groups * d
    grid_spec = pltpu.PrefetchScalarGridSpec(
        num_scalar_prefetch=1, grid=(nb, npages),
        in_specs=[_slot_spec(src, lead, k_slot, groups, d), _slot_spec(src, lead, k_slot + 1, groups, d),
                  pl.BlockSpec((1, d), lambda b, p, t: (0, 0))],
        out_specs=[pl.BlockSpec((None, KEY_CHUNK, gd), lambda b, p, t: (b, p, 0)),
                   pl.BlockSpec((None, KEY_CHUNK, gd), lambda b, p, t: (b, p, 0))])
    return pl.pallas_call(
        functools.partial(_kv_pack_kernel, groups=groups, d=d),
        grid_spec=grid_spec,
        out_shape=[jax.ShapeDtypeStruct((nb, npages * KEY_CHUNK, gd), BF16)] * 2,
        compiler_params=pltpu.CompilerParams(dimension_semantics=("parallel", "parallel")),
        name="nsa_kv_pack",
    )(tbl, src, src, k_norm_row)


def _attend_kernel(tiles_ref, q_ref, ocmp_ref, gate_ref, msel_ref, ks_ref, vs_ref, kw_ref, vw_ref, qw_ref, o_ref,
                   qn_ref, ms_ref, ls_ref, as_ref, mw_ref, lw_ref, aw_ref, *, tq, groups, hpg, d, q_pos0,
                   win_chunk0, n_win_steps, nselp, scale):
    qi = pl.program_id(1)
    c = pl.program_id(2)
    pos0 = q_pos0 + qi * tq
    p_hi = (pos0 + tq - 1) // KEY_CHUNK // ATT_CHUNKS
    rows_g = hpg * tq
    nheads = groups * hpg

    @pl.when(c == 0)
    def _():
        for h in range(nheads):
            qn_ref[h * tq:(h + 1) * tq, :] = _head_rmsnorm(q_ref[:, h * d:(h + 1) * d], qw_ref[...]).astype(BF16)
        for m_ref, l_ref, a_ref in ((ms_ref, ls_ref, as_ref), (mw_ref, lw_ref, aw_ref)):
            m_ref[...] = jnp.full(m_ref.shape, NEG_INF, F32)
            l_ref[...] = jnp.zeros(l_ref.shape, F32)
            a_ref[...] = jnp.zeros(a_ref.shape, F32)

    assert tq & (tq - 1) == 0 and d == LANE
    t_idx = lax.broadcasted_iota(jnp.int32, (rows_g, KEY_CHUNK), 0) & (tq - 1)
    j_idx = lax.broadcasted_iota(jnp.int32, (rows_g, KEY_CHUNK), 1)
    ones_v = jnp.ones((KEY_CHUNK, d), BF16)

    def step(g, k_ref, v_ref, m_ref, l_ref, a_ref, chunk0, masks):
        rows = slice(g * rows_g, (g + 1) * rows_g)
        q = qn_ref[rows, :]
        logits = []
        for u, mask in enumerate(masks):
            delta = pos0 - (chunk0 + u) * KEY_CHUNK
            tile = jnp.clip(delta // KEY_CHUNK, 0, 2)
            k = k_ref[u * KEY_CHUNK:(u + 1) * KEY_CHUNK, g * d:(g + 1) * d]
            s = lax.dot_general(q, k, (((1,), (1,)), ((), ())), preferred_element_type=F32) * scale
            s = s + tiles_ref[tile, pl.ds(g * rows_g, rows_g), :]
            logits.append(jnp.where(mask, s, NEG_INF))
        m_prev = m_ref[rows, :]
        m_new = m_prev
        for s in logits:
            m_new = jnp.maximum(m_new, jnp.max(s, axis=-1, keepdims=True))
        alpha = jnp.exp(m_prev - m_new)
        pv = None
        for u, (s, mask) in enumerate(zip(logits, masks)):
            p = jnp.where(mask, jnp.exp(s - m_new), 0.0).astype(BF16)
            v_ext = jnp.concatenate([v_ref[u * KEY_CHUNK:(u + 1) * KEY_CHUNK, g * d:(g + 1) * d], ones_v], axis=1)
            part = jnp.dot(p, v_ext, preferred_element_type=F32)
            pv = part if pv is None else pv + part
        l_ref[rows, :] = alpha * l_ref[rows, :] + pv[:, d:]
        a_ref[rows, :] = alpha * a_ref[rows, :] + pv[:, :d]
        m_ref[rows, :] = m_new

    @pl.when(c <= p_hi)
    def _():
        chunk0 = c * ATT_CHUNKS
        blocks_per_chunk = KEY_CHUNK // SEL_LEN
        jr = lax.broadcasted_iota(jnp.int32, (nselp, KEY_CHUNK), 0)
        lr = lax.broadcasted_iota(jnp.int32, (nselp, KEY_CHUNK), 1)
        causal, expand = [], []
        for u in range(ATT_CHUNKS):
            causal.append(pos0 - (chunk0 + u) * KEY_CHUNK + t_idx - j_idx >= 0)
            expand.append((jr == (chunk0 + u) * blocks_per_chunk + lr // SEL_LEN).astype(BF16))
        for g in range(groups):
            ms = msel_ref[:, g * nselp:(g + 1) * nselp].astype(BF16)
            masks = []
            for u in range(ATT_CHUNKS):
                sel = jnp.dot(ms, expand[u], preferred_element_type=F32)
                masks.append(causal[u] & (jnp.concatenate([sel] * hpg, axis=0) > 0.5))
            step(g, ks_ref, vs_ref, ms_ref, ls_ref, as_ref, chunk0, masks)

    wpair = p_hi - c

    @pl.when((c < n_win_steps) & (wpair * ATT_CHUNKS >= win_chunk0))
    def _():
        chunk0 = wpair * ATT_CHUNKS
        masks = []
        for u in range(ATT_CHUNKS):
            dist = pos0 - (chunk0 + u) * KEY_CHUNK + t_idx - j_idx
            masks.append((dist >= 0) & (dist < WINDOW))
        for g in range(groups):
            step(g, kw_ref, vw_ref, mw_ref, lw_ref, aw_ref, chunk0, masks)

    @pl.when(c == pl.num_programs(2) - 1)
    def _():
        gates = jax.nn.sigmoid(gate_ref[...])
        for h in range(nheads):
            r = slice(h * tq, (h + 1) * tq)
            cols = slice(h * d, (h + 1) * d)
            o_sel = as_ref[r, :] / jnp.maximum(ls_ref[r, :], TINY)
            o_win = aw_ref[r, :] / jnp.maximum(lw_ref[r, :], TINY)
            o_ref[:, cols] = (gates[:, 3 * h:3 * h + 1] * ocmp_ref[:, cols] + gates[:, 3 * h + 1:3 * h + 2] * o_sel
                              + gates[:, 3 * h + 2:3 * h + 3] * o_win)


def _nsa_attend(tiles, qp, gp, row0, ocmp, msel, ksel, vsel, kwin, vwin, q_norm_row, *, nb, seq, tq, q_pos0,
                win_chunk0, groups, hpg, d):
    nq = seq // tq
    n_q = groups * hpg * d
    gd = groups * d
    nheads = groups * hpg
    nselp = msel.shape[1] // groups
    pair = ATT_CHUNKS * KEY_CHUNK
    assert ksel.shape[1] % pair == 0 and kwin.shape[1] % pair == 0 and win_chunk0 % ATT_CHUNKS == 0
    npair = ksel.shape[1] // pair
    nwpair = kwin.shape[1] // pair
    wpair0 = win_chunk0 // ATT_CHUNKS
    n_win_steps = (WINDOW // KEY_CHUNK + ATT_CHUNKS - 1) // ATT_CHUNKS + 1
    ncg = max(npair, n_win_steps)
    assert row0 % tq == 0 and q_pos0 % KEY_CHUNK == 0 and (tq == KEY_CHUNK or nq == 1) and tq <= KEY_CHUNK
    blk0 = row0 // tq

    def p_hi(q):
        return (q_pos0 + q * tq + tq - 1) // KEY_CHUNK // ATT_CHUNKS

    def sel_map(b, q, c):
        return (b, jnp.minimum(c, p_hi(q)), 0)

    def win_map(b, q, c):
        return (b, jnp.clip(p_hi(q) - jnp.minimum(c, n_win_steps - 1) - wpair0, 0, nwpair - 1), 0)

    grp_map = lambda b, q, c: (b * nq + q, 0)
    all_map = lambda b, q, c: (blk0 + b * nq + q, 0)
    kern = functools.partial(_attend_kernel, tq=tq, groups=groups, hpg=hpg, d=d, q_pos0=q_pos0,
                             win_chunk0=win_chunk0, n_win_steps=n_win_steps, nselp=nselp, scale=d ** -0.5)
    return pl.pallas_call(
        kern, grid=(nb, nq, ncg),
        in_specs=[pl.BlockSpec(tiles.shape, lambda b, q, c: (0, 0, 0)),
                  pl.BlockSpec((tq, n_q), all_map),
                  pl.BlockSpec((tq, n_q), grp_map),
                  pl.BlockSpec((tq, gp.shape[1]), all_map),
                  pl.BlockSpec((tq, groups * nselp), grp_map),
                  pl.BlockSpec((None, pair, gd), sel_map),
                  pl.BlockSpec((None, pair, gd), sel_map),
                  pl.BlockSpec((None, pair, gd), win_map),
                  pl.BlockSpec((None, pair, gd), win_map),
                  pl.BlockSpec((1, d), lambda b, q, c: (0, 0))],
        out_specs=pl.BlockSpec((tq, n_q), grp_map),
        out_shape=jax.ShapeDtypeStruct((nb * seq, n_q), F32),
        scratch_shapes=[pltpu.VMEM((nheads * tq, d), BF16),
                        pltpu.VMEM((nheads * tq, d), F32), pltpu.VMEM((nheads * tq, d), F32),
                        pltpu.VMEM((nheads * tq, d), F32),
                        pltpu.VMEM((nheads * tq, d), F32), pltpu.VMEM((nheads * tq, d), F32),
                        pltpu.VMEM((nheads * tq, d), F32)],
        compiler_params=pltpu.CompilerParams(dimension_semantics=("parallel", "parallel", "arbitrary"),
                                             vmem_limit_bytes=V7X_VMEM_LIMIT_BYTES),
        name="nsa_attend",
    )(tiles, qp, ocmp, gp, msel, ksel, vsel, kwin, vwin, q_norm_row)


def _nsa_group(qp, kvp, gp, row0, nb, seq, q_pos0, past, win_past, w, groups, hpg, d):
    q_norm_w, k_norm_w, cmp_pe, cmp_w1, cmp_w2, table = w
    gd = groups * d
    tq = KEY_CHUNK if seq % KEY_CHUNK == 0 else seq
    assert tq % SUBLANE == 0
    w1b, w2b = cmp_w1.astype(BF16), cmp_w2.astype(BF16)
    if past is None:
        assert seq % KEY_CHUNK == 0 and row0 % KEY_CHUNK == 0
        npg = seq // KEY_CHUNK
        tbl = row0 // KEY_CHUNK + jnp.arange(nb * npg, dtype=jnp.int32).reshape(nb, npg)
        src = kvp
        ab = _compress_ab(src, (), tbl, cmp_pe, w1b, groups, d)
        ksel, vsel = _kv_pack(src, (), tbl, 2, k_norm_w[1:2], groups, d)
        kwin, vwin = _kv_pack(src, (), tbl, 4, k_norm_w[2:3], groups, d)
        tk = seq
        win_chunk0 = 0
    else:
        cache, lead, tbl = past
        npg = tbl.shape[1]
        past_len = npg * KEY_CHUNK
        assert q_pos0 == past_len and seq < CMP_STRIDE and seq <= KEY_CHUNK
        ab = _compress_ab(cache, lead, tbl, cmp_pe, w1b, groups, d)
        ksel_p, vsel_p = _kv_pack(cache, lead, tbl, 2, k_norm_w[1:2], groups, d)
        wb = win_past.shape[1]
        assert wb % KEY_CHUNK == 0 and wb <= WINDOW and (past_len - wb) % KEY_CHUNK == 0
        nwp = wb // KEY_CHUNK
        wtbl = jnp.arange(nb * nwp, dtype=jnp.int32).reshape(nb, nwp)
        win_pages = win_past.reshape((nb * nwp, KEY_CHUNK) + win_past.shape[2:])
        kwin_p, vwin_p = _kv_pack(win_pages, (), wtbl, 0, k_norm_w[2:3], groups, d)
        new = kvp[row0:row0 + nb * seq].reshape(nb, seq, kvp.shape[1])
        new = jnp.pad(new, ((0, 0), (0, KEY_CHUNK - seq), (0, 0)))
        ntbl = jnp.arange(nb, dtype=jnp.int32).reshape(nb, 1)
        ksel_n, vsel_n = _kv_pack(new, (), ntbl, 2, k_norm_w[1:2], groups, d)
        kwin_n, vwin_n = _kv_pack(new, (), ntbl, 4, k_norm_w[2:3], groups, d)

        def join(past_part, new_part):
            rows = past_part.shape[1] + new_part.shape[1]
            fill = jnp.zeros((nb, -rows % (ATT_CHUNKS * KEY_CHUNK), gd), past_part.dtype)
            return jnp.concatenate([past_part, new_part, fill], axis=1)

        ksel, vsel = join(ksel_p, ksel_n), join(vsel_p, vsel_n)
        kwin, vwin = join(kwin_p, kwin_n), join(vwin_p, vwin_n)
        tk = past_len + seq
        win_chunk0 = (past_len - wb) // KEY_CHUNK
    n_cmp = (tk - CMP_LEN) // CMP_STRIDE + 1
    n_sel = -(-tk // SEL_LEN)
    assert n_cmp <= ab.shape[4] - 1
    kcv = _compress_finish(ab, w2b, k_norm_w[0:1], groups, d)
    ocmp, msel = _nsa_select(table, qp, row0, kcv, q_norm_w[None], nb=nb, seq=seq, tq=tq, q_pos0=q_pos0,
                             n_cmp=n_cmp, n_sel=n_sel, groups=groups, hpg=hpg, d=d)
    tiles = _bias_tiles(table, tq)
    return _nsa_attend(tiles, qp, gp, row0, ocmp, msel, ksel, vsel, kwin, vwin, q_norm_w[None], nb=nb, seq=seq,
                       tq=tq, q_pos0=q_pos0, win_chunk0=win_chunk0, groups=groups, hpg=hpg, d=d)


def kernel(x_prompt, x_sample, state_ssd, state_ssd_conv, state_lru, state_lru_conv, cache_nsa_kv, cache_nsa_win, page_table, ffn_norm, w_ffn_in, w_ffn_out, mix_norm, ssd_w_in, ssd_conv_w, ssd_conv_b, ssd_dt_bias, ssd_a_log, ssd_d, ssd_norm, ssd_w_out, lru_w_in, lru_conv_w, lru_conv_b, lru_w_r, lru_b_r, lru_w_i, lru_b_i, lru_lambda, lru_w_out, nsa_w_in, nsa_q_norm, nsa_k_norm, nsa_cmp_pe, nsa_cmp_w1, nsa_cmp_w2, rel_bias_table, nsa_w_out):
    prm = dict(ffn_norm=ffn_norm, w_ffn_in=w_ffn_in, w_ffn_out=w_ffn_out)
    bp, sp, dm = x_prompt.shape
    bs, ss, _ = x_sample.shape
    mp, ms = bp * sp, bs * ss
    depth = mix_norm.shape[0]
    G = NSA_GROUPS
    E = NSA_HEADS // G
    d_head = dm // NSA_HEADS
    wb = cache_nsa_win.shape[2]
    n_pages = page_table.shape[1]
    page = cache_nsa_kv.shape[2]
    past_len = n_pages * page
    assert page == KEY_CHUNK

    def merge(tp, ts):
        return jnp.concatenate([tp.reshape(mp, -1), ts.reshape(ms, -1)], axis=0)

    def tail(buf, rows, prompt, c0, c1):
        n = buf.shape[1]
        nb_, seq, r0 = (bp, sp, 0) if prompt else (bs, ss, mp)
        k = min(n, seq)
        new = rows[r0:r0 + nb_ * seq].reshape(nb_, seq, -1)[:, seq - k:, c0:c1]
        return jnp.concatenate([buf, new.reshape((nb_, k) + buf.shape[2:])], axis=1)[:, -n:]

    x = merge(x_prompt, x_sample)
    outs = {k: [] for k in ('ssd_h_p', 'ssd_h_s', 'ssd_buf_p', 'ssd_buf_s', 'lru_h_p', 'lru_h_s',
                            'lru_buf_p', 'lru_buf_s', 'rows_p', 'rows_s', 'win_p', 'win_s')}
    for i in range(depth):
        kind, j = i % N_MIXERS, i // N_MIXERS
        x = _ffn(x, prm, i, 0)
        hn = _rmsnorm(x, mix_norm[i][None])
        if kind == 0:
            d_inner = ssd_w_out.shape[1]
            conv_dim = ssd_conv_w.shape[2]
            heads = ssd_dt_bias.shape[1]
            z = _matmul(hn, ssd_w_in, (j,), kdim=dm, n0=0, ndim=d_inner, tn=512, name="ssd_in_z")
            xbc = _matmul(hn, ssd_w_in, (j,), kdim=dm, n0=d_inner, ndim=conv_dim, tn=512, name="ssd_in_xbc")
            dt_raw = _matmul(hn, ssd_w_in, (j,), kdim=dm, n0=d_inner + conv_dim, ndim=heads, tn=heads,
                             name="ssd_in_dt")
            w = (ssd_conv_w[j], ssd_conv_b[j], ssd_dt_bias[j], ssd_a_log[j], ssd_d[j], ssd_norm[j])
            q = SSD_CHUNK
            assert sp % q == 0 and ss <= q
            nbuf = ssd_conv_w.shape[1] - 1
            hd, nstate = d_inner // heads, state_ssd.shape[-1]
            shape = (SSD_GROUPS, nstate, hd)
            zero_buf = jnp.zeros((bp, nbuf, conv_dim), F32)
            zero_h = jnp.zeros((bp,) + state_ssd.shape[2:], F32)
            y_p, st_p = _ssd_mix(xbc, z, dt_raw, 0, bp, sp // q, q, zero_buf, zero_h, *w, *shape)

            def pad_s(t):
                return jnp.pad(t[mp:].reshape(bs, ss, -1), ((0, 0), (0, q - ss), (0, 0))).reshape(bs * q, -1)

            y_s, st_s = _ssd_mix(pad_s(xbc), pad_s(z), pad_s(dt_raw), 0, bs, 1, ss, state_ssd_conv[j], state_ssd[j],
                                 *w, *shape)
            y_s = y_s.reshape(bs, q, d_inner)[:, :ss].reshape(ms, d_inner)
            outs['ssd_h_p'].append(st_p); outs['ssd_h_s'].append(st_s)
            outs['ssd_buf_p'].append(tail(zero_buf, xbc, True, 0, conv_dim))
            outs['ssd_buf_s'].append(tail(state_ssd_conv[j], xbc, False, 0, conv_dim))
            x = _out_proj(jnp.concatenate([y_p, y_s], axis=0), ssd_w_out, (j,), x, 1.0, "ssd_out")
        elif kind == 1:
            width = lru_w_out.shape[1]
            nbuf = lru_conv_w.shape[1] - 1
            gx = _matmul(hn, lru_w_in, (j,), kdim=dm, n0=0, ndim=2 * width, tn=512, name="lru_in")
            w = (lru_conv_w[j], lru_conv_b[j], lru_w_r[j], lru_b_r[j], lru_w_i[j], lru_b_i[j], lru_lambda[j])
            zero_buf = jnp.zeros((bp, nbuf, width), F32)
            y_p, st_p = _lru_mix(gx, 0, bp, sp, zero_buf, jnp.zeros((bp, width), F32), *w)
            y_s, st_s = _lru_mix(gx, mp, bs, ss, state_lru_conv[j], state_lru[j], *w)
            outs['lru_h_p'].append(st_p); outs['lru_h_s'].append(st_s)
            outs['lru_buf_p'].append(tail(zero_buf, gx, True, width, 2 * width))
            outs['lru_buf_s'].append(tail(state_lru_conv[j], gx, False, width, 2 * width))
            x = _out_proj(jnp.concatenate([y_p, y_s], axis=0).astype(BF16), lru_w_out, (j,), x, 1.0, "lru_out")
        else:
            n_q, n_rows, n_kv = NSA_HEADS * d_head, 4 * G * d_head, 6 * G * d_head
            qp = _matmul(hn, nsa_w_in, (j,), kdim=dm, n0=0, ndim=n_q, tn=512, name="nsa_in_q")
            kvp = _matmul(hn, nsa_w_in, (j,), kdim=dm, n0=n_q, ndim=n_kv, tn=512, name="nsa_in_kv")
            n_gate = nsa_w_in.shape[2] - n_q - n_kv
            w_gate = jnp.pad(nsa_w_in[j][None, :, n_q + n_kv:], ((0, 0), (0, 0), (0, LANE - n_gate)))
            gp = _matmul(hn, w_gate, (0,), kdim=dm, n0=0, ndim=LANE, tn=LANE, name="nsa_in_gate")[:, :n_gate]
            w = (nsa_q_norm[j], nsa_k_norm[j], nsa_cmp_pe[j], nsa_cmp_w1[j], nsa_cmp_w2[j], rel_bias_table)
            o_p = _nsa_group(qp, kvp, gp, 0, bp, sp, 0, None, None, w, G, E, d_head)
            o_s = _nsa_group(qp, kvp, gp, mp, bs, ss, past_len, (cache_nsa_kv, (j,), page_table), cache_nsa_win[j],
                             w, G, E, d_head)
            outs['rows_p'].append(kvp[:mp, :n_rows].reshape(bp, sp, 4, G, d_head))
            outs['rows_s'].append(kvp[mp:, :n_rows].reshape(bs, ss, 4, G, d_head))
            outs['win_p'].append(tail(jnp.zeros((bp,) + cache_nsa_win.shape[2:], F32), kvp, True, n_rows, n_kv))
            outs['win_s'].append(tail(cache_nsa_win[j], kvp, False, n_rows, n_kv))
            x = _out_proj(jnp.concatenate([o_p, o_s], axis=0).astype(BF16), nsa_w_out, (j,), x, 1.0, "nsa_out")
        x = _ffn(x, prm, i, 1)
    y_prompt, y_sample = x[:mp].reshape(bp, sp, dm), x[mp:].reshape(bs, ss, dm)
    st = {k: jnp.stack(v) for k, v in outs.items()}
    return (y_prompt, y_sample, st['ssd_h_p'], st['ssd_h_s'], st['ssd_buf_p'], st['ssd_buf_s'],
            st['lru_h_p'], st['lru_h_s'], st['lru_buf_p'], st['lru_buf_s'],
            st['rows_p'], st['rows_s'], st['win_p'], st['win_s'])
```

```python
import functools
import math

import jax
import jax.numpy as jnp
from jax import lax
from jax.experimental import pallas as pl
from jax.experimental.pallas import tpu as pltpu

F32 = jnp.float32
BF16 = jnp.bfloat16

NORM_EPS = 1e-6
N_MIXERS = 3
SSD_HEAD_DIM = 64
SSD_STATE = 128
SSD_GROUPS = 8
SSD_CHUNK = 128
LRU_BLOCKS = 16
LRU_C = 8.0
NSA_HEADS = 32
NSA_GROUPS = 4
CMP_LEN = 32
CMP_STRIDE = 16
SEL_LEN = 64
SEL_TOPK = 16
WINDOW = 512
SEL_QBLOCK = 64
WIN_QBLOCK = 128
FORCE_BONUS = 1e4
NEG_INF = -1e30
REL_BUCKETS = 32
REL_MAX_DIST = 128

V7X_VMEM_LIMIT_BYTES = 56 * 1024 * 1024
LANE = 128
SUBLANE = 8


def _row_tile(m, cap):
    best = None
    for t in range(16, min(m, cap) + 1, 16):
        if m % t == 0:
            best = t
    assert best is not None, m
    return best


def _rmsnorm_kernel(x_ref, w_ref, o_ref):
    x = x_ref[...]
    y = x * lax.rsqrt(jnp.mean(x * x, axis=-1, keepdims=True) + NORM_EPS)
    o_ref[...] = (y * w_ref[...]).astype(o_ref.dtype)


def _rmsnorm(x, w_row):
    m, d = x.shape
    tm = _row_tile(m, 512)
    return pl.pallas_call(
        _rmsnorm_kernel,
        grid=(m // tm,),
        in_specs=[pl.BlockSpec((tm, d), lambda i: (i, 0)),
                  pl.BlockSpec((1, d), lambda i: (0, 0))],
        out_specs=pl.BlockSpec((tm, d), lambda i: (i, 0)),
        out_shape=jax.ShapeDtypeStruct((m, d), BF16),
        compiler_params=pltpu.CompilerParams(dimension_semantics=("parallel",)),
        name="rmsnorm",
    )(x, w_row)


def _mm_kernel(a_ref, w_ref, o_ref):
    acc = jnp.dot(a_ref[...], w_ref[...].astype(BF16), preferred_element_type=F32)
    o_ref[...] = acc.astype(o_ref.dtype)


def _mm_res_kernel(a_ref, w_ref, r_ref, o_ref, *, scale):
    acc = jnp.dot(a_ref[...], w_ref[...].astype(BF16), preferred_element_type=F32)
    o_ref[...] = r_ref[...] + scale * acc


def _mm_swiglu_kernel(a_ref, wg_ref, wu_ref, o_ref):
    a = a_ref[...]
    g = jnp.dot(a, wg_ref[...].astype(BF16), preferred_element_type=F32)
    u = jnp.dot(a, wu_ref[...].astype(BF16), preferred_element_type=F32)
    o_ref[...] = (jax.nn.silu(g) * u).astype(o_ref.dtype)


def _w_spec(w, lead, kdim, tn, k_blk, n_blk0):
    nlead = len(lead)
    return pl.BlockSpec((None,) * nlead + (kdim, tn),
                        lambda i, j: tuple(lead) + (k_blk, n_blk0 + j))


def _matmul(a, w, lead, *, a_k0=0, w_k0=0, kdim, n0, ndim, tn, tm_cap=1408,
            res=None, scale=1.0, out_dtype=F32, name="matmul"):
    m = a.shape[0]
    tm = _row_tile(m, tm_cap)
    assert a_k0 % kdim == 0 and w_k0 % kdim == 0 and n0 % tn == 0 and ndim % tn == 0
    a_spec = pl.BlockSpec((tm, kdim), lambda i, j: (i, a_k0 // kdim))
    w_spec = _w_spec(w, lead, kdim, tn, w_k0 // kdim, n0 // tn)
    o_spec = pl.BlockSpec((tm, tn), lambda i, j: (i, j))
    grid = (m // tm, ndim // tn)
    params = pltpu.CompilerParams(dimension_semantics=("parallel", "parallel"),
                                  vmem_limit_bytes=V7X_VMEM_LIMIT_BYTES)
    if res is None:
        return pl.pallas_call(
            _mm_kernel, grid=grid, in_specs=[a_spec, w_spec], out_specs=o_spec,
            out_shape=jax.ShapeDtypeStruct((m, ndim), out_dtype),
            compiler_params=params, name=name)(a, w)
    return pl.pallas_call(
        functools.partial(_mm_res_kernel, scale=scale), grid=grid,
        in_specs=[a_spec, w_spec, o_spec], out_specs=o_spec,
        out_shape=jax.ShapeDtypeStruct((m, ndim), F32),
        compiler_params=params, name=name)(a, w, res)


def _matmul_swiglu(a, w, lead, *, tn=256, tm_cap=1408):
    m, k = a.shape
    f = w.shape[-1] // 2
    tm = _row_tile(m, tm_cap)
    assert f % tn == 0
    a_spec = pl.BlockSpec((tm, k), lambda i, j: (i, 0))
    wg_spec = _w_spec(w, lead, k, tn, 0, 0)
    wu_spec = _w_spec(w, lead, k, tn, 0, f // tn)
    return pl.pallas_call(
        _mm_swiglu_kernel, grid=(m // tm, f // tn),
        in_specs=[a_spec, wg_spec, wu_spec],
        out_specs=pl.BlockSpec((tm, tn), lambda i, j: (i, j)),
        out_shape=jax.ShapeDtypeStruct((m, f), BF16),
        compiler_params=pltpu.CompilerParams(dimension_semantics=("parallel", "parallel"),
                                             vmem_limit_bytes=V7X_VMEM_LIMIT_BYTES),
        name="ffn_in_swiglu")(a, w, w)


def _out_proj(a, w, lead, res, scale, name):
    k = a.shape[1]
    n = w.shape[-1]
    nsplit = 1
    while (k // nsplit) > 5632 or k % nsplit:
        nsplit += 1
    kdim = k // nsplit
    assert kdim % LANE == 0
    out = res
    for s in range(nsplit):
        out = _matmul(a, w, lead, a_k0=s * kdim, w_k0=s * kdim, kdim=kdim, n0=0, ndim=n, tn=256,
                      res=out, scale=scale, name=name)
    return out


def _ffn(x, prm, i, which):
    xn = _rmsnorm(x, prm['ffn_norm'][i, which][None])
    h = _matmul_swiglu(xn, prm['w_ffn_in'], (i, which))
    return _out_proj(h, prm['w_ffn_out'], (i, which), x, 0.5, "ffn_out")


def _split3(v):
    hi = v.astype(BF16)
    r1 = v - hi.astype(F32)
    mid = r1.astype(BF16)
    lo = (r1 - mid.astype(F32)).astype(BF16)
    return hi, mid, lo


def _expand(parts, e):
    out = jnp.dot(parts[0], e, preferred_element_type=F32)
    for p in parts[1:]:
        out = out + jnp.dot(p, e, preferred_element_type=F32)
    return out


def _ssd_kernel(xbc_ref, prev_ref, conv8_ref, z_ref, dtr_ref, h0_ref, cw_ref, cb_ref, dtb_ref, alog_ref, dx_ref,
                nw_ref, ex_ref, y_ref, hl_ref, buf_ref, h_ref, xdt_ref, xw_ref, yoff_ref, yg_ref, *, q, d_inner,
                groups, nstate, hd, valid):
    c = pl.program_id(1)
    gw = d_inner // groups
    hpg = gw // hd
    conv_dim = buf_ref.shape[1]

    ppg = gw // LANE

    @pl.when(c == 0)
    def _():
        for g in range(groups):
            for p in range(ppg):
                r0 = (g * ppg + p) * LANE
                h_ref[g, :, p * LANE:(p + 1) * LANE] = h0_ref[r0:r0 + LANE, :].T

    buf_ref[0:SUBLANE, :] = jnp.where(c == 0, conv8_ref[...], prev_ref[...])
    buf_ref[SUBLANE:SUBLANE + valid, :] = xbc_ref[...]
    if valid < q:
        buf_ref[SUBLANE + valid:SUBLANE + q, :] = jnp.zeros((q - valid, conv_dim), F32)
    nconv = cw_ref.shape[0]
    first = SUBLANE - (nconv - 1)
    cblk = 512
    for cbk in range(conv_dim // cblk):
        cols = slice(cbk * cblk, (cbk + 1) * cblk)
        s = buf_ref[pl.ds(first, q), cols] * cw_ref[0:1, cols]
        for k in range(1, nconv):
            s = s + buf_ref[pl.ds(first + k, q), cols] * cw_ref[k:k + 1, cols]
        buf_ref[SUBLANE:SUBLANE + q, cols] = jax.nn.silu(cb_ref[:, cols] + s)

    heads = dtr_ref.shape[1]
    row = lax.broadcasted_iota(jnp.int32, (q, heads), 0)
    dt = jax.nn.softplus(dtr_ref[...] + dtb_ref[...])
    if valid < q:
        dt = jnp.concatenate([dt, jnp.zeros((q - valid, heads), F32)], axis=0)
    cs = dt * (-jnp.exp(alog_ref[...]))
    sh = 1
    while sh < q:
        cs = cs + jnp.where(row >= sh, pltpu.roll(cs, sh, 0), 0.0)
        sh *= 2
    cs_t = cs.T
    cs_last = cs[q - 1:q, :]
    to_end = jnp.exp(cs_last - cs)
    dt_parts = _split3(dt)
    dtw_parts = _split3(dt * to_end)
    ecs_parts = _split3(jnp.exp(cs))
    dec_parts = _split3(jnp.broadcast_to(jnp.exp(cs_last), (SUBLANE, heads)))
    li = lax.broadcasted_iota(jnp.int32, (q, q), 0)
    si = lax.broadcasted_iota(jnp.int32, (q, q), 1)
    causal = li >= si
    lane = lax.broadcasted_iota(jnp.int32, (q, 2 * hd), 1)
    xs0 = SUBLANE
    for g in range(groups):
        gc = slice(g * gw, (g + 1) * gw)
        eg = ex_ref[:, gc]
        xs = buf_ref[xs0:xs0 + q, gc]
        xdt_ref[...] = (xs * _expand(dt_parts, eg)).astype(BF16)
        xw_ref[...] = (xs * _expand(dtw_parts, eg)).astype(BF16)
        bcol = d_inner + g * nstate
        ccol = d_inner + groups * nstate + g * nstate
        b_f = buf_ref[xs0:xs0 + q, bcol:bcol + nstate]
        bg = b_f.astype(BF16)
        bg_t = b_f.T.astype(BF16)
        cg = buf_ref[xs0:xs0 + q, ccol:ccol + nstate].astype(BF16)
        cbm = lax.dot_general(cg, bg, (((1,), (1,)), ((), ())), preferred_element_type=F32)
        h_t = h_ref[g]
        yoff_ref[...] = jnp.dot(cg, h_t.astype(BF16), preferred_element_type=F32) * _expand(ecs_parts, eg)
        st = jnp.dot(bg_t, xw_ref[...], preferred_element_type=F32)
        h_ref[g] = _expand(dec_parts, eg)[0:1, :] * h_t + st
        for p in range(hpg // 2):
            pc = slice(p * 2 * hd, (p + 1) * 2 * hd)
            ac = slice(g * gw + p * 2 * hd, g * gw + (p + 1) * 2 * hd)
            x_pair = xdt_ref[:, pc]
            ys = []
            for e in range(2):
                h = g * hpg + 2 * p + e
                seg = cs[:, h:h + 1] - cs_t[h:h + 1, :]
                m = (cbm * jnp.exp(jnp.where(causal, seg, NEG_INF))).astype(BF16)
                ys.append(jnp.dot(m, x_pair, preferred_element_type=F32))
            y = jnp.where(lane < hd, ys[0], ys[1]) + yoff_ref[:, pc]
            y = y + dx_ref[:, ac] * buf_ref[xs0:xs0 + q, ac]
            zz = z_ref[:, ac]
            yg_ref[0:valid, pc] = y[0:valid] * (zz * jax.nn.sigmoid(zz))
        yg = yg_ref[0:valid, :]
        yn = yg * lax.rsqrt(jnp.mean(yg * yg, axis=-1, keepdims=True) + NORM_EPS) * nw_ref[:, gc]
        y_ref[:, gc] = yn.astype(y_ref.dtype)

    @pl.when(c == pl.num_programs(1) - 1)
    def _():
        for g in range(groups):
            for p in range(ppg):
                r0 = (g * ppg + p) * LANE
                hl_ref[r0:r0 + LANE, :] = h_ref[g, :, p * LANE:(p + 1) * LANE].T


def _ssd_mix(xbc, z, dt_raw, row0, nb, nchunk, valid, conv_buf, h0, conv_w, conv_b, dt_bias, a_log, d_skip, norm_w,
             groups, nstate, hd):
    q = SSD_CHUNK
    assert valid == q or (nchunk == 1 and valid % SUBLANE == 0 and valid < q)
    conv_dim = xbc.shape[1]
    d_inner = z.shape[1]
    heads = dt_raw.shape[1]
    gw = d_inner // groups
    assert row0 % valid == 0 and conv_dim == d_inner + 2 * groups * nstate and heads * hd == d_inner
    assert gw % LANE == 0 and 2 * hd == LANE and nstate == LANE and conv_dim % 512 == 0
    blk0 = row0 // valid
    sub = valid // SUBLANE
    nconv = conv_w.shape[0]
    conv8 = jnp.pad(conv_buf, ((0, 0), (SUBLANE - (nconv - 1), 0), (0, 0)))
    expand = (jnp.arange(heads)[:, None] == jnp.arange(d_inner)[None, :] // hd).astype(BF16)
    dx = jnp.repeat(d_skip, hd)[None]
    row_map = lambda b, c: (blk0 + b * nchunk + c, 0)
    full2 = lambda b, c: (0, 0)
    kern = functools.partial(_ssd_kernel, q=q, d_inner=d_inner, groups=groups, nstate=nstate, hd=hd, valid=valid)
    y, hl = pl.pallas_call(
        kern, grid=(nb, nchunk),
        in_specs=[pl.BlockSpec((valid, conv_dim), row_map),
                  pl.BlockSpec((SUBLANE, conv_dim), lambda b, c: (jnp.maximum((blk0 + b * nchunk + c) * sub - 1, 0), 0)),
                  pl.BlockSpec((None, SUBLANE, conv_dim), lambda b, c: (b, 0, 0)),
                  pl.BlockSpec((valid, d_inner), row_map),
                  pl.BlockSpec((valid, heads), row_map),
                  pl.BlockSpec((None, heads * hd, nstate), lambda b, c: (b, 0, 0), pipeline_mode=pl.Buffered(1)),
                  pl.BlockSpec(conv_w.shape, full2),
                  pl.BlockSpec((1, conv_dim), full2),
                  pl.BlockSpec((1, heads), full2),
                  pl.BlockSpec((1, heads), full2),
                  pl.BlockSpec((1, d_inner), full2),
                  pl.BlockSpec((1, d_inner), full2),
                  pl.BlockSpec((heads, d_inner), full2, pipeline_mode=pl.Buffered(1))],
        out_specs=[pl.BlockSpec((valid, d_inner), lambda b, c: (b * nchunk + c, 0)),
                   pl.BlockSpec((None, heads * hd, nstate), lambda b, c: (b, 0, 0))],
        out_shape=[jax.ShapeDtypeStruct((nb * nchunk * valid, d_inner), BF16 if valid == q else F32),
                   jax.ShapeDtypeStruct((nb, heads * hd, nstate), F32)],
        scratch_shapes=[pltpu.VMEM((q + SUBLANE, conv_dim), F32),
                        pltpu.VMEM((groups, nstate, gw), F32),
                        pltpu.VMEM((q, gw), BF16), pltpu.VMEM((q, gw), BF16),
                        pltpu.VMEM((q, gw), F32), pltpu.VMEM((q, gw), F32)],
        compiler_params=pltpu.CompilerParams(dimension_semantics=("parallel", "arbitrary"),
                                             vmem_limit_bytes=V7X_VMEM_LIMIT_BYTES),
        name="ssd_mix",
    )(xbc, xbc, conv8, z, dt_raw, h0.reshape(nb, heads * hd, nstate), conv_w, conv_b[None], dt_bias[None],
      a_log[None], dx, norm_w[None], expand)
    return y, hl.reshape(nb, heads, hd, nstate)


LRU_SCAN_LANES = 512


def _gelu_tanh(x):
    return 0.5 * x * (1.0 + jnp.tanh(math.sqrt(2.0 / math.pi) * (x + 0.044715 * (x * x * x))))


def _lru_kernel(gate_ref, xb_ref, prev_ref, conv8_ref, h0_ref, cw_ref, cb_ref, wr_ref, br_ref, wi_ref, bi_ref,
                lam_ref, y_ref, hl_ref, e_ref, a_ref, u_ref, h_ref, *, tm, width):
    t = pl.program_id(1)

    @pl.when(t == 0)
    def _():
        h_ref[...] = h0_ref[...]

    e_ref[0:SUBLANE, :] = jnp.where(t == 0, conv8_ref[...], prev_ref[...])
    e_ref[SUBLANE:SUBLANE + tm, :] = xb_ref[...]
    nconv = cw_ref.shape[0]
    first = SUBLANE - (nconv - 1)
    nblk = wr_ref.shape[0]
    bd = width // nblk
    for k in range(nblk):
        cols = slice(k * bd, (k + 1) * bd)
        s = e_ref[pl.ds(first, tm), cols] * cw_ref[0:1, cols]
        for c in range(1, nconv):
            s = s + e_ref[pl.ds(first + c, tm), cols] * cw_ref[c:c + 1, cols]
        xk = cb_ref[:, cols] + s
        xkb = xk.astype(BF16)
        r = jax.nn.sigmoid(jnp.dot(xkb, wr_ref[k], preferred_element_type=F32) + br_ref[:, cols])
        ig = jax.nn.sigmoid(jnp.dot(xkb, wi_ref[k], preferred_element_type=F32) + bi_ref[:, cols])
        log_a = (-LRU_C * r) * jax.nn.softplus(-lam_ref[:, cols])
        a_ref[:, cols] = jnp.exp(log_a)
        one_minus_a2 = -jnp.tanh(log_a) * (jnp.exp(2.0 * log_a) + 1.0)
        u_ref[:, cols] = jnp.sqrt(one_minus_a2) * ig * xk

    row = lax.broadcasted_iota(jnp.int32, (SUBLANE, LRU_SCAN_LANES), 0)
    for sl in range(width // LRU_SCAN_LANES):
        cols = slice(sl * LRU_SCAN_LANES, (sl + 1) * LRU_SCAN_LANES)

        def body(r, h):
            rows = pl.ds(pl.multiple_of(r * SUBLANE, SUBLANE), SUBLANE)
            a = a_ref[rows, cols]
            u = u_ref[rows, cols]
            for sh in (1, 2, 4):
                a_sh = jnp.where(row >= sh, pltpu.roll(a, sh, 0), 1.0)
                u_sh = jnp.where(row >= sh, pltpu.roll(u, sh, 0), 0.0)
                u = a * u_sh + u
                a = a * a_sh
            hs = a * h + u
            y_ref[rows, cols] = hs * _gelu_tanh(gate_ref[rows, cols])
            return hs[SUBLANE - 1:SUBLANE, :]

        h_ref[:, cols] = lax.fori_loop(0, tm // SUBLANE, body, h_ref[:, cols])
    hl_ref[...] = h_ref[...]


def _lru_mix(gx, row0, nb, seq, conv_buf, h0, conv_w, conv_b, w_r, b_r, w_i, b_i, lam):
    width = gx.shape[1] // 2
    tm = _row_tile(seq, 128) if seq % 16 == 0 else seq
    assert seq % tm == 0 and tm % SUBLANE == 0 and row0 % tm == 0 and width % LRU_SCAN_LANES == 0
    nt = seq // tm
    nconv = conv_w.shape[0]
    conv8 = jnp.pad(conv_buf, ((0, 0), (SUBLANE - (nconv - 1), 0), (0, 0)))
    blk0 = row0 // tm
    sub = tm // SUBLANE

    def row_map(b, t):
        return (blk0 + b * nt + t, 0)

    def xb_map(b, t):
        return (blk0 + b * nt + t, 1)

    def prev_map(b, t):
        return (jnp.maximum((blk0 + b * nt + t) * sub - 1, 0), 1)

    full2 = lambda b, t: (0, 0)
    full3 = lambda b, t: (0, 0, 0)
    y, hl = pl.pallas_call(
        functools.partial(_lru_kernel, tm=tm, width=width),
        grid=(nb, nt),
        in_specs=[pl.BlockSpec((tm, width), row_map),
                  pl.BlockSpec((tm, width), xb_map),
                  pl.BlockSpec((SUBLANE, width), prev_map),
                  pl.BlockSpec((None, SUBLANE, width), lambda b, t: (b, 0, 0)),
                  pl.BlockSpec((None, 1, width), lambda b, t: (b, 0, 0)),
                  pl.BlockSpec(conv_w.shape, full2),
                  pl.BlockSpec((1, width), full2),
                  pl.BlockSpec(w_r.shape, full3),
                  pl.BlockSpec((1, width), full2),
                  pl.BlockSpec(w_i.shape, full3),
                  pl.BlockSpec((1, width), full2),
                  pl.BlockSpec((1, width), full2)],
        out_specs=[pl.BlockSpec((tm, width), lambda b, t: (b * nt + t, 0)),
                   pl.BlockSpec((None, 1, width), lambda b, t: (b, 0, 0))],
        out_shape=[jax.ShapeDtypeStruct((nb * seq, width), F32),
                   jax.ShapeDtypeStruct((nb, 1, width), F32)],
        scratch_shapes=[pltpu.VMEM((tm + SUBLANE, width), F32),
                        pltpu.VMEM((tm, width), F32),
                        pltpu.VMEM((tm, width), F32),
                        pltpu.VMEM((1, width), F32)],
        compiler_params=pltpu.CompilerParams(dimension_semantics=("parallel", "arbitrary"),
                                             vmem_limit_bytes=V7X_VMEM_LIMIT_BYTES),
        name="lru_mix",
    )(gx, gx, gx, conv8, h0[:, None, :], conv_w, conv_b[None], w_r.astype(BF16), b_r.reshape(1, width),
      w_i.astype(BF16), b_i.reshape(1, width), lam[None])
    return y, hl[:, 0]


KEY_CHUNK = 128
TINY = 1e-30
ATT_CHUNKS = 2


def _bucket_of(dist):
    n = jnp.maximum(dist, 0)
    exact = REL_BUCKETS // 2
    log_ratio = jnp.log(jnp.maximum(n, 1).astype(F32) / exact) / math.log(REL_MAX_DIST / exact)
    large = jnp.minimum(exact + (log_ratio * (REL_BUCKETS - exact)).astype(jnp.int32), REL_BUCKETS - 1)
    return jnp.where(n < exact, n, large)


def _bias_from_buckets(bucket, tab_ref, h):
    out = jnp.full(bucket.shape, tab_ref[0, h], F32)
    for k in range(1, REL_BUCKETS):
        out = jnp.where(bucket == k, tab_ref[k, h], out)
    return out


def _head_rmsnorm(x, w_row):
    return x * lax.rsqrt(jnp.mean(x * x, axis=-1, keepdims=True) + NORM_EPS) * w_row


def _bias_tiles_kernel(tab_ref, o_ref, *, tq):
    i = lax.broadcasted_iota(jnp.int32, (tq, KEY_CHUNK), 0)
    j = lax.broadcasted_iota(jnp.int32, (tq, KEY_CHUNK), 1)
    buckets = [_bucket_of(i - j + KEY_CHUNK * k) for k in range(2)]
    nheads = o_ref.shape[1] // tq

    def body(h, carry):
        rows = pl.ds(pl.multiple_of(h * tq, SUBLANE), tq)
        for k in range(2):
            o_ref[k, rows, :] = _bias_from_buckets(buckets[k], tab_ref, h)
        o_ref[2, rows, :] = jnp.full((tq, KEY_CHUNK), tab_ref[REL_BUCKETS - 1, h], F32)
        return carry

    lax.fori_loop(0, nheads, body, 0)


def _bias_tiles(table, tq):
    nheads = table.shape[1]
    return pl.pallas_call(
        functools.partial(_bias_tiles_kernel, tq=tq),
        in_specs=[pl.BlockSpec(memory_space=pltpu.SMEM)],
        out_specs=pl.BlockSpec(memory_space=pltpu.VMEM),
        out_shape=jax.ShapeDtypeStruct((3, nheads * tq, KEY_CHUNK), F32),
        name="nsa_bias_tiles",
    )(table)


PAGES_PER_STEP = 8


def _pages_per_step(npages):
    p = PAGES_PER_STEP
    while npages % p:
        p //= 2
    return p


def _slot_spec(src, lead, slot, groups, d, k, pps):
    gd = groups * d
    if src.shape[-2:] == (groups, d):
        nlead = len(lead)
        assert src.ndim == nlead + 5 and src.shape[-4] == KEY_CHUNK
        return pl.BlockSpec((None,) * (nlead + 1) + (KEY_CHUNK, None, groups, d),
                            lambda b, p, t: tuple(lead) + (t[b, p * pps + k], 0, slot, 0, 0))
    assert not lead
    if src.ndim == 2:
        return pl.BlockSpec((KEY_CHUNK, gd), lambda b, p, t: (t[b, p * pps + k], slot))
    assert src.ndim == 3 and src.shape[1] == KEY_CHUNK
    return pl.BlockSpec((None, KEY_CHUNK, gd), lambda b, p, t: (t[b, p * pps + k], 0, slot))


def _slab(ref, g, d):
    return ref[:, g, :] if len(ref.shape) == 3 else ref[:, g * d:(g + 1) * d]


def _compress_ab_kernel(tbl_ref, *refs, groups, d, pps):
    x_refs, (pe_ref, w1_ref, o_ref, slab_ref) = refs[:2 * pps], refs[2 * pps:]
    half = CMP_LEN // 2
    nchunk = KEY_CHUNK // CMP_STRIDE
    for k in range(pps):
        for s in range(2):
            for g in range(groups):
                slab_ref[k, s * groups + g] = _slab(x_refs[2 * k + s], g, d)
    rows_g = pps * nchunk
    for s in range(2):
        acc = [jnp.zeros((groups * rows_g, d), F32) for _ in range(2)]
        for l in range(half):
            x = jnp.concatenate(
                [slab_ref[k, s * groups + g, pl.ds(l, nchunk, stride=CMP_STRIDE), :]
                 for g in range(groups) for k in range(pps)], axis=0)
            for part in range(2):
                ll = part * half + l
                xa = (x + pe_ref[s, ll:ll + 1, :]).astype(BF16)
                acc[part] = acc[part] + jnp.dot(xa, w1_ref[s, ll], preferred_element_type=F32)
        for part in range(2):
            for g in range(groups):
                o_ref[s, part, g] = acc[part][g * rows_g:(g + 1) * rows_g]


def _compress_ab(src, lead, tbl, pe, w1b, groups, d):
    nb, npages = tbl.shape
    pps = _pages_per_step(npages)
    nchunk = KEY_CHUNK // CMP_STRIDE
    assert CMP_LEN == 2 * CMP_STRIDE
    grid_spec = pltpu.PrefetchScalarGridSpec(
        num_scalar_prefetch=1, grid=(nb, npages // pps),
        in_specs=[_slot_spec(src, lead, s, groups, d, k, pps) for k in range(pps) for s in range(2)]
        + [pl.BlockSpec(pe.shape, lambda b, p, t: (0, 0, 0)),
           pl.BlockSpec(w1b.shape, lambda b, p, t: (0, 0, 0, 0))],
        out_specs=pl.BlockSpec((None, 2, 2, groups, pps * nchunk, d), lambda b, p, t: (b, 0, 0, 0, p, 0)),
        scratch_shapes=[pltpu.VMEM((pps, 2 * groups, KEY_CHUNK, d), F32)])
    return pl.pallas_call(
        functools.partial(_compress_ab_kernel, groups=groups, d=d, pps=pps),
        grid_spec=grid_spec,
        out_shape=jax.ShapeDtypeStruct((nb, 2, 2, groups, npages * nchunk, d), F32),
        compiler_params=pltpu.CompilerParams(dimension_semantics=("parallel", "arbitrary"),
                                             vmem_limit_bytes=V7X_VMEM_LIMIT_BYTES),
        name="nsa_compress_ab",
    )(tbl, *([src] * (2 * pps)), pe, w1b)


def _compress_finish_kernel(ab_ref, w2_ref, kw_ref, o_ref, *, groups, d):
    n = ab_ref.shape[3]
    for s in range(2):
        for g in range(groups):
            first = ab_ref[s, 0, g]
            second = pltpu.roll(ab_ref[s, 1, g], n - 1, 0)
            hid = jax.nn.silu(first + second).astype(BF16)
            out = jnp.dot(hid, w2_ref[s], preferred_element_type=F32)
            if s == 0:
                out = _head_rmsnorm(out, kw_ref[...])
            o_ref[s, :, g * d:(g + 1) * d] = out.astype(o_ref.dtype)


def _compress_finish(ab, w2b, k_norm_row, groups, d):
    nb, _, _, _, n, _ = ab.shape
    return pl.pallas_call(
        functools.partial(_compress_finish_kernel, groups=groups, d=d),
        grid=(nb,),
        in_specs=[pl.BlockSpec((None, 2, 2, groups, n, d), lambda b: (b, 0, 0, 0, 0, 0)),
                  pl.BlockSpec(w2b.shape, lambda b: (0, 0, 0)),
                  pl.BlockSpec((1, d), lambda b: (0, 0))],
        out_specs=pl.BlockSpec((None, 2, n, groups * d), lambda b: (b, 0, 0, 0)),
        out_shape=jax.ShapeDtypeStruct((nb, 2, n, groups * d), BF16),
        compiler_params=pltpu.CompilerParams(dimension_semantics=("parallel",)),
        name="nsa_compress_finish",
    )(ab, w2b, k_norm_row)


def _select_kernel(tab_ref, q_ref, kcv_ref, qw_ref, ocmp_ref, msel_ref, *, tq, groups, hpg, d, q_pos0, n_cmp, n_sel,
                   nselp, scale):
    qi = pl.program_id(1)
    pos0 = q_pos0 + qi * tq
    nck = kcv_ref.shape[1]
    t_c = lax.broadcasted_iota(jnp.int32, (tq, nck), 0)
    n_c = lax.broadcasted_iota(jnp.int32, (tq, nck), 1)
    dist = pos0 + t_c - (n_c * CMP_STRIDE + CMP_LEN - 1)
    cmask = (dist >= 0) & (n_c < n_cmp)
    cmaskf = cmask.astype(F32)
    bucket = _bucket_of(dist)
    n_r = lax.broadcasted_iota(jnp.int32, (nck, nselp), 0)
    j_r = lax.broadcasted_iota(jnp.int32, (nck, nselp), 1)
    cover = ((n_r * CMP_STRIDE < j_r * SEL_LEN + SEL_LEN) & (n_r * CMP_STRIDE + CMP_LEN > j_r * SEL_LEN)
             & (n_r < n_cmp) & (j_r < n_sel)).astype(BF16)
    j_s = lax.broadcasted_iota(jnp.int32, (tq, nselp), 1)
    pos = pos0 + lax.broadcasted_iota(jnp.int32, (tq, nselp), 0)
    cur = pos // SEL_LEN
    visible = (j_s * SEL_LEN <= pos) & (j_s < n_sel)
    forced = ((j_s == 0) | (j_s == cur) | (j_s == cur - 1)).astype(F32)
    for g in range(groups):
        kc = kcv_ref[0, :, g * d:(g + 1) * d]
        vc = kcv_ref[1, :, g * d:(g + 1) * d]
        p_grp = jnp.zeros((tq, nck), F32)
        for e in range(hpg):
            h = g * hpg + e
            cols = slice(h * d, (h + 1) * d)
            qn = _head_rmsnorm(q_ref[:, cols], qw_ref[...]).astype(BF16)
            logits = lax.dot_general(qn, kc, (((1,), (1,)), ((), ())), preferred_element_type=F32) * scale
            logits = jnp.where(cmask, logits + _bias_from_buckets(bucket, tab_ref, h), NEG_INF)
            m = jnp.max(logits, axis=-1, keepdims=True)
            p = jnp.exp(logits - m)
            p = p / jnp.sum(p, axis=-1, keepdims=True) * cmaskf
            ocmp_ref[:, cols] = jnp.dot(p.astype(BF16), vc, preferred_element_type=F32)
            p_grp = p_grp + p
        p_sel = jnp.dot(p_grp.astype(BF16), cover, preferred_element_type=F32)
        score = jnp.where(visible, p_sel + FORCE_BONUS * forced, NEG_INF)
        cnt = jnp.zeros((tq, nselp), jnp.int32)
        for jp in range(n_sel):
            col = score[:, jp:jp + 1]
            beats = (col > score) | ((col == score) & (j_s > jp))
            cnt = cnt + beats.astype(jnp.int32)
        msel_ref[:, g * nselp:(g + 1) * nselp] = ((cnt < SEL_TOPK) & visible).astype(F32)


def _nsa_select(table, qp, row0, kcv, q_norm_row, *, nb, seq, tq, q_pos0, n_cmp, n_sel, groups, hpg, d):
    nq = seq // tq
    n_q = groups * hpg * d
    nselp = -(-n_sel // LANE) * LANE
    nck = kcv.shape[2]
    assert row0 % tq == 0
    blk0 = row0 // tq
    kern = functools.partial(_select_kernel, tq=tq, groups=groups, hpg=hpg, d=d, q_pos0=q_pos0, n_cmp=n_cmp,
                             n_sel=n_sel, nselp=nselp, scale=d ** -0.5)
    return pl.pallas_call(
        kern, grid=(nb, nq),
        in_specs=[pl.BlockSpec(memory_space=pltpu.SMEM),
                  pl.BlockSpec((tq, n_q), lambda b, q: (blk0 + b * nq + q, 0)),
                  pl.BlockSpec((None, 2, nck, groups * d), lambda b, q: (b, 0, 0, 0)),
                  pl.BlockSpec((1, d), lambda b, q: (0, 0))],
        out_specs=[pl.BlockSpec((tq, n_q), lambda b, q: (b * nq + q, 0)),
                   pl.BlockSpec((tq, groups * nselp), lambda b, q: (b * nq + q, 0))],
        out_shape=[jax.ShapeDtypeStruct((nb * seq, n_q), F32),
                   jax.ShapeDtypeStruct((nb * seq, groups * nselp), F32)],
        compiler_params=pltpu.CompilerParams(dimension_semantics=("parallel", "parallel"),
                                             vmem_limit_bytes=V7X_VMEM_LIMIT_BYTES),
        name="nsa_cmp_select",
    )(table, qp, kcv, q_norm_row)


def _kv_pack_kernel(tbl_ref, *refs, groups, d, pps):
    x_refs, (kw_ref, k_ref, v_ref, slab_ref) = refs[:2 * pps], refs[2 * pps:]

    def dense(x_ref, i, g):
        if len(x_ref.shape) == 3:
            slab_ref[i * groups + g] = _slab(x_ref, g, d)
            return slab_ref[i * groups + g]
        return _slab(x_ref, g, d)

    for k in range(pps):
        rows = slice(k * KEY_CHUNK, (k + 1) * KEY_CHUNK)
        for g in range(groups):
            cols = slice(g * d, (g + 1) * d)
            k_ref[rows, cols] = _head_rmsnorm(dense(x_refs[2 * k], 0, g), kw_ref[...]).astype(k_ref.dtype)
            v_ref[rows, cols] = dense(x_refs[2 * k + 1], 1, g).astype(v_ref.dtype)


def _kv_pack(src, lead, tbl, k_slot, k_norm_row, groups, d):
    nb, npages = tbl.shape
    pps = _pages_per_step(npages)
    gd = groups * d
    grid_spec = pltpu.PrefetchScalarGridSpec(
        num_scalar_prefetch=1, grid=(nb, npages // pps),
        in_specs=[_slot_spec(src, lead, k_slot + s, groups, d, k, pps) for k in range(pps) for s in range(2)]
        + [pl.BlockSpec((1, d), lambda b, p, t: (0, 0))],
        out_specs=[pl.BlockSpec((None, pps * KEY_CHUNK, gd), lambda b, p, t: (b, p, 0)),
                   pl.BlockSpec((None, pps * KEY_CHUNK, gd), lambda b, p, t: (b, p, 0))],
        scratch_shapes=[pltpu.VMEM((2 * groups, KEY_CHUNK, d), F32)])
    return pl.pallas_call(
        functools.partial(_kv_pack_kernel, groups=groups, d=d, pps=pps),
        grid_spec=grid_spec,
        out_shape=[jax.ShapeDtypeStruct((nb, npages * KEY_CHUNK, gd), BF16)] * 2,
        compiler_params=pltpu.CompilerParams(dimension_semantics=("parallel", "parallel"),
                                             vmem_limit_bytes=V7X_VMEM_LIMIT_BYTES),
        name="nsa_kv_pack",
    )(tbl, *([src] * (2 * pps)), k_norm_row)


def _attend_kernel(tiles_ref, q_ref, ocmp_ref, gate_ref, msel_ref, ks_ref, vs_ref, kw_ref, vw_ref, qw_ref, o_ref,
                   qn_ref, ms_ref, ls_ref, as_ref, mw_ref, lw_ref, aw_ref, *, tq, groups, hpg, d, q_pos0,
                   win_chunk0, n_win_steps, nselp, scale):
    qi = pl.program_id(1)
    c = pl.program_id(2)
    pos0 = q_pos0 + qi * tq
    p_hi = (pos0 + tq - 1) // KEY_CHUNK // ATT_CHUNKS
    rows_g = hpg * tq
    nheads = groups * hpg

    @pl.when(c == 0)
    def _():
        for h in range(nheads):
            qn_ref[h * tq:(h + 1) * tq, :] = _head_rmsnorm(q_ref[:, h * d:(h + 1) * d], qw_ref[...]).astype(BF16)
        for m_ref, l_ref, a_ref in ((ms_ref, ls_ref, as_ref), (mw_ref, lw_ref, aw_ref)):
            m_ref[...] = jnp.full(m_ref.shape, NEG_INF, F32)
            l_ref[...] = jnp.zeros(l_ref.shape, F32)
            a_ref[...] = jnp.zeros(a_ref.shape, F32)

    assert tq & (tq - 1) == 0 and d == LANE
    t_idx = lax.broadcasted_iota(jnp.int32, (rows_g, KEY_CHUNK), 0) & (tq - 1)
    j_idx = lax.broadcasted_iota(jnp.int32, (rows_g, KEY_CHUNK), 1)
    ones_v = jnp.ones((KEY_CHUNK, d), BF16)

    def step(g, k_ref, v_ref, m_ref, l_ref, a_ref, chunk0, masks):
        rows = slice(g * rows_g, (g + 1) * rows_g)
        q = qn_ref[rows, :]
        logits = []
        for u, mask in enumerate(masks):
            delta = pos0 - (chunk0 + u) * KEY_CHUNK
            tile = jnp.clip(delta // KEY_CHUNK, 0, 2)
            k = k_ref[u * KEY_CHUNK:(u + 1) * KEY_CHUNK, g * d:(g + 1) * d]
            s = lax.dot_general(q, k, (((1,), (1,)), ((), ())), preferred_element_type=F32) * scale
            s = s + tiles_ref[tile, pl.ds(g * rows_g, rows_g), :]
            logits.append(jnp.where(mask, s, NEG_INF))
        m_prev = m_ref[rows, :]
        m_new = m_prev
        for s in logits:
            m_new = jnp.maximum(m_new, jnp.max(s, axis=-1, keepdims=True))
        alpha = jnp.exp(m_prev - m_new)
        pv = None
        for u, (s, mask) in enumerate(zip(logits, masks)):
            p = jnp.where(mask, jnp.exp(s - m_new), 0.0).astype(BF16)
            v_ext = jnp.concatenate([v_ref[u * KEY_CHUNK:(u + 1) * KEY_CHUNK, g * d:(g + 1) * d], ones_v], axis=1)
            part = jnp.dot(p, v_ext, preferred_element_type=F32)
            pv = part if pv is None else pv + part
        l_ref[rows, :] = alpha * l_ref[rows, :] + pv[:, d:]
        a_ref[rows, :] = alpha * a_ref[rows, :] + pv[:, :d]
        m_ref[rows, :] = m_new

    @pl.when(c <= p_hi)
    def _():
        chunk0 = c * ATT_CHUNKS
        blocks_per_chunk = KEY_CHUNK // SEL_LEN
        jr = lax.broadcasted_iota(jnp.int32, (nselp, KEY_CHUNK), 0)
        lr = lax.broadcasted_iota(jnp.int32, (nselp, KEY_CHUNK), 1)
        causal, expand = [], []
        for u in range(ATT_CHUNKS):
            causal.append(pos0 - (chunk0 + u) * KEY_CHUNK + t_idx - j_idx >= 0)
            expand.append((jr == (chunk0 + u) * blocks_per_chunk + lr // SEL_LEN).astype(BF16))
        for g in range(groups):
            ms = msel_ref[:, g * nselp:(g + 1) * nselp].astype(BF16)
            masks = []
            for u in range(ATT_CHUNKS):
                sel = jnp.dot(ms, expand[u], preferred_element_type=F32)
                masks.append(causal[u] & (jnp.concatenate([sel] * hpg, axis=0) > 0.5))
            step(g, ks_ref, vs_ref, ms_ref, ls_ref, as_ref, chunk0, masks)

    wpair = p_hi - c

    @pl.when((c < n_win_steps) & (wpair * ATT_CHUNKS >= win_chunk0))
    def _():
        chunk0 = wpair * ATT_CHUNKS
        masks = []
        for u in range(ATT_CHUNKS):
            dist = pos0 - (chunk0 + u) * KEY_CHUNK + t_idx - j_idx
            masks.append((dist >= 0) & (dist < WINDOW))
        for g in range(groups):
            step(g, kw_ref, vw_ref, mw_ref, lw_ref, aw_ref, chunk0, masks)

    @pl.when(c == pl.num_programs(2) - 1)
    def _():
        gates = jax.nn.sigmoid(gate_ref[...])
        for h in range(nheads):
            r = slice(h * tq, (h + 1) * tq)
            cols = slice(h * d, (h + 1) * d)
            o_sel = as_ref[r, :] / jnp.maximum(ls_ref[r, :], TINY)
            o_win = aw_ref[r, :] / jnp.maximum(lw_ref[r, :], TINY)
            o = (gates[:, 3 * h:3 * h + 1] * ocmp_ref[:, cols] + gates[:, 3 * h + 1:3 * h + 2] * o_sel
                 + gates[:, 3 * h + 2:3 * h + 3] * o_win)
            o_ref[:, cols] = o.astype(o_ref.dtype)


def _nsa_attend(tiles, qp, gp, row0, ocmp, msel, ksel, vsel, kwin, vwin, q_norm_row, *, nb, seq, tq, q_pos0,
                win_chunk0, groups, hpg, d):
    nq = seq // tq
    n_q = groups * hpg * d
    gd = groups * d
    nheads = groups * hpg
    nselp = msel.shape[1] // groups
    pair = ATT_CHUNKS * KEY_CHUNK
    assert ksel.shape[1] % pair == 0 and kwin.shape[1] % pair == 0 and win_chunk0 % ATT_CHUNKS == 0
    npair = ksel.shape[1] // pair
    nwpair = kwin.shape[1] // pair
    wpair0 = win_chunk0 // ATT_CHUNKS
    n_win_steps = (WINDOW // KEY_CHUNK + ATT_CHUNKS - 1) // ATT_CHUNKS + 1
    ncg = max(npair, n_win_steps)
    assert row0 % tq == 0 and q_pos0 % KEY_CHUNK == 0 and (tq == KEY_CHUNK or nq == 1) and tq <= KEY_CHUNK
    blk0 = row0 // tq

    def p_hi(q):
        return (q_pos0 + q * tq + tq - 1) // KEY_CHUNK // ATT_CHUNKS

    def sel_map(b, q, c):
        return (b, jnp.minimum(c, p_hi(q)), 0)

    def win_map(b, q, c):
        return (b, jnp.clip(p_hi(q) - jnp.minimum(c, n_win_steps - 1) - wpair0, 0, nwpair - 1), 0)

    grp_map = lambda b, q, c: (b * nq + q, 0)
    all_map = lambda b, q, c: (blk0 + b * nq + q, 0)
    kern = functools.partial(_attend_kernel, tq=tq, groups=groups, hpg=hpg, d=d, q_pos0=q_pos0,
                             win_chunk0=win_chunk0, n_win_steps=n_win_steps, nselp=nselp, scale=d ** -0.5)
    return pl.pallas_call(
        kern, grid=(nb, nq, ncg),
        in_specs=[pl.BlockSpec(tiles.shape, lambda b, q, c: (0, 0, 0)),
                  pl.BlockSpec((tq, n_q), all_map),
                  pl.BlockSpec((tq, n_q), grp_map),
                  pl.BlockSpec((tq, gp.shape[1]), all_map),
                  pl.BlockSpec((tq, groups * nselp), grp_map),
                  pl.BlockSpec((None, pair, gd), sel_map),
                  pl.BlockSpec((None, pair, gd), sel_map),
                  pl.BlockSpec((None, pair, gd), win_map),
                  pl.BlockSpec((None, pair, gd), win_map),
                  pl.BlockSpec((1, d), lambda b, q, c: (0, 0))],
        out_specs=pl.BlockSpec((tq, n_q), grp_map),
        out_shape=jax.ShapeDtypeStruct((nb * seq, n_q), BF16 if tq % (2 * SUBLANE) == 0 else F32),
        scratch_shapes=[pltpu.VMEM((nheads * tq, d), BF16),
                        pltpu.VMEM((nheads * tq, d), F32), pltpu.VMEM((nheads * tq, d), F32),
                        pltpu.VMEM((nheads * tq, d), F32),
                        pltpu.VMEM((nheads * tq, d), F32), pltpu.VMEM((nheads * tq, d), F32),
                        pltpu.VMEM((nheads * tq, d), F32)],
        compiler_params=pltpu.CompilerParams(dimension_semantics=("parallel", "parallel", "arbitrary"),
                                             vmem_limit_bytes=V7X_VMEM_LIMIT_BYTES),
        name="nsa_attend",
    )(tiles, qp, ocmp, gp, msel, ksel, vsel, kwin, vwin, q_norm_row)


def _nsa_group(qp, kvp, gp, row0, nb, seq, q_pos0, past, win_past, w, groups, hpg, d):
    q_norm_w, k_norm_w, cmp_pe, cmp_w1, cmp_w2, table = w
    gd = groups * d
    tq = KEY_CHUNK if seq % KEY_CHUNK == 0 else seq
    assert tq % SUBLANE == 0
    w1b, w2b = cmp_w1.astype(BF16), cmp_w2.astype(BF16)
    if past is None:
        assert seq % KEY_CHUNK == 0 and row0 % KEY_CHUNK == 0
        npg = seq // KEY_CHUNK
        tbl = row0 // KEY_CHUNK + jnp.arange(nb * npg, dtype=jnp.int32).reshape(nb, npg)
        src = kvp
        ab = _compress_ab(src, (), tbl, cmp_pe, w1b, groups, d)
        ksel, vsel = _kv_pack(src, (), tbl, 2, k_norm_w[1:2], groups, d)
        kwin, vwin = _kv_pack(src, (), tbl, 4, k_norm_w[2:3], groups, d)
        tk = seq
        win_chunk0 = 0
    else:
        cache, lead, tbl = past
        npg = tbl.shape[1]
        past_len = npg * KEY_CHUNK
        assert q_pos0 == past_len and seq < CMP_STRIDE and seq <= KEY_CHUNK
        ab = _compress_ab(cache, lead, tbl, cmp_pe, w1b, groups, d)
        ksel_p, vsel_p = _kv_pack(cache, lead, tbl, 2, k_norm_w[1:2], groups, d)
        wb = win_past.shape[1]
        assert wb % KEY_CHUNK == 0 and wb <= WINDOW and (past_len - wb) % KEY_CHUNK == 0
        nwp = wb // KEY_CHUNK
        wtbl = jnp.arange(nb * nwp, dtype=jnp.int32).reshape(nb, nwp)
        win_pages = win_past.reshape((nb * nwp, KEY_CHUNK) + win_past.shape[2:])
        kwin_p, vwin_p = _kv_pack(win_pages, (), wtbl, 0, k_norm_w[2:3], groups, d)
        new = kvp[row0:row0 + nb * seq].reshape(nb, seq, kvp.shape[1])
        new = jnp.pad(new, ((0, 0), (0, KEY_CHUNK - seq), (0, 0)))
        ntbl = jnp.arange(nb, dtype=jnp.int32).reshape(nb, 1)
        ksel_n, vsel_n = _kv_pack(new, (), ntbl, 2, k_norm_w[1:2], groups, d)
        kwin_n, vwin_n = _kv_pack(new, (), ntbl, 4, k_norm_w[2:3], groups, d)

        def join(past_part, new_part):
            rows = past_part.shape[1] + new_part.shape[1]
            fill = jnp.zeros((nb, -rows % (ATT_CHUNKS * KEY_CHUNK), gd), past_part.dtype)
            return jnp.concatenate([past_part, new_part, fill], axis=1)

        ksel, vsel = join(ksel_p, ksel_n), join(vsel_p, vsel_n)
        kwin, vwin = join(kwin_p, kwin_n), join(vwin_p, vwin_n)
        tk = past_len + seq
        win_chunk0 = (past_len - wb) // KEY_CHUNK
    n_cmp = (tk - CMP_LEN) // CMP_STRIDE + 1
    n_sel = -(-tk // SEL_LEN)
    assert n_cmp <= ab.shape[4] - 1
    kcv = _compress_finish(ab, w2b, k_norm_w[0:1], groups, d)
    ocmp, msel = _nsa_select(table, qp, row0, kcv, q_norm_w[None], nb=nb, seq=seq, tq=tq, q_pos0=q_pos0,
                             n_cmp=n_cmp, n_sel=n_sel, groups=groups, hpg=hpg, d=d)
    tiles = _bias_tiles(table, tq)
    return _nsa_attend(tiles, qp, gp, row0, ocmp, msel, ksel, vsel, kwin, vwin, q_norm_w[None], nb=nb, seq=seq,
                       tq=tq, q_pos0=q_pos0, win_chunk0=win_chunk0, groups=groups, hpg=hpg, d=d)


def kernel(x_prompt, x_sample, state_ssd, state_ssd_conv, state_lru, state_lru_conv, cache_nsa_kv, cache_nsa_win, page_table, ffn_norm, w_ffn_in, w_ffn_out, mix_norm, ssd_w_in, ssd_conv_w, ssd_conv_b, ssd_dt_bias, ssd_a_log, ssd_d, ssd_norm, ssd_w_out, lru_w_in, lru_conv_w, lru_conv_b, lru_w_r, lru_b_r, lru_w_i, lru_b_i, lru_lambda, lru_w_out, nsa_w_in, nsa_q_norm, nsa_k_norm, nsa_cmp_pe, nsa_cmp_w1, nsa_cmp_w2, rel_bias_table, nsa_w_out):
    prm = dict(ffn_norm=ffn_norm, w_ffn_in=w_ffn_in, w_ffn_out=w_ffn_out)
    bp, sp, dm = x_prompt.shape
    bs, ss, _ = x_sample.shape
    mp, ms = bp * sp, bs * ss
    depth = mix_norm.shape[0]
    G = NSA_GROUPS
    E = NSA_HEADS // G
    d_head = dm // NSA_HEADS
    n_pages = page_table.shape[1]
    page = cache_nsa_kv.shape[2]
    past_len = n_pages * page
    assert page == KEY_CHUNK

    def merge(tp, ts):
        return jnp.concatenate([tp.reshape(mp, -1), ts.reshape(ms, -1)], axis=0)

    def tail(buf, rows, prompt, c0, c1):
        n = buf.shape[1]
        nb_, seq, r0 = (bp, sp, 0) if prompt else (bs, ss, mp)
        k = min(n, seq)
        new = jnp.stack([rows[r0 + (b + 1) * seq - k:r0 + (b + 1) * seq, c0:c1] for b in range(nb_)])
        return jnp.concatenate([buf, new.reshape((nb_, k) + buf.shape[2:])], axis=1)[:, -n:]

    x = merge(x_prompt, x_sample)
    outs = {k: [] for k in ('ssd_h_p', 'ssd_h_s', 'ssd_buf_p', 'ssd_buf_s', 'lru_h_p', 'lru_h_s',
                            'lru_buf_p', 'lru_buf_s', 'rows_p', 'rows_s', 'win_p', 'win_s')}
    for i in range(depth):
        kind, j = i % N_MIXERS, i // N_MIXERS
        x = _ffn(x, prm, i, 0)
        hn = _rmsnorm(x, mix_norm[i][None])
        if kind == 0:
            d_inner = ssd_w_out.shape[1]
            conv_dim = ssd_conv_w.shape[2]
            heads = ssd_dt_bias.shape[1]
            z = _matmul(hn, ssd_w_in, (j,), kdim=dm, n0=0, ndim=d_inner, tn=512, name="ssd_in_z")
            xbc = _matmul(hn, ssd_w_in, (j,), kdim=dm, n0=d_inner, ndim=conv_dim, tn=512, name="ssd_in_xbc")
            dt_raw = _matmul(hn, ssd_w_in, (j,), kdim=dm, n0=d_inner + conv_dim, ndim=heads, tn=heads,
                             name="ssd_in_dt")
            w = (ssd_conv_w[j], ssd_conv_b[j], ssd_dt_bias[j], ssd_a_log[j], ssd_d[j], ssd_norm[j])
            q = SSD_CHUNK
            assert sp % q == 0
            nbuf = ssd_conv_w.shape[1] - 1
            hd, nstate = d_inner // heads, state_ssd.shape[-1]
            shape = (SSD_GROUPS, nstate, hd)
            zero_buf = jnp.zeros((bp, nbuf, conv_dim), F32)
            zero_h = jnp.zeros((bp,) + state_ssd.shape[2:], F32)
            y_p, st_p = _ssd_mix(xbc, z, dt_raw, 0, bp, sp // q, q, zero_buf, zero_h, *w, *shape)
            y_s, st_s = _ssd_mix(xbc, z, dt_raw, mp, bs, 1, ss, state_ssd_conv[j], state_ssd[j], *w, *shape)
            outs['ssd_h_p'].append(st_p); outs['ssd_h_s'].append(st_s)
            outs['ssd_buf_p'].append(tail(zero_buf, xbc, True, 0, conv_dim))
            outs['ssd_buf_s'].append(tail(state_ssd_conv[j], xbc, False, 0, conv_dim))
            x = _out_proj(jnp.concatenate([y_p, y_s.astype(BF16)], axis=0), ssd_w_out, (j,), x, 1.0, "ssd_out")
        elif kind == 1:
            width = lru_w_out.shape[1]
            nbuf = lru_conv_w.shape[1] - 1
            gx = _matmul(hn, lru_w_in, (j,), kdim=dm, n0=0, ndim=2 * width, tn=512, name="lru_in")
            w = (lru_conv_w[j], lru_conv_b[j], lru_w_r[j], lru_b_r[j], lru_w_i[j], lru_b_i[j], lru_lambda[j])
            zero_buf = jnp.zeros((bp, nbuf, width), F32)
            y_p, st_p = _lru_mix(gx, 0, bp, sp, zero_buf, jnp.zeros((bp, width), F32), *w)
            y_s, st_s = _lru_mix(gx, mp, bs, ss, state_lru_conv[j], state_lru[j], *w)
            outs['lru_h_p'].append(st_p); outs['lru_h_s'].append(st_s)
            outs['lru_buf_p'].append(tail(zero_buf, gx, True, width, 2 * width))
            outs['lru_buf_s'].append(tail(state_lru_conv[j], gx, False, width, 2 * width))
            x = _out_proj(jnp.concatenate([y_p, y_s], axis=0).astype(BF16), lru_w_out, (j,), x, 1.0, "lru_out")
        else:
            n_q, n_rows, n_kv = NSA_HEADS * d_head, 4 * G * d_head, 6 * G * d_head
            qp = _matmul(hn, nsa_w_in, (j,), kdim=dm, n0=0, ndim=n_q, tn=512, name="nsa_in_q")
            kvp = _matmul(hn, nsa_w_in, (j,), kdim=dm, n0=n_q, ndim=n_kv, tn=512, name="nsa_in_kv")
            n_gate = nsa_w_in.shape[2] - n_q - n_kv
            w_gate = jnp.pad(nsa_w_in[j][None, :, n_q + n_kv:], ((0, 0), (0, 0), (0, LANE - n_gate)))
            gp = _matmul(hn, w_gate, (0,), kdim=dm, n0=0, ndim=LANE, tn=LANE, name="nsa_in_gate")[:, :n_gate]
            w = (nsa_q_norm[j], nsa_k_norm[j], nsa_cmp_pe[j], nsa_cmp_w1[j], nsa_cmp_w2[j], rel_bias_table)
            o_p = _nsa_group(qp, kvp, gp, 0, bp, sp, 0, None, None, w, G, E, d_head)
            o_s = _nsa_group(qp, kvp, gp, mp, bs, ss, past_len, (cache_nsa_kv, (j,), page_table), cache_nsa_win[j],
                             w, G, E, d_head)
            outs['rows_p'].append(kvp[:mp, :n_rows].reshape(bp, sp, 4, G, d_head))
            outs['rows_s'].append(kvp[mp:, :n_rows].reshape(bs, ss, 4, G, d_head))
            outs['win_p'].append(tail(jnp.zeros((bp,) + cache_nsa_win.shape[2:], F32), kvp, True, n_rows, n_kv))
            outs['win_s'].append(tail(cache_nsa_win[j], kvp, False, n_rows, n_kv))
            x = _out_proj(jnp.concatenate([o_p.astype(BF16), o_s.astype(BF16)], axis=0), nsa_w_out, (j,), x, 1.0,
                          "nsa_out")
        x = _ffn(x, prm, i, 1)
    y_prompt, y_sample = x[:mp].reshape(bp, sp, dm), x[mp:].reshape(bs, ss, dm)
    st = {k: jnp.stack(v) for k, v in outs.items()}
    return (y_prompt, y_sample, st['ssd_h_p'], st['ssd_h_s'], st['ssd_buf_p'], st['ssd_buf_s'],
            st['lru_h_p'], st['lru_h_s'], st['lru_buf_p'], st['lru_buf_s'],
            st['rows_p'], st['rows_s'], st['win_p'], st['win_s'])
```

```python
import functools
import math

import jax
import jax.numpy as jnp
from jax import lax
from jax.experimental import pallas as pl
from jax.experimental.pallas import tpu as pltpu

F32 = jnp.float32
BF16 = jnp.bfloat16

NORM_EPS = 1e-6
N_MIXERS = 3
SSD_HEAD_DIM = 64
SSD_STATE = 128
SSD_GROUPS = 8
SSD_CHUNK = 128
LRU_BLOCKS = 16
LRU_C = 8.0
NSA_HEADS = 32
NSA_GROUPS = 4
CMP_LEN = 32
CMP_STRIDE = 16
SEL_LEN = 64
SEL_TOPK = 16
WINDOW = 512
SEL_QBLOCK = 64
WIN_QBLOCK = 128
FORCE_BONUS = 1e4
NEG_INF = -1e30
REL_BUCKETS = 32
REL_MAX_DIST = 128

V7X_VMEM_LIMIT_BYTES = 56 * 1024 * 1024
LANE = 128
SUBLANE = 8


def _row_tile(m, cap):
    best = None
    for t in range(16, min(m, cap) + 1, 16):
        if m % t == 0:
            best = t
    assert best is not None, m
    return best


def _rmsnorm_kernel(x_ref, w_ref, o_ref):
    x = x_ref[...]
    y = x * lax.rsqrt(jnp.mean(x * x, axis=-1, keepdims=True) + NORM_EPS)
    o_ref[...] = (y * w_ref[...]).astype(o_ref.dtype)


def _rmsnorm(x, w_row):
    m, d = x.shape
    tm = _row_tile(m, 704)
    return pl.pallas_call(
        _rmsnorm_kernel,
        grid=(m // tm,),
        in_specs=[pl.BlockSpec((tm, d), lambda i: (i, 0)),
                  pl.BlockSpec((1, d), lambda i: (0, 0))],
        out_specs=pl.BlockSpec((tm, d), lambda i: (i, 0)),
        out_shape=jax.ShapeDtypeStruct((m, d), BF16),
        compiler_params=pltpu.CompilerParams(dimension_semantics=("parallel",),
                                             vmem_limit_bytes=V7X_VMEM_LIMIT_BYTES),
        name="rmsnorm",
    )(x, w_row)


def _mm_kernel(a_ref, w_ref, o_ref):
    acc = jnp.dot(a_ref[...], w_ref[...].astype(BF16), preferred_element_type=F32)
    o_ref[...] = acc.astype(o_ref.dtype)


def _mm_res_kernel(a_ref, w_ref, r_ref, o_ref, *, scale):
    acc = jnp.dot(a_ref[...], w_ref[...].astype(BF16), preferred_element_type=F32)
    o_ref[...] = r_ref[...] + scale * acc


def _mm_swiglu_kernel(a_ref, wg_ref, wu_ref, o_ref):
    a = a_ref[...]
    g = jnp.dot(a, wg_ref[...].astype(BF16), preferred_element_type=F32)
    u = jnp.dot(a, wu_ref[...].astype(BF16), preferred_element_type=F32)
    o_ref[...] = (jax.nn.silu(g) * u).astype(o_ref.dtype)


def _w_spec(w, lead, kdim, tn, k_blk, n_blk0):
    nlead = len(lead)
    return pl.BlockSpec((None,) * nlead + (kdim, tn),
                        lambda i, j: tuple(lead) + (k_blk, n_blk0 + j))


def _matmul(a, w, lead, *, a_k0=0, w_k0=0, kdim, n0, ndim, tn, tm_cap=1408,
            res=None, scale=1.0, out_dtype=F32, name="matmul"):
    m = a.shape[0]
    tm = _row_tile(m, tm_cap)
    assert a_k0 % kdim == 0 and w_k0 % kdim == 0 and n0 % tn == 0 and ndim % tn == 0
    a_spec = pl.BlockSpec((tm, kdim), lambda i, j: (i, a_k0 // kdim))
    w_spec = _w_spec(w, lead, kdim, tn, w_k0 // kdim, n0 // tn)
    o_spec = pl.BlockSpec((tm, tn), lambda i, j: (i, j))
    grid = (m // tm, ndim // tn)
    params = pltpu.CompilerParams(dimension_semantics=("parallel", "parallel"),
                                  vmem_limit_bytes=V7X_VMEM_LIMIT_BYTES)
    if res is None:
        return pl.pallas_call(
            _mm_kernel, grid=grid, in_specs=[a_spec, w_spec], out_specs=o_spec,
            out_shape=jax.ShapeDtypeStruct((m, ndim), out_dtype),
            compiler_params=params, name=name)(a, w)
    return pl.pallas_call(
        functools.partial(_mm_res_kernel, scale=scale), grid=grid,
        in_specs=[a_spec, w_spec, o_spec], out_specs=o_spec,
        out_shape=jax.ShapeDtypeStruct((m, ndim), F32),
        compiler_params=params, name=name)(a, w, res)


def _matmul_swiglu(a, w, lead, *, tn=256, tm_cap=2064):
    m, k = a.shape
    f = w.shape[-1] // 2
    tm = _row_tile(m, tm_cap)
    assert f % tn == 0
    a_spec = pl.BlockSpec((tm, k), lambda i, j: (i, 0))
    wg_spec = _w_spec(w, lead, k, tn, 0, 0)
    wu_spec = _w_spec(w, lead, k, tn, 0, f // tn)
    return pl.pallas_call(
        _mm_swiglu_kernel, grid=(m // tm, f // tn),
        in_specs=[a_spec, wg_spec, wu_spec],
        out_specs=pl.BlockSpec((tm, tn), lambda i, j: (i, j)),
        out_shape=jax.ShapeDtypeStruct((m, f), BF16),
        compiler_params=pltpu.CompilerParams(dimension_semantics=("parallel", "parallel"),
                                             vmem_limit_bytes=V7X_VMEM_LIMIT_BYTES),
        name="ffn_in_swiglu")(a, w, w)


def _out_proj(a, w, lead, res, scale, name):
    k = a.shape[1]
    n = w.shape[-1]
    nsplit = 1
    while (k // nsplit) > 5632 or k % nsplit:
        nsplit += 1
    kdim = k // nsplit
    assert kdim % LANE == 0
    out = res
    for s in range(nsplit):
        out = _matmul(a, w, lead, a_k0=s * kdim, w_k0=s * kdim, kdim=kdim, n0=0, ndim=n,
                      tn=512 if kdim <= 4096 else 256, res=out, scale=scale, name=name)
    return out


def _ffn(x, prm, i, which):
    xn = _rmsnorm(x, prm['ffn_norm'][i, which][None])
    h = _matmul_swiglu(xn, prm['w_ffn_in'], (i, which))
    return _out_proj(h, prm['w_ffn_out'], (i, which), x, 0.5, "ffn_out")


def _split3(v):
    hi = v.astype(BF16)
    r1 = v - hi.astype(F32)
    mid = r1.astype(BF16)
    lo = (r1 - mid.astype(F32)).astype(BF16)
    return hi, mid, lo


def _expand(parts, e):
    out = jnp.dot(parts[0], e, preferred_element_type=F32)
    for p in parts[1:]:
        out = out + jnp.dot(p, e, preferred_element_type=F32)
    return out


def _ssd_kernel(xbc_ref, prev_ref, conv8_ref, z_ref, dtr_ref, h0_ref, cw_ref, cb_ref, dtb_ref, alog_ref, dx_ref,
                nw_ref, ex_ref, y_ref, hl_ref, buf_ref, h_ref, xdt_ref, xw_ref, yoff_ref, yg_ref, *, q, d_inner,
                groups, nstate, hd, valid):
    c = pl.program_id(1)
    gw = d_inner // groups
    hpg = gw // hd
    conv_dim = buf_ref.shape[1]

    ppg = gw // LANE

    @pl.when(c == 0)
    def _():
        for g in range(groups):
            for p in range(ppg):
                r0 = (g * ppg + p) * LANE
                h_ref[g, :, p * LANE:(p + 1) * LANE] = h0_ref[r0:r0 + LANE, :].T

    buf_ref[0:SUBLANE, :] = jnp.where(c == 0, conv8_ref[...], prev_ref[...])
    buf_ref[SUBLANE:SUBLANE + valid, :] = xbc_ref[...]
    if valid < q:
        buf_ref[SUBLANE + valid:SUBLANE + q, :] = jnp.zeros((q - valid, conv_dim), F32)
    nconv = cw_ref.shape[0]
    first = SUBLANE - (nconv - 1)
    cblk = 512
    for cbk in range(conv_dim // cblk):
        cols = slice(cbk * cblk, (cbk + 1) * cblk)
        s = buf_ref[pl.ds(first, q), cols] * cw_ref[0:1, cols]
        for k in range(1, nconv):
            s = s + buf_ref[pl.ds(first + k, q), cols] * cw_ref[k:k + 1, cols]
        buf_ref[SUBLANE:SUBLANE + q, cols] = jax.nn.silu(cb_ref[:, cols] + s)

    heads = dtr_ref.shape[1]
    row = lax.broadcasted_iota(jnp.int32, (q, heads), 0)
    dt = jax.nn.softplus(dtr_ref[...] + dtb_ref[...])
    if valid < q:
        dt = jnp.concatenate([dt, jnp.zeros((q - valid, heads), F32)], axis=0)
    cs = dt * (-jnp.exp(alog_ref[...]))
    sh = 1
    while sh < q:
        cs = cs + jnp.where(row >= sh, pltpu.roll(cs, sh, 0), 0.0)
        sh *= 2
    cs_t = cs.T
    cs_last = cs[q - 1:q, :]
    to_end = jnp.exp(cs_last - cs)
    dt_parts = _split3(dt)
    dtw_parts = _split3(dt * to_end)
    ecs_parts = _split3(jnp.exp(cs))
    dec_parts = _split3(jnp.broadcast_to(jnp.exp(cs_last), (SUBLANE, heads)))
    li = lax.broadcasted_iota(jnp.int32, (q, q), 0)
    si = lax.broadcasted_iota(jnp.int32, (q, q), 1)
    causal = li >= si
    lane = lax.broadcasted_iota(jnp.int32, (q, 2 * hd), 1)
    xs0 = SUBLANE
    for g in range(groups):
        gc = slice(g * gw, (g + 1) * gw)
        eg = ex_ref[:, gc]
        xs = buf_ref[xs0:xs0 + q, gc]
        xdt_ref[...] = (xs * _expand(dt_parts, eg)).astype(BF16)
        xw_ref[...] = (xs * _expand(dtw_parts, eg)).astype(BF16)
        bcol = d_inner + g * nstate
        ccol = d_inner + groups * nstate + g * nstate
        b_f = buf_ref[xs0:xs0 + q, bcol:bcol + nstate]
        bg = b_f.astype(BF16)
        bg_t = b_f.T.astype(BF16)
        cg = buf_ref[xs0:xs0 + q, ccol:ccol + nstate].astype(BF16)
        cbm = lax.dot_general(cg, bg, (((1,), (1,)), ((), ())), preferred_element_type=F32)
        h_t = h_ref[g]
        yoff_ref[...] = jnp.dot(cg, h_t.astype(BF16), preferred_element_type=F32) * _expand(ecs_parts, eg)
        st = jnp.dot(bg_t, xw_ref[...], preferred_element_type=F32)
        h_ref[g] = _expand(dec_parts, eg)[0:1, :] * h_t + st
        for p in range(hpg // 2):
            pc = slice(p * 2 * hd, (p + 1) * 2 * hd)
            ac = slice(g * gw + p * 2 * hd, g * gw + (p + 1) * 2 * hd)
            x_pair = xdt_ref[:, pc]
            ys = []
            for e in range(2):
                h = g * hpg + 2 * p + e
                seg = cs[:, h:h + 1] - cs_t[h:h + 1, :]
                m = (cbm * jnp.exp(jnp.where(causal, seg, NEG_INF))).astype(BF16)
                ys.append(jnp.dot(m, x_pair, preferred_element_type=F32))
            y = jnp.where(lane < hd, ys[0], ys[1]) + yoff_ref[:, pc]
            y = y + dx_ref[:, ac] * buf_ref[xs0:xs0 + q, ac]
            zz = z_ref[:, ac]
            yg_ref[0:valid, pc] = y[0:valid] * (zz * jax.nn.sigmoid(zz))
        yg = yg_ref[0:valid, :]
        yn = yg * lax.rsqrt(jnp.mean(yg * yg, axis=-1, keepdims=True) + NORM_EPS) * nw_ref[:, gc]
        y_ref[:, gc] = yn.astype(y_ref.dtype)

    @pl.when(c == pl.num_programs(1) - 1)
    def _():
        for g in range(groups):
            for p in range(ppg):
                r0 = (g * ppg + p) * LANE
                hl_ref[r0:r0 + LANE, :] = h_ref[g, :, p * LANE:(p + 1) * LANE].T


def _ssd_mix(xbc, z, dt_raw, row0, nb, nchunk, valid, conv_buf, h0, conv_w, conv_b, dt_bias, a_log, d_skip, norm_w,
             groups, nstate, hd):
    q = SSD_CHUNK
    assert valid == q or (nchunk == 1 and valid % SUBLANE == 0 and valid < q)
    conv_dim = xbc.shape[1]
    d_inner = z.shape[1]
    heads = dt_raw.shape[1]
    gw = d_inner // groups
    assert row0 % valid == 0 and conv_dim == d_inner + 2 * groups * nstate and heads * hd == d_inner
    assert gw % LANE == 0 and 2 * hd == LANE and nstate == LANE and conv_dim % 512 == 0
    blk0 = row0 // valid
    sub = valid // SUBLANE
    nconv = conv_w.shape[0]
    conv8 = jnp.pad(conv_buf, ((0, 0), (SUBLANE - (nconv - 1), 0), (0, 0)))
    expand = (jnp.arange(heads)[:, None] == jnp.arange(d_inner)[None, :] // hd).astype(BF16)
    dx = jnp.repeat(d_skip, hd)[None]
    row_map = lambda b, c: (blk0 + b * nchunk + c, 0)
    full2 = lambda b, c: (0, 0)
    kern = functools.partial(_ssd_kernel, q=q, d_inner=d_inner, groups=groups, nstate=nstate, hd=hd, valid=valid)
    y, hl = pl.pallas_call(
        kern, grid=(nb, nchunk),
        in_specs=[pl.BlockSpec((valid, conv_dim), row_map),
                  pl.BlockSpec((SUBLANE, conv_dim), lambda b, c: (jnp.maximum((blk0 + b * nchunk + c) * sub - 1, 0), 0)),
                  pl.BlockSpec((None, SUBLANE, conv_dim), lambda b, c: (b, 0, 0)),
                  pl.BlockSpec((valid, d_inner), row_map),
                  pl.BlockSpec((valid, heads), row_map),
                  pl.BlockSpec((None, heads * hd, nstate), lambda b, c: (b, 0, 0), pipeline_mode=pl.Buffered(1)),
                  pl.BlockSpec(conv_w.shape, full2),
                  pl.BlockSpec((1, conv_dim), full2),
                  pl.BlockSpec((1, heads), full2),
                  pl.BlockSpec((1, heads), full2),
                  pl.BlockSpec((1, d_inner), full2),
                  pl.BlockSpec((1, d_inner), full2),
                  pl.BlockSpec((heads, d_inner), full2, pipeline_mode=pl.Buffered(1))],
        out_specs=[pl.BlockSpec((valid, d_inner), lambda b, c: (b * nchunk + c, 0)),
                   pl.BlockSpec((None, heads * hd, nstate), lambda b, c: (b, 0, 0))],
        out_shape=[jax.ShapeDtypeStruct((nb * nchunk * valid, d_inner), BF16 if valid == q else F32),
                   jax.ShapeDtypeStruct((nb, heads * hd, nstate), F32)],
        scratch_shapes=[pltpu.VMEM((q + SUBLANE, conv_dim), F32),
                        pltpu.VMEM((groups, nstate, gw), F32),
                        pltpu.VMEM((q, gw), BF16), pltpu.VMEM((q, gw), BF16),
                        pltpu.VMEM((q, gw), F32), pltpu.VMEM((q, gw), F32)],
        compiler_params=pltpu.CompilerParams(dimension_semantics=("parallel", "arbitrary"),
                                             vmem_limit_bytes=V7X_VMEM_LIMIT_BYTES),
        name="ssd_mix",
    )(xbc, xbc, conv8, z, dt_raw, h0.reshape(nb, heads * hd, nstate), conv_w, conv_b[None], dt_bias[None],
      a_log[None], dx, norm_w[None], expand)
    return y, hl.reshape(nb, heads, hd, nstate)


LRU_SCAN_LANES = 512


def _gelu_tanh(x):
    return 0.5 * x * (1.0 + jnp.tanh(math.sqrt(2.0 / math.pi) * (x + 0.044715 * (x * x * x))))


def _lru_kernel(gate_ref, xb_ref, prev_ref, conv8_ref, h0_ref, cw_ref, cb_ref, wr_ref, br_ref, wi_ref, bi_ref,
                lam_ref, y_ref, hl_ref, e_ref, a_ref, u_ref, h_ref, *, tm, width):
    t = pl.program_id(1)

    @pl.when(t == 0)
    def _():
        h_ref[...] = h0_ref[...]

    e_ref[0:SUBLANE, :] = jnp.where(t == 0, conv8_ref[...], prev_ref[...])
    e_ref[SUBLANE:SUBLANE + tm, :] = xb_ref[...]
    nconv = cw_ref.shape[0]
    first = SUBLANE - (nconv - 1)
    nblk = wr_ref.shape[0]
    bd = width // nblk
    for k in range(nblk):
        cols = slice(k * bd, (k + 1) * bd)
        s = e_ref[pl.ds(first, tm), cols] * cw_ref[0:1, cols]
        for c in range(1, nconv):
            s = s + e_ref[pl.ds(first + c, tm), cols] * cw_ref[c:c + 1, cols]
        xk = cb_ref[:, cols] + s
        xkb = xk.astype(BF16)
        r = jax.nn.sigmoid(jnp.dot(xkb, wr_ref[k], preferred_element_type=F32) + br_ref[:, cols])
        ig = jax.nn.sigmoid(jnp.dot(xkb, wi_ref[k], preferred_element_type=F32) + bi_ref[:, cols])
        log_a = (-LRU_C * r) * jax.nn.softplus(-lam_ref[:, cols])
        a_ref[:, cols] = jnp.exp(log_a)
        one_minus_a2 = -jnp.tanh(log_a) * (jnp.exp(2.0 * log_a) + 1.0)
        u_ref[:, cols] = jnp.sqrt(one_minus_a2) * ig * xk

    row = lax.broadcasted_iota(jnp.int32, (SUBLANE, LRU_SCAN_LANES), 0)
    for sl in range(width // LRU_SCAN_LANES):
        cols = slice(sl * LRU_SCAN_LANES, (sl + 1) * LRU_SCAN_LANES)

        def body(r, h):
            rows = pl.ds(pl.multiple_of(r * SUBLANE, SUBLANE), SUBLANE)
            a = a_ref[rows, cols]
            u = u_ref[rows, cols]
            for sh in (1, 2, 4):
                a_sh = jnp.where(row >= sh, pltpu.roll(a, sh, 0), 1.0)
                u_sh = jnp.where(row >= sh, pltpu.roll(u, sh, 0), 0.0)
                u = a * u_sh + u
                a = a * a_sh
            hs = a * h + u
            y_ref[rows, cols] = hs * _gelu_tanh(gate_ref[rows, cols])
            return hs[SUBLANE - 1:SUBLANE, :]

        h_ref[:, cols] = lax.fori_loop(0, tm // SUBLANE, body, h_ref[:, cols])
    hl_ref[...] = h_ref[...]


def _lru_mix(gx, row0, nb, seq, conv_buf, h0, conv_w, conv_b, w_r, b_r, w_i, b_i, lam):
    width = gx.shape[1] // 2
    tm = _row_tile(seq, 128) if seq % 16 == 0 else seq
    assert seq % tm == 0 and tm % SUBLANE == 0 and row0 % tm == 0 and width % LRU_SCAN_LANES == 0
    nt = seq // tm
    nconv = conv_w.shape[0]
    conv8 = jnp.pad(conv_buf, ((0, 0), (SUBLANE - (nconv - 1), 0), (0, 0)))
    blk0 = row0 // tm
    sub = tm // SUBLANE

    def row_map(b, t):
        return (blk0 + b * nt + t, 0)

    def xb_map(b, t):
        return (blk0 + b * nt + t, 1)

    def prev_map(b, t):
        return (jnp.maximum((blk0 + b * nt + t) * sub - 1, 0), 1)

    full2 = lambda b, t: (0, 0)
    full3 = lambda b, t: (0, 0, 0)
    y, hl = pl.pallas_call(
        functools.partial(_lru_kernel, tm=tm, width=width),
        grid=(nb, nt),
        in_specs=[pl.BlockSpec((tm, width), row_map),
                  pl.BlockSpec((tm, width), xb_map),
                  pl.BlockSpec((SUBLANE, width), prev_map),
                  pl.BlockSpec((None, SUBLANE, width), lambda b, t: (b, 0, 0)),
                  pl.BlockSpec((None, 1, width), lambda b, t: (b, 0, 0)),
                  pl.BlockSpec(conv_w.shape, full2),
                  pl.BlockSpec((1, width), full2),
                  pl.BlockSpec(w_r.shape, full3),
                  pl.BlockSpec((1, width), full2),
                  pl.BlockSpec(w_i.shape, full3),
                  pl.BlockSpec((1, width), full2),
                  pl.BlockSpec((1, width), full2)],
        out_specs=[pl.BlockSpec((tm, width), lambda b, t: (b * nt + t, 0)),
                   pl.BlockSpec((None, 1, width), lambda b, t: (b, 0, 0))],
        out_shape=[jax.ShapeDtypeStruct((nb * seq, width), F32),
                   jax.ShapeDtypeStruct((nb, 1, width), F32)],
        scratch_shapes=[pltpu.VMEM((tm + SUBLANE, width), F32),
                        pltpu.VMEM((tm, width), F32),
                        pltpu.VMEM((tm, width), F32),
                        pltpu.VMEM((1, width), F32)],
        compiler_params=pltpu.CompilerParams(dimension_semantics=("parallel", "arbitrary"),
                                             vmem_limit_bytes=V7X_VMEM_LIMIT_BYTES),
        name="lru_mix",
    )(gx, gx, gx, conv8, h0[:, None, :], conv_w, conv_b[None], w_r.astype(BF16), b_r.reshape(1, width),
      w_i.astype(BF16), b_i.reshape(1, width), lam[None])
    return y, hl[:, 0]


KEY_CHUNK = 128
TINY = 1e-30
ATT_CHUNKS = 2


def _bucket_of(dist):
    n = jnp.maximum(dist, 0)
    exact = REL_BUCKETS // 2
    log_ratio = jnp.log(jnp.maximum(n, 1).astype(F32) / exact) / math.log(REL_MAX_DIST / exact)
    large = jnp.minimum(exact + (log_ratio * (REL_BUCKETS - exact)).astype(jnp.int32), REL_BUCKETS - 1)
    return jnp.where(n < exact, n, large)


def _bias_from_buckets(bucket, tab_ref, h):
    out = jnp.full(bucket.shape, tab_ref[0, h], F32)
    for k in range(1, REL_BUCKETS):
        out = jnp.where(bucket == k, tab_ref[k, h], out)
    return out


def _head_rmsnorm(x, w_row):
    return x * lax.rsqrt(jnp.mean(x * x, axis=-1, keepdims=True) + NORM_EPS) * w_row


def _bias_tiles_kernel(tab_ref, o_ref, *, tq):
    i = lax.broadcasted_iota(jnp.int32, (tq, KEY_CHUNK), 0)
    j = lax.broadcasted_iota(jnp.int32, (tq, KEY_CHUNK), 1)
    buckets = [_bucket_of(i - j + KEY_CHUNK * k) for k in range(2)]
    nheads = o_ref.shape[1] // tq

    def body(h, carry):
        rows = pl.ds(pl.multiple_of(h * tq, SUBLANE), tq)
        for k in range(2):
            o_ref[k, rows, :] = _bias_from_buckets(buckets[k], tab_ref, h)
        o_ref[2, rows, :] = jnp.full((tq, KEY_CHUNK), tab_ref[REL_BUCKETS - 1, h], F32)
        return carry

    lax.fori_loop(0, nheads, body, 0)


def _bias_tiles(table, tq):
    nheads = table.shape[1]
    return pl.pallas_call(
        functools.partial(_bias_tiles_kernel, tq=tq),
        in_specs=[pl.BlockSpec(memory_space=pltpu.SMEM)],
        out_specs=pl.BlockSpec(memory_space=pltpu.VMEM),
        out_shape=jax.ShapeDtypeStruct((3, nheads * tq, KEY_CHUNK), F32),
        name="nsa_bias_tiles",
    )(table)


PAGES_PER_STEP = 8


def _pages_per_step(npages):
    p = PAGES_PER_STEP
    while npages % p:
        p //= 2
    return p


def _slot_spec(src, lead, slot, groups, d, k, pps):
    gd = groups * d
    if src.shape[-2:] == (groups, d):
        nlead = len(lead)
        assert src.ndim == nlead + 5 and src.shape[-4] == KEY_CHUNK
        return pl.BlockSpec((None,) * (nlead + 1) + (KEY_CHUNK, None, groups, d),
                            lambda b, p, t: tuple(lead) + (t[b, p * pps + k], 0, slot, 0, 0))
    assert not lead
    if src.ndim == 2:
        return pl.BlockSpec((KEY_CHUNK, gd), lambda b, p, t: (t[b, p * pps + k], slot))
    assert src.ndim == 3 and src.shape[1] == KEY_CHUNK
    return pl.BlockSpec((None, KEY_CHUNK, gd), lambda b, p, t: (t[b, p * pps + k], 0, slot))


def _slab(ref, g, d):
    return ref[:, g, :] if len(ref.shape) == 3 else ref[:, g * d:(g + 1) * d]


def _compress_ab_kernel(tbl_ref, *refs, groups, d, pps):
    x_refs, (pe_ref, w1_ref, o_ref, slab_ref) = refs[:2 * pps], refs[2 * pps:]
    half = CMP_LEN // 2
    nchunk = KEY_CHUNK // CMP_STRIDE
    for k in range(pps):
        for s in range(2):
            for g in range(groups):
                slab_ref[k, s * groups + g] = _slab(x_refs[2 * k + s], g, d)
    rows_g = pps * nchunk
    for s in range(2):
        acc = [jnp.zeros((groups * rows_g, d), F32) for _ in range(2)]
        for l in range(half):
            x = jnp.concatenate(
                [slab_ref[k, s * groups + g, pl.ds(l, nchunk, stride=CMP_STRIDE), :]
                 for g in range(groups) for k in range(pps)], axis=0)
            for part in range(2):
                ll = part * half + l
                xa = (x + pe_ref[s, ll:ll + 1, :]).astype(BF16)
                acc[part] = acc[part] + jnp.dot(xa, w1_ref[s, ll], preferred_element_type=F32)
        for part in range(2):
            for g in range(groups):
                o_ref[s, part, g] = acc[part][g * rows_g:(g + 1) * rows_g]


def _compress_ab(src, lead, tbl, pe, w1b, groups, d):
    nb, npages = tbl.shape
    pps = _pages_per_step(npages)
    nchunk = KEY_CHUNK // CMP_STRIDE
    assert CMP_LEN == 2 * CMP_STRIDE
    grid_spec = pltpu.PrefetchScalarGridSpec(
        num_scalar_prefetch=1, grid=(nb, npages // pps),
        in_specs=[_slot_spec(src, lead, s, groups, d, k, pps) for k in range(pps) for s in range(2)]
        + [pl.BlockSpec(pe.shape, lambda b, p, t: (0, 0, 0)),
           pl.BlockSpec(w1b.shape, lambda b, p, t: (0, 0, 0, 0))],
        out_specs=pl.BlockSpec((None, 2, 2, groups, pps * nchunk, d), lambda b, p, t: (b, 0, 0, 0, p, 0)),
        scratch_shapes=[pltpu.VMEM((pps, 2 * groups, KEY_CHUNK, d), F32)])
    return pl.pallas_call(
        functools.partial(_compress_ab_kernel, groups=groups, d=d, pps=pps),
        grid_spec=grid_spec,
        out_shape=jax.ShapeDtypeStruct((nb, 2, 2, groups, npages * nchunk, d), F32),
        compiler_params=pltpu.CompilerParams(dimension_semantics=("parallel", "arbitrary"),
                                             vmem_limit_bytes=V7X_VMEM_LIMIT_BYTES),
        name="nsa_compress_ab",
    )(tbl, *([src] * (2 * pps)), pe, w1b)


def _compress_finish_kernel(ab_ref, w2_ref, kw_ref, o_ref, *, groups, d):
    n = ab_ref.shape[3]
    for s in range(2):
        for g in range(groups):
            first = ab_ref[s, 0, g]
            second = pltpu.roll(ab_ref[s, 1, g], n - 1, 0)
            hid = jax.nn.silu(first + second).astype(BF16)
            out = jnp.dot(hid, w2_ref[s], preferred_element_type=F32)
            if s == 0:
                out = _head_rmsnorm(out, kw_ref[...])
            o_ref[s, :, g * d:(g + 1) * d] = out.astype(o_ref.dtype)


def _compress_finish(ab, w2b, k_norm_row, groups, d):
    nb, _, _, _, n, _ = ab.shape
    return pl.pallas_call(
        functools.partial(_compress_finish_kernel, groups=groups, d=d),
        grid=(nb,),
        in_specs=[pl.BlockSpec((None, 2, 2, groups, n, d), lambda b: (b, 0, 0, 0, 0, 0)),
                  pl.BlockSpec(w2b.shape, lambda b: (0, 0, 0)),
                  pl.BlockSpec((1, d), lambda b: (0, 0))],
        out_specs=pl.BlockSpec((None, 2, n, groups * d), lambda b: (b, 0, 0, 0)),
        out_shape=jax.ShapeDtypeStruct((nb, 2, n, groups * d), BF16),
        compiler_params=pltpu.CompilerParams(dimension_semantics=("parallel",)),
        name="nsa_compress_finish",
    )(ab, w2b, k_norm_row)


def _select_kernel(tab_ref, q_ref, kcv_ref, qw_ref, ocmp_ref, msel_ref, *, tq, groups, hpg, d, q_pos0, n_cmp, n_sel,
                   nselp, scale):
    qi = pl.program_id(1)
    pos0 = q_pos0 + qi * tq
    nck = kcv_ref.shape[1]
    t_c = lax.broadcasted_iota(jnp.int32, (tq, nck), 0)
    n_c = lax.broadcasted_iota(jnp.int32, (tq, nck), 1)
    dist = pos0 + t_c - (n_c * CMP_STRIDE + CMP_LEN - 1)
    cmask = (dist >= 0) & (n_c < n_cmp)
    cmaskf = cmask.astype(F32)
    bucket = _bucket_of(dist)
    n_r = lax.broadcasted_iota(jnp.int32, (nck, nselp), 0)
    j_r = lax.broadcasted_iota(jnp.int32, (nck, nselp), 1)
    cover = ((n_r * CMP_STRIDE < j_r * SEL_LEN + SEL_LEN) & (n_r * CMP_STRIDE + CMP_LEN > j_r * SEL_LEN)
             & (n_r < n_cmp) & (j_r < n_sel)).astype(BF16)
    j_s = lax.broadcasted_iota(jnp.int32, (tq, nselp), 1)
    pos = pos0 + lax.broadcasted_iota(jnp.int32, (tq, nselp), 0)
    cur = pos // SEL_LEN
    visible = (j_s * SEL_LEN <= pos) & (j_s < n_sel)
    forced = ((j_s == 0) | (j_s == cur) | (j_s == cur - 1)).astype(F32)
    for g in range(groups):
        kc = kcv_ref[0, :, g * d:(g + 1) * d]
        vc = kcv_ref[1, :, g * d:(g + 1) * d]
        p_grp = jnp.zeros((tq, nck), F32)
        for e in range(hpg):
            h = g * hpg + e
            cols = slice(h * d, (h + 1) * d)
            qn = _head_rmsnorm(q_ref[:, cols], qw_ref[...]).astype(BF16)
            logits = lax.dot_general(qn, kc, (((1,), (1,)), ((), ())), preferred_element_type=F32) * scale
            logits = jnp.where(cmask, logits + _bias_from_buckets(bucket, tab_ref, h), NEG_INF)
            m = jnp.max(logits, axis=-1, keepdims=True)
            p = jnp.exp(logits - m)
            p = p / jnp.sum(p, axis=-1, keepdims=True) * cmaskf
            ocmp_ref[:, cols] = jnp.dot(p.astype(BF16), vc, preferred_element_type=F32)
            p_grp = p_grp + p
        p_sel = jnp.dot(p_grp.astype(BF16), cover, preferred_element_type=F32)
        score = jnp.where(visible, p_sel + FORCE_BONUS * forced, NEG_INF)
        cnt = jnp.zeros((tq, nselp), jnp.int32)
        for jp in range(n_sel):
            col = score[:, jp:jp + 1]
            beats = (col > score) | ((col == score) & (j_s > jp))
            cnt = cnt + beats.astype(jnp.int32)
        msel_ref[:, g * nselp:(g + 1) * nselp] = ((cnt < SEL_TOPK) & visible).astype(F32)


def _nsa_select(table, qp, row0, kcv, q_norm_row, *, nb, seq, tq, q_pos0, n_cmp, n_sel, groups, hpg, d):
    nq = seq // tq
    n_q = groups * hpg * d
    nselp = -(-n_sel // LANE) * LANE
    nck = kcv.shape[2]
    assert row0 % tq == 0
    blk0 = row0 // tq
    kern = functools.partial(_select_kernel, tq=tq, groups=groups, hpg=hpg, d=d, q_pos0=q_pos0, n_cmp=n_cmp,
                             n_sel=n_sel, nselp=nselp, scale=d ** -0.5)
    return pl.pallas_call(
        kern, grid=(nb, nq),
        in_specs=[pl.BlockSpec(memory_space=pltpu.SMEM),
                  pl.BlockSpec((tq, n_q), lambda b, q: (blk0 + b * nq + q, 0)),
                  pl.BlockSpec((None, 2, nck, groups * d), lambda b, q: (b, 0, 0, 0)),
                  pl.BlockSpec((1, d), lambda b, q: (0, 0))],
        out_specs=[pl.BlockSpec((tq, n_q), lambda b, q: (b * nq + q, 0)),
                   pl.BlockSpec((tq, groups * nselp), lambda b, q: (b * nq + q, 0))],
        out_shape=[jax.ShapeDtypeStruct((nb * seq, n_q), F32),
                   jax.ShapeDtypeStruct((nb * seq, groups * nselp), F32)],
        compiler_params=pltpu.CompilerParams(dimension_semantics=("parallel", "parallel"),
                                             vmem_limit_bytes=V7X_VMEM_LIMIT_BYTES),
        name="nsa_cmp_select",
    )(table, qp, kcv, q_norm_row)


def _kv_pack_kernel(tbl_ref, *refs, groups, d, pps):
    x_refs, (kw_ref, k_ref, v_ref, slab_ref) = refs[:2 * pps], refs[2 * pps:]

    def dense(x_ref, i, g):
        if len(x_ref.shape) == 3:
            slab_ref[i * groups + g] = _slab(x_ref, g, d)
            return slab_ref[i * groups + g]
        return _slab(x_ref, g, d)

    for k in range(pps):
        rows = slice(k * KEY_CHUNK, (k + 1) * KEY_CHUNK)
        for g in range(groups):
            cols = slice(g * d, (g + 1) * d)
            k_ref[rows, cols] = _head_rmsnorm(dense(x_refs[2 * k], 0, g), kw_ref[...]).astype(k_ref.dtype)
            v_ref[rows, cols] = dense(x_refs[2 * k + 1], 1, g).astype(v_ref.dtype)


def _kv_pack(src, lead, tbl, k_slot, k_norm_row, groups, d):
    nb, npages = tbl.shape
    pps = _pages_per_step(npages)
    gd = groups * d
    grid_spec = pltpu.PrefetchScalarGridSpec(
        num_scalar_prefetch=1, grid=(nb, npages // pps),
        in_specs=[_slot_spec(src, lead, k_slot + s, groups, d, k, pps) for k in range(pps) for s in range(2)]
        + [pl.BlockSpec((1, d), lambda b, p, t: (0, 0))],
        out_specs=[pl.BlockSpec((None, pps * KEY_CHUNK, gd), lambda b, p, t: (b, p, 0)),
                   pl.BlockSpec((None, pps * KEY_CHUNK, gd), lambda b, p, t: (b, p, 0))],
        scratch_shapes=[pltpu.VMEM((2 * groups, KEY_CHUNK, d), F32)])
    return pl.pallas_call(
        functools.partial(_kv_pack_kernel, groups=groups, d=d, pps=pps),
        grid_spec=grid_spec,
        out_shape=[jax.ShapeDtypeStruct((nb, npages * KEY_CHUNK, gd), BF16)] * 2,
        compiler_params=pltpu.CompilerParams(dimension_semantics=("parallel", "parallel"),
                                             vmem_limit_bytes=V7X_VMEM_LIMIT_BYTES),
        name="nsa_kv_pack",
    )(tbl, *([src] * (2 * pps)), k_norm_row)


def _attend_kernel(tiles_ref, q_ref, ocmp_ref, gate_ref, msel_ref, ks_ref, vs_ref, kw_ref, vw_ref, qw_ref, o_ref,
                   qn_ref, ms_ref, ls_ref, as_ref, mw_ref, lw_ref, aw_ref, *, tq, groups, hpg, d, q_pos0,
                   win_chunk0, n_win_steps, nselp, scale):
    qi = pl.program_id(1)
    c = pl.program_id(2)
    pos0 = q_pos0 + qi * tq
    p_hi = (pos0 + tq - 1) // KEY_CHUNK // ATT_CHUNKS
    rows_g = hpg * tq
    nheads = groups * hpg

    @pl.when(c == 0)
    def _():
        for h in range(nheads):
            qn_ref[h * tq:(h + 1) * tq, :] = _head_rmsnorm(q_ref[:, h * d:(h + 1) * d], qw_ref[...]).astype(BF16)
        for m_ref, l_ref, a_ref in ((ms_ref, ls_ref, as_ref), (mw_ref, lw_ref, aw_ref)):
            m_ref[...] = jnp.full(m_ref.shape, NEG_INF, F32)
            l_ref[...] = jnp.zeros(l_ref.shape, F32)
            a_ref[...] = jnp.zeros(a_ref.shape, F32)

    assert tq & (tq - 1) == 0 and d == LANE
    t_idx = lax.broadcasted_iota(jnp.int32, (rows_g, KEY_CHUNK), 0) & (tq - 1)
    j_idx = lax.broadcasted_iota(jnp.int32, (rows_g, KEY_CHUNK), 1)
    ones_v = jnp.ones((KEY_CHUNK, d), BF16)

    def step(g, k_ref, v_ref, m_ref, l_ref, a_ref, chunk0, masks):
        rows = slice(g * rows_g, (g + 1) * rows_g)
        q = qn_ref[rows, :]
        logits = []
        for u, mask in enumerate(masks):
            delta = pos0 - (chunk0 + u) * KEY_CHUNK
            tile = jnp.clip(delta // KEY_CHUNK, 0, 2)
            k = k_ref[u * KEY_CHUNK:(u + 1) * KEY_CHUNK, g * d:(g + 1) * d]
            s = lax.dot_general(q, k, (((1,), (1,)), ((), ())), preferred_element_type=F32) * scale
            s = s + tiles_ref[tile, pl.ds(g * rows_g, rows_g), :]
            logits.append(jnp.where(mask, s, NEG_INF))
        m_prev = m_ref[rows, :]
        m_new = m_prev
        for s in logits:
            m_new = jnp.maximum(m_new, jnp.max(s, axis=-1, keepdims=True))
        alpha = jnp.exp(m_prev - m_new)
        pv = None
        for u, (s, mask) in enumerate(zip(logits, masks)):
            p = jnp.where(mask, jnp.exp(s - m_new), 0.0).astype(BF16)
            v_ext = jnp.concatenate([v_ref[u * KEY_CHUNK:(u + 1) * KEY_CHUNK, g * d:(g + 1) * d], ones_v], axis=1)
            part = jnp.dot(p, v_ext, preferred_element_type=F32)
            pv = part if pv is None else pv + part
        l_ref[rows, :] = alpha * l_ref[rows, :] + pv[:, d:]
        a_ref[rows, :] = alpha * a_ref[rows, :] + pv[:, :d]
        m_ref[rows, :] = m_new

    @pl.when(c <= p_hi)
    def _():
        chunk0 = c * ATT_CHUNKS
        blocks_per_chunk = KEY_CHUNK // SEL_LEN
        jr = lax.broadcasted_iota(jnp.int32, (nselp, KEY_CHUNK), 0)
        lr = lax.broadcasted_iota(jnp.int32, (nselp, KEY_CHUNK), 1)
        causal, expand = [], []
        for u in range(ATT_CHUNKS):
            causal.append(pos0 - (chunk0 + u) * KEY_CHUNK + t_idx - j_idx >= 0)
            expand.append((jr == (chunk0 + u) * blocks_per_chunk + lr // SEL_LEN).astype(BF16))
        for g in range(groups):
            ms = msel_ref[:, g * nselp:(g + 1) * nselp].astype(BF16)
            masks = []
            for u in range(ATT_CHUNKS):
                sel = jnp.dot(ms, expand[u], preferred_element_type=F32)
                masks.append(causal[u] & (jnp.concatenate([sel] * hpg, axis=0) > 0.5))
            step(g, ks_ref, vs_ref, ms_ref, ls_ref, as_ref, chunk0, masks)

    wpair = p_hi - c

    @pl.when((c < n_win_steps) & (wpair * ATT_CHUNKS >= win_chunk0))
    def _():
        chunk0 = wpair * ATT_CHUNKS
        masks = []
        for u in range(ATT_CHUNKS):
            dist = pos0 - (chunk0 + u) * KEY_CHUNK + t_idx - j_idx
            masks.append((dist >= 0) & (dist < WINDOW))
        for g in range(groups):
            step(g, kw_ref, vw_ref, mw_ref, lw_ref, aw_ref, chunk0, masks)

    @pl.when(c == pl.num_programs(2) - 1)
    def _():
        gates = jax.nn.sigmoid(gate_ref[...])
        for h in range(nheads):
            r = slice(h * tq, (h + 1) * tq)
            cols = slice(h * d, (h + 1) * d)
            o_sel = as_ref[r, :] / jnp.maximum(ls_ref[r, :], TINY)
            o_win = aw_ref[r, :] / jnp.maximum(lw_ref[r, :], TINY)
            o = (gates[:, 3 * h:3 * h + 1] * ocmp_ref[:, cols] + gates[:, 3 * h + 1:3 * h + 2] * o_sel
                 + gates[:, 3 * h + 2:3 * h + 3] * o_win)
            o_ref[:, cols] = o.astype(o_ref.dtype)


def _nsa_attend(tiles, qp, gp, row0, ocmp, msel, ksel, vsel, kwin, vwin, q_norm_row, *, nb, seq, tq, q_pos0,
                win_chunk0, groups, hpg, d):
    nq = seq // tq
    n_q = groups * hpg * d
    gd = groups * d
    nheads = groups * hpg
    nselp = msel.shape[1] // groups
    pair = ATT_CHUNKS * KEY_CHUNK
    assert ksel.shape[1] % pair == 0 and kwin.shape[1] % pair == 0 and win_chunk0 % ATT_CHUNKS == 0
    npair = ksel.shape[1] // pair
    nwpair = kwin.shape[1] // pair
    wpair0 = win_chunk0 // ATT_CHUNKS
    n_win_steps = (WINDOW // KEY_CHUNK + ATT_CHUNKS - 1) // ATT_CHUNKS + 1
    ncg = max(npair, n_win_steps)
    assert row0 % tq == 0 and q_pos0 % KEY_CHUNK == 0 and (tq == KEY_CHUNK or nq == 1) and tq <= KEY_CHUNK
    blk0 = row0 // tq

    def p_hi(q):
        return (q_pos0 + q * tq + tq - 1) // KEY_CHUNK // ATT_CHUNKS

    def sel_map(b, q, c):
        return (b, jnp.minimum(c, p_hi(q)), 0)

    def win_map(b, q, c):
        return (b, jnp.clip(p_hi(q) - jnp.minimum(c, n_win_steps - 1) - wpair0, 0, nwpair - 1), 0)

    grp_map = lambda b, q, c: (b * nq + q, 0)
    all_map = lambda b, q, c: (blk0 + b * nq + q, 0)
    kern = functools.partial(_attend_kernel, tq=tq, groups=groups, hpg=hpg, d=d, q_pos0=q_pos0,
                             win_chunk0=win_chunk0, n_win_steps=n_win_steps, nselp=nselp, scale=d ** -0.5)
    return pl.pallas_call(
        kern, grid=(nb, nq, ncg),
        in_specs=[pl.BlockSpec(tiles.shape, lambda b, q, c: (0, 0, 0)),
                  pl.BlockSpec((tq, n_q), all_map),
                  pl.BlockSpec((tq, n_q), grp_map),
                  pl.BlockSpec((tq, gp.shape[1]), all_map),
                  pl.BlockSpec((tq, groups * nselp), grp_map),
                  pl.BlockSpec((None, pair, gd), sel_map),
                  pl.BlockSpec((None, pair, gd), sel_map),
                  pl.BlockSpec((None, pair, gd), win_map),
                  pl.BlockSpec((None, pair, gd), win_map),
                  pl.BlockSpec((1, d), lambda b, q, c: (0, 0))],
        out_specs=pl.BlockSpec((tq, n_q), grp_map),
        out_shape=jax.ShapeDtypeStruct((nb * seq, n_q), BF16 if tq % (2 * SUBLANE) == 0 else F32),
        scratch_shapes=[pltpu.VMEM((nheads * tq, d), BF16),
                        pltpu.VMEM((nheads * tq, d), F32), pltpu.VMEM((nheads * tq, d), F32),
                        pltpu.VMEM((nheads * tq, d), F32),
                        pltpu.VMEM((nheads * tq, d), F32), pltpu.VMEM((nheads * tq, d), F32),
                        pltpu.VMEM((nheads * tq, d), F32)],
        compiler_params=pltpu.CompilerParams(dimension_semantics=("parallel", "parallel", "arbitrary"),
                                             vmem_limit_bytes=V7X_VMEM_LIMIT_BYTES),
        name="nsa_attend",
    )(tiles, qp, ocmp, gp, msel, ksel, vsel, kwin, vwin, q_norm_row)


def _nsa_group(qp, kvp, gp, row0, nb, seq, q_pos0, past, win_past, w, groups, hpg, d):
    q_norm_w, k_norm_w, cmp_pe, cmp_w1, cmp_w2, table = w
    gd = groups * d
    tq = KEY_CHUNK if seq % KEY_CHUNK == 0 else seq
    assert tq % SUBLANE == 0
    w1b, w2b = cmp_w1.astype(BF16), cmp_w2.astype(BF16)
    if past is None:
        assert seq % KEY_CHUNK == 0 and row0 % KEY_CHUNK == 0
        npg = seq // KEY_CHUNK
        tbl = row0 // KEY_CHUNK + jnp.arange(nb * npg, dtype=jnp.int32).reshape(nb, npg)
        src = kvp
        ab = _compress_ab(src, (), tbl, cmp_pe, w1b, groups, d)
        ksel, vsel = _kv_pack(src, (), tbl, 2, k_norm_w[1:2], groups, d)
        kwin, vwin = _kv_pack(src, (), tbl, 4, k_norm_w[2:3], groups, d)
        tk = seq
        win_chunk0 = 0
    else:
        cache, lead, tbl = past
        npg = tbl.shape[1]
        past_len = npg * KEY_CHUNK
        assert q_pos0 == past_len and seq < CMP_STRIDE and seq <= KEY_CHUNK
        ab = _compress_ab(cache, lead, tbl, cmp_pe, w1b, groups, d)
        ksel_p, vsel_p = _kv_pack(cache, lead, tbl, 2, k_norm_w[1:2], groups, d)
        wb = win_past.shape[1]
        assert wb % KEY_CHUNK == 0 and wb <= WINDOW and (past_len - wb) % KEY_CHUNK == 0
        nwp = wb // KEY_CHUNK
        wtbl = jnp.arange(nb * nwp, dtype=jnp.int32).reshape(nb, nwp)
        win_pages = win_past.reshape((nb * nwp, KEY_CHUNK) + win_past.shape[2:])
        kwin_p, vwin_p = _kv_pack(win_pages, (), wtbl, 0, k_norm_w[2:3], groups, d)
        new = kvp[row0:row0 + nb * seq].reshape(nb, seq, kvp.shape[1])
        new = jnp.pad(new, ((0, 0), (0, KEY_CHUNK - seq), (0, 0)))
        ntbl = jnp.arange(nb, dtype=jnp.int32).reshape(nb, 1)
        ksel_n, vsel_n = _kv_pack(new, (), ntbl, 2, k_norm_w[1:2], groups, d)
        kwin_n, vwin_n = _kv_pack(new, (), ntbl, 4, k_norm_w[2:3], groups, d)

        def join(past_part, new_part):
            rows = past_part.shape[1] + new_part.shape[1]
            fill = jnp.zeros((nb, -rows % (ATT_CHUNKS * KEY_CHUNK), gd), past_part.dtype)
            return jnp.concatenate([past_part, new_part, fill], axis=1)

        ksel, vsel = join(ksel_p, ksel_n), join(vsel_p, vsel_n)
        kwin, vwin = join(kwin_p, kwin_n), join(vwin_p, vwin_n)
        tk = past_len + seq
        win_chunk0 = (past_len - wb) // KEY_CHUNK
    n_cmp = (tk - CMP_LEN) // CMP_STRIDE + 1
    n_sel = -(-tk // SEL_LEN)
    assert n_cmp <= ab.shape[4] - 1
    kcv = _compress_finish(ab, w2b, k_norm_w[0:1], groups, d)
    ocmp, msel = _nsa_select(table, qp, row0, kcv, q_norm_w[None], nb=nb, seq=seq, tq=tq, q_pos0=q_pos0,
                             n_cmp=n_cmp, n_sel=n_sel, groups=groups, hpg=hpg, d=d)
    tiles = _bias_tiles(table, tq)
    return _nsa_attend(tiles, qp, gp, row0, ocmp, msel, ksel, vsel, kwin, vwin, q_norm_w[None], nb=nb, seq=seq,
                       tq=tq, q_pos0=q_pos0, win_chunk0=win_chunk0, groups=groups, hpg=hpg, d=d)


def kernel(x_prompt, x_sample, state_ssd, state_ssd_conv, state_lru, state_lru_conv, cache_nsa_kv, cache_nsa_win, page_table, ffn_norm, w_ffn_in, w_ffn_out, mix_norm, ssd_w_in, ssd_conv_w, ssd_conv_b, ssd_dt_bias, ssd_a_log, ssd_d, ssd_norm, ssd_w_out, lru_w_in, lru_conv_w, lru_conv_b, lru_w_r, lru_b_r, lru_w_i, lru_b_i, lru_lambda, lru_w_out, nsa_w_in, nsa_q_norm, nsa_k_norm, nsa_cmp_pe, nsa_cmp_w1, nsa_cmp_w2, rel_bias_table, nsa_w_out):
    prm = dict(ffn_norm=ffn_norm, w_ffn_in=w_ffn_in, w_ffn_out=w_ffn_out)
    bp, sp, dm = x_prompt.shape
    bs, ss, _ = x_sample.shape
    mp, ms = bp * sp, bs * ss
    depth = mix_norm.shape[0]
    G = NSA_GROUPS
    E = NSA_HEADS // G
    d_head = dm // NSA_HEADS
    n_pages = page_table.shape[1]
    page = cache_nsa_kv.shape[2]
    past_len = n_pages * page
    assert page == KEY_CHUNK

    def merge(tp, ts):
        return jnp.concatenate([tp.reshape(mp, -1), ts.reshape(ms, -1)], axis=0)

    def tail(buf, rows, prompt, c0, c1):
        n = buf.shape[1]
        nb_, seq, r0 = (bp, sp, 0) if prompt else (bs, ss, mp)
        k = min(n, seq)
        new = jnp.stack([rows[r0 + (b + 1) * seq - k:r0 + (b + 1) * seq, c0:c1] for b in range(nb_)])
        return jnp.concatenate([buf, new.reshape((nb_, k) + buf.shape[2:])], axis=1)[:, -n:]

    x = merge(x_prompt, x_sample)
    outs = {k: [] for k in ('ssd_h_p', 'ssd_h_s', 'ssd_buf_p', 'ssd_buf_s', 'lru_h_p', 'lru_h_s',
                            'lru_buf_p', 'lru_buf_s', 'rows_p', 'rows_s', 'win_p', 'win_s')}
    for i in range(depth):
        kind, j = i % N_MIXERS, i // N_MIXERS
        x = _ffn(x, prm, i, 0)
        hn = _rmsnorm(x, mix_norm[i][None])
        if kind == 0:
            d_inner = ssd_w_out.shape[1]
            conv_dim = ssd_conv_w.shape[2]
            heads = ssd_dt_bias.shape[1]
            z = _matmul(hn, ssd_w_in, (j,), kdim=dm, n0=0, ndim=d_inner, tn=512, name="ssd_in_z")
            xbc = _matmul(hn, ssd_w_in, (j,), kdim=dm, n0=d_inner, ndim=conv_dim, tn=512, name="ssd_in_xbc")
            dt_raw = _matmul(hn, ssd_w_in, (j,), kdim=dm, n0=d_inner + conv_dim, ndim=heads, tn=heads,
                             name="ssd_in_dt")
            w = (ssd_conv_w[j], ssd_conv_b[j], ssd_dt_bias[j], ssd_a_log[j], ssd_d[j], ssd_norm[j])
            q = SSD_CHUNK
            assert sp % q == 0
            nbuf = ssd_conv_w.shape[1] - 1
            hd, nstate = d_inner // heads, state_ssd.shape[-1]
            shape = (SSD_GROUPS, nstate, hd)
            zero_buf = jnp.zeros((bp, nbuf, conv_dim), F32)
            zero_h = jnp.zeros((bp,) + state_ssd.shape[2:], F32)
            y_p, st_p = _ssd_mix(xbc, z, dt_raw, 0, bp, sp // q, q, zero_buf, zero_h, *w, *shape)
            y_s, st_s = _ssd_mix(xbc, z, dt_raw, mp, bs, 1, ss, state_ssd_conv[j], state_ssd[j], *w, *shape)
            outs['ssd_h_p'].append(st_p); outs['ssd_h_s'].append(st_s)
            outs['ssd_buf_p'].append(tail(zero_buf, xbc, True, 0, conv_dim))
            outs['ssd_buf_s'].append(tail(state_ssd_conv[j], xbc, False, 0, conv_dim))
            x = _out_proj(jnp.concatenate([y_p, y_s.astype(BF16)], axis=0), ssd_w_out, (j,), x, 1.0, "ssd_out")
        elif kind == 1:
            width = lru_w_out.shape[1]
            nbuf = lru_conv_w.shape[1] - 1
            gx = _matmul(hn, lru_w_in, (j,), kdim=dm, n0=0, ndim=2 * width, tn=512, name="lru_in")
            w = (lru_conv_w[j], lru_conv_b[j], lru_w_r[j], lru_b_r[j], lru_w_i[j], lru_b_i[j], lru_lambda[j])
            zero_buf = jnp.zeros((bp, nbuf, width), F32)
            y_p, st_p = _lru_mix(gx, 0, bp, sp, zero_buf, jnp.zeros((bp, width), F32), *w)
            y_s, st_s = _lru_mix(gx, mp, bs, ss, state_lru_conv[j], state_lru[j], *w)
            outs['lru_h_p'].append(st_p); outs['lru_h_s'].append(st_s)
            outs['lru_buf_p'].append(tail(zero_buf, gx, True, width, 2 * width))
            outs['lru_buf_s'].append(tail(state_lru_conv[j], gx, False, width, 2 * width))
            x = _out_proj(jnp.concatenate([y_p, y_s], axis=0).astype(BF16), lru_w_out, (j,), x, 1.0, "lru_out")
        else:
            n_q, n_rows, n_kv = NSA_HEADS * d_head, 4 * G * d_head, 6 * G * d_head
            qp = _matmul(hn, nsa_w_in, (j,), kdim=dm, n0=0, ndim=n_q, tn=512, name="nsa_in_q")
            kvp = _matmul(hn, nsa_w_in, (j,), kdim=dm, n0=n_q, ndim=n_kv, tn=512, name="nsa_in_kv")
            n_gate = nsa_w_in.shape[2] - n_q - n_kv
            w_gate = jnp.pad(nsa_w_in[j][None, :, n_q + n_kv:], ((0, 0), (0, 0), (0, LANE - n_gate)))
            gp = _matmul(hn, w_gate, (0,), kdim=dm, n0=0, ndim=LANE, tn=LANE, name="nsa_in_gate")[:, :n_gate]
            w = (nsa_q_norm[j], nsa_k_norm[j], nsa_cmp_pe[j], nsa_cmp_w1[j], nsa_cmp_w2[j], rel_bias_table)
            o_p = _nsa_group(qp, kvp, gp, 0, bp, sp, 0, None, None, w, G, E, d_head)
            o_s = _nsa_group(qp, kvp, gp, mp, bs, ss, past_len, (cache_nsa_kv, (j,), page_table), cache_nsa_win[j],
                             w, G, E, d_head)
            outs['rows_p'].append(kvp[:mp, :n_rows].reshape(bp, sp, 4, G, d_head))
            outs['rows_s'].append(kvp[mp:, :n_rows].reshape(bs, ss, 4, G, d_head))
            outs['win_p'].append(tail(jnp.zeros((bp,) + cache_nsa_win.shape[2:], F32), kvp, True, n_rows, n_kv))
            outs['win_s'].append(tail(cache_nsa_win[j], kvp, False, n_rows, n_kv))
            x = _out_proj(jnp.concatenate([o_p.astype(BF16), o_s.astype(BF16)], axis=0), nsa_w_out, (j,), x, 1.0,
                          "nsa_out")
        x = _ffn(x, prm, i, 1)
    y_prompt, y_sample = x[:mp].reshape(bp, sp, dm), x[mp:].reshape(bs, ss, dm)
    st = {k: jnp.stack(v) for k, v in outs.items()}
    return (y_prompt, y_sample, st['ssd_h_p'], st['ssd_h_s'], st['ssd_buf_p'], st['ssd_buf_s'],
            st['lru_h_p'], st['lru_h_s'], st['lru_buf_p'], st['lru_buf_s'],
            st['rows_p'], st['rows_s'], st['win_p'], st['win_s'])
```

```python
import functools
import math

import jax
import jax.numpy as jnp
from jax import lax
from jax.experimental import pallas as pl
from jax.experimental.pallas import tpu as pltpu

F32 = jnp.float32
BF16 = jnp.bfloat16

NORM_EPS = 1e-6
N_MIXERS = 3
SSD_HEAD_DIM = 64
SSD_STATE = 128
SSD_GROUPS = 8
SSD_CHUNK = 128
LRU_BLOCKS = 16
LRU_C = 8.0
NSA_HEADS = 32
NSA_GROUPS = 4
CMP_LEN = 32
CMP_STRIDE = 16
SEL_LEN = 64
SEL_TOPK = 16
WINDOW = 512
SEL_QBLOCK = 64
WIN_QBLOCK = 128
FORCE_BONUS = 1e4
NEG_INF = -1e30
REL_BUCKETS = 32
REL_MAX_DIST = 128

V7X_VMEM_LIMIT_BYTES = 56 * 1024 * 1024
LANE = 128
SUBLANE = 8


def _row_tile(m, cap):
    best = None
    for t in range(16, min(m, cap) + 1, 16):
        if m % t == 0:
            best = t
    assert best is not None, m
    return best


def _rmsnorm_kernel(x_ref, w_ref, o_ref):
    x = x_ref[...]
    y = x * lax.rsqrt(jnp.mean(x * x, axis=-1, keepdims=True) + NORM_EPS)
    o_ref[...] = (y * w_ref[...]).astype(o_ref.dtype)


def _rmsnorm(x, w_row):
    m, d = x.shape
    tm = _row_tile(m, 704)
    return pl.pallas_call(
        _rmsnorm_kernel,
        grid=(m // tm,),
        in_specs=[pl.BlockSpec((tm, d), lambda i: (i, 0)),
                  pl.BlockSpec((1, d), lambda i: (0, 0))],
        out_specs=pl.BlockSpec((tm, d), lambda i: (i, 0)),
        out_shape=jax.ShapeDtypeStruct((m, d), BF16),
        compiler_params=pltpu.CompilerParams(dimension_semantics=("parallel",),
                                             vmem_limit_bytes=V7X_VMEM_LIMIT_BYTES),
        name="rmsnorm",
    )(x, w_row)


def _mm_kernel(a_ref, w_ref, o_ref):
    acc = jnp.dot(a_ref[...], w_ref[...].astype(BF16), preferred_element_type=F32)
    o_ref[...] = acc.astype(o_ref.dtype)


def _mm_res_kernel(a_ref, w_ref, r_ref, o_ref, *, scale):
    acc = jnp.dot(a_ref[...], w_ref[...].astype(BF16), preferred_element_type=F32)
    o_ref[...] = r_ref[...] + scale * acc


def _mm_swiglu_kernel(a_ref, wg_ref, wu_ref, o_ref):
    a = a_ref[...]
    g = jnp.dot(a, wg_ref[...].astype(BF16), preferred_element_type=F32)
    u = jnp.dot(a, wu_ref[...].astype(BF16), preferred_element_type=F32)
    o_ref[...] = (jax.nn.silu(g) * u).astype(o_ref.dtype)


def _w_spec(w, lead, kdim, tn, k_blk, n_blk0):
    nlead = len(lead)
    return pl.BlockSpec((None,) * nlead + (kdim, tn),
                        lambda i, j: tuple(lead) + (k_blk, n_blk0 + j))


def _matmul(a, w, lead, *, a_k0=0, w_k0=0, kdim, n0, ndim, tn, tm_cap=1408,
            res=None, scale=1.0, out_dtype=F32, name="matmul"):
    m = a.shape[0]
    tm = _row_tile(m, tm_cap)
    assert a_k0 % kdim == 0 and w_k0 % kdim == 0 and n0 % tn == 0 and ndim % tn == 0
    a_spec = pl.BlockSpec((tm, kdim), lambda i, j: (i, a_k0 // kdim))
    w_spec = _w_spec(w, lead, kdim, tn, w_k0 // kdim, n0 // tn)
    o_spec = pl.BlockSpec((tm, tn), lambda i, j: (i, j))
    grid = (m // tm, ndim // tn)
    params = pltpu.CompilerParams(dimension_semantics=("parallel", "parallel"),
                                  vmem_limit_bytes=V7X_VMEM_LIMIT_BYTES)
    if res is None:
        return pl.pallas_call(
            _mm_kernel, grid=grid, in_specs=[a_spec, w_spec], out_specs=o_spec,
            out_shape=jax.ShapeDtypeStruct((m, ndim), out_dtype),
            compiler_params=params, name=name)(a, w)
    return pl.pallas_call(
        functools.partial(_mm_res_kernel, scale=scale), grid=grid,
        in_specs=[a_spec, w_spec, o_spec], out_specs=o_spec,
        out_shape=jax.ShapeDtypeStruct((m, ndim), F32),
        compiler_params=params, name=name)(a, w, res)


def _matmul_swiglu(a, w, lead, *, tn=256, tm_cap=2064):
    m, k = a.shape
    f = w.shape[-1] // 2
    tm = _row_tile(m, tm_cap)
    assert f % tn == 0
    a_spec = pl.BlockSpec((tm, k), lambda i, j: (i, 0))
    wg_spec = _w_spec(w, lead, k, tn, 0, 0)
    wu_spec = _w_spec(w, lead, k, tn, 0, f // tn)
    return pl.pallas_call(
        _mm_swiglu_kernel, grid=(m // tm, f // tn),
        in_specs=[a_spec, wg_spec, wu_spec],
        out_specs=pl.BlockSpec((tm, tn), lambda i, j: (i, j)),
        out_shape=jax.ShapeDtypeStruct((m, f), BF16),
        compiler_params=pltpu.CompilerParams(dimension_semantics=("parallel", "parallel"),
                                             vmem_limit_bytes=V7X_VMEM_LIMIT_BYTES),
        name="ffn_in_swiglu")(a, w, w)


def _out_proj(a, w, lead, res, scale, name):
    k = a.shape[1]
    n = w.shape[-1]
    nsplit = 1
    while (k // nsplit) > 5632 or k % nsplit:
        nsplit += 1
    kdim = k // nsplit
    assert kdim % LANE == 0
    out = res
    for s in range(nsplit):
        out = _matmul(a, w, lead, a_k0=s * kdim, w_k0=s * kdim, kdim=kdim, n0=0, ndim=n,
                      tn=512 if kdim <= 4096 else 256, res=out, scale=scale, name=name)
    return out


def _ffn(x, prm, i, which):
    xn = _rmsnorm(x, prm['ffn_norm'][i, which][None])
    h = _matmul_swiglu(xn, prm['w_ffn_in'], (i, which))
    return _out_proj(h, prm['w_ffn_out'], (i, which), x, 0.5, "ffn_out")


def _split3(v):
    hi = v.astype(BF16)
    r1 = v - hi.astype(F32)
    mid = r1.astype(BF16)
    lo = (r1 - mid.astype(F32)).astype(BF16)
    return hi, mid, lo


def _expand(parts, e):
    out = jnp.dot(parts[0], e, preferred_element_type=F32)
    for p in parts[1:]:
        out = out + jnp.dot(p, e, preferred_element_type=F32)
    return out


def _ssd_kernel(xbc_ref, prev_ref, conv8_ref, z_ref, dtr_ref, h0_ref, cw_ref, cb_ref, dtb_ref, alog_ref, dx_ref,
                nw_ref, ex_ref, y_ref, hl_ref, buf_ref, h_ref, xdt_ref, xw_ref, yoff_ref, yg_ref, *, q, d_inner,
                groups, nstate, hd, valid):
    c = pl.program_id(1)
    gw = d_inner // groups
    hpg = gw // hd
    conv_dim = buf_ref.shape[1]

    ppg = gw // LANE

    @pl.when(c == 0)
    def _():
        for g in range(groups):
            for p in range(ppg):
                r0 = (g * ppg + p) * LANE
                h_ref[g, :, p * LANE:(p + 1) * LANE] = h0_ref[r0:r0 + LANE, :].T

    buf_ref[0:SUBLANE, :] = jnp.where(c == 0, conv8_ref[...], prev_ref[...])
    buf_ref[SUBLANE:SUBLANE + valid, :] = xbc_ref[...]
    if valid < q:
        buf_ref[SUBLANE + valid:SUBLANE + q, :] = jnp.zeros((q - valid, conv_dim), F32)
    nconv = cw_ref.shape[0]
    first = SUBLANE - (nconv - 1)
    cblk = 512
    for cbk in range(conv_dim // cblk):
        cols = slice(cbk * cblk, (cbk + 1) * cblk)
        s = buf_ref[pl.ds(first, q), cols] * cw_ref[0:1, cols]
        for k in range(1, nconv):
            s = s + buf_ref[pl.ds(first + k, q), cols] * cw_ref[k:k + 1, cols]
        buf_ref[SUBLANE:SUBLANE + q, cols] = jax.nn.silu(cb_ref[:, cols] + s)

    heads = dtr_ref.shape[1]
    row = lax.broadcasted_iota(jnp.int32, (q, heads), 0)
    dt = jax.nn.softplus(dtr_ref[...] + dtb_ref[...])
    if valid < q:
        dt = jnp.concatenate([dt, jnp.zeros((q - valid, heads), F32)], axis=0)
    cs = dt * (-jnp.exp(alog_ref[...]))
    sh = 1
    while sh < q:
        cs = cs + jnp.where(row >= sh, pltpu.roll(cs, sh, 0), 0.0)
        sh *= 2
    cs_t = cs.T
    cs_last = cs[q - 1:q, :]
    to_end = jnp.exp(cs_last - cs)
    dt_parts = _split3(dt)
    dtw_parts = _split3(dt * to_end)
    ecs_parts = _split3(jnp.exp(cs))
    dec_parts = _split3(jnp.broadcast_to(jnp.exp(cs_last), (SUBLANE, heads)))
    li = lax.broadcasted_iota(jnp.int32, (q, q), 0)
    si = lax.broadcasted_iota(jnp.int32, (q, q), 1)
    causal = li >= si
    lane = lax.broadcasted_iota(jnp.int32, (q, 2 * hd), 1)
    xs0 = SUBLANE
    for g in range(groups):
        gc = slice(g * gw, (g + 1) * gw)
        eg = ex_ref[:, gc]
        xs = buf_ref[xs0:xs0 + q, gc]
        xdt_ref[...] = (xs * _expand(dt_parts, eg)).astype(BF16)
        xw_ref[...] = (xs * _expand(dtw_parts, eg)).astype(BF16)
        bcol = d_inner + g * nstate
        ccol = d_inner + groups * nstate + g * nstate
        b_f = buf_ref[xs0:xs0 + q, bcol:bcol + nstate]
        bg = b_f.astype(BF16)
        bg_t = b_f.T.astype(BF16)
        cg = buf_ref[xs0:xs0 + q, ccol:ccol + nstate].astype(BF16)
        cbm = lax.dot_general(cg, bg, (((1,), (1,)), ((), ())), preferred_element_type=F32)
        h_t = h_ref[g]
        yoff_ref[...] = jnp.dot(cg, h_t.astype(BF16), preferred_element_type=F32) * _expand(ecs_parts, eg)
        st = jnp.dot(bg_t, xw_ref[...], preferred_element_type=F32)
        h_ref[g] = _expand(dec_parts, eg)[0:1, :] * h_t + st
        for p in range(hpg // 2):
            pc = slice(p * 2 * hd, (p + 1) * 2 * hd)
            ac = slice(g * gw + p * 2 * hd, g * gw + (p + 1) * 2 * hd)
            x_pair = xdt_ref[:, pc]
            ys = []
            for e in range(2):
                h = g * hpg + 2 * p + e
                seg = cs[:, h:h + 1] - cs_t[h:h + 1, :]
                m = (cbm * jnp.exp(jnp.where(causal, seg, NEG_INF))).astype(BF16)
                ys.append(jnp.dot(m, x_pair, preferred_element_type=F32))
            y = jnp.where(lane < hd, ys[0], ys[1]) + yoff_ref[:, pc]
            y = y + dx_ref[:, ac] * buf_ref[xs0:xs0 + q, ac]
            zz = z_ref[:, ac]
            yg_ref[0:valid, pc] = y[0:valid] * (zz * jax.nn.sigmoid(zz))
        yg = yg_ref[0:valid, :]
        yn = yg * lax.rsqrt(jnp.mean(yg * yg, axis=-1, keepdims=True) + NORM_EPS) * nw_ref[:, gc]
        y_ref[:, gc] = yn.astype(y_ref.dtype)

    @pl.when(c == pl.num_programs(1) - 1)
    def _():
        for g in range(groups):
            for p in range(ppg):
                r0 = (g * ppg + p) * LANE
                hl_ref[r0:r0 + LANE, :] = h_ref[g, :, p * LANE:(p + 1) * LANE].T


def _ssd_mix(xbc, z, dt_raw, row0, nb, nchunk, valid, conv_buf, h0, conv_w, conv_b, dt_bias, a_log, d_skip, norm_w,
             groups, nstate, hd):
    q = SSD_CHUNK
    assert valid == q or (nchunk == 1 and valid % SUBLANE == 0 and valid < q)
    conv_dim = xbc.shape[1]
    d_inner = z.shape[1]
    heads = dt_raw.shape[1]
    gw = d_inner // groups
    assert row0 % valid == 0 and conv_dim == d_inner + 2 * groups * nstate and heads * hd == d_inner
    assert gw % LANE == 0 and 2 * hd == LANE and nstate == LANE and conv_dim % 512 == 0
    blk0 = row0 // valid
    sub = valid // SUBLANE
    nconv = conv_w.shape[0]
    conv8 = jnp.pad(conv_buf, ((0, 0), (SUBLANE - (nconv - 1), 0), (0, 0)))
    expand = (jnp.arange(heads)[:, None] == jnp.arange(d_inner)[None, :] // hd).astype(BF16)
    dx = jnp.repeat(d_skip, hd)[None]
    row_map = lambda b, c: (blk0 + b * nchunk + c, 0)
    full2 = lambda b, c: (0, 0)
    kern = functools.partial(_ssd_kernel, q=q, d_inner=d_inner, groups=groups, nstate=nstate, hd=hd, valid=valid)
    y, hl = pl.pallas_call(
        kern, grid=(nb, nchunk),
        in_specs=[pl.BlockSpec((valid, conv_dim), row_map),
                  pl.BlockSpec((SUBLANE, conv_dim), lambda b, c: (jnp.maximum((blk0 + b * nchunk + c) * sub - 1, 0), 0)),
                  pl.BlockSpec((None, SUBLANE, conv_dim), lambda b, c: (b, 0, 0)),
                  pl.BlockSpec((valid, d_inner), row_map),
                  pl.BlockSpec((valid, heads), row_map),
                  pl.BlockSpec((None, heads * hd, nstate), lambda b, c: (b, 0, 0), pipeline_mode=pl.Buffered(1)),
                  pl.BlockSpec(conv_w.shape, full2),
                  pl.BlockSpec((1, conv_dim), full2),
                  pl.BlockSpec((1, heads), full2),
                  pl.BlockSpec((1, heads), full2),
                  pl.BlockSpec((1, d_inner), full2),
                  pl.BlockSpec((1, d_inner), full2),
                  pl.BlockSpec((heads, d_inner), full2, pipeline_mode=pl.Buffered(1))],
        out_specs=[pl.BlockSpec((valid, d_inner), lambda b, c: (b * nchunk + c, 0)),
                   pl.BlockSpec((None, heads * hd, nstate), lambda b, c: (b, 0, 0))],
        out_shape=[jax.ShapeDtypeStruct((nb * nchunk * valid, d_inner), BF16 if valid == q else F32),
                   jax.ShapeDtypeStruct((nb, heads * hd, nstate), F32)],
        scratch_shapes=[pltpu.VMEM((q + SUBLANE, conv_dim), F32),
                        pltpu.VMEM((groups, nstate, gw), F32),
                        pltpu.VMEM((q, gw), BF16), pltpu.VMEM((q, gw), BF16),
                        pltpu.VMEM((q, gw), F32), pltpu.VMEM((q, gw), F32)],
        compiler_params=pltpu.CompilerParams(dimension_semantics=("parallel", "arbitrary"),
                                             vmem_limit_bytes=V7X_VMEM_LIMIT_BYTES),
        name="ssd_mix",
    )(xbc, xbc, conv8, z, dt_raw, h0.reshape(nb, heads * hd, nstate), conv_w, conv_b[None], dt_bias[None],
      a_log[None], dx, norm_w[None], expand)
    return y, hl.reshape(nb, heads, hd, nstate)


LRU_SCAN_LANES = 512


def _gelu_tanh(x):
    return 0.5 * x * (1.0 + jnp.tanh(math.sqrt(2.0 / math.pi) * (x + 0.044715 * (x * x * x))))


def _lru_kernel(gate_ref, xb_ref, prev_ref, conv8_ref, h0_ref, cw_ref, cb_ref, wr_ref, br_ref, wi_ref, bi_ref,
                lam_ref, y_ref, hl_ref, e_ref, a_ref, u_ref, h_ref, *, tm, width):
    t = pl.program_id(1)

    @pl.when(t == 0)
    def _():
        h_ref[...] = h0_ref[...]

    e_ref[0:SUBLANE, :] = jnp.where(t == 0, conv8_ref[...], prev_ref[...])
    e_ref[SUBLANE:SUBLANE + tm, :] = xb_ref[...]
    nconv = cw_ref.shape[0]
    first = SUBLANE - (nconv - 1)
    nblk = wr_ref.shape[0]
    bd = width // nblk
    for k in range(nblk):
        cols = slice(k * bd, (k + 1) * bd)
        s = e_ref[pl.ds(first, tm), cols] * cw_ref[0:1, cols]
        for c in range(1, nconv):
            s = s + e_ref[pl.ds(first + c, tm), cols] * cw_ref[c:c + 1, cols]
        xk = cb_ref[:, cols] + s
        xkb = xk.astype(BF16)
        r = jax.nn.sigmoid(jnp.dot(xkb, wr_ref[k], preferred_element_type=F32) + br_ref[:, cols])
        ig = jax.nn.sigmoid(jnp.dot(xkb, wi_ref[k], preferred_element_type=F32) + bi_ref[:, cols])
        log_a = (-LRU_C * r) * jax.nn.softplus(-lam_ref[:, cols])
        a_ref[:, cols] = jnp.exp(log_a)
        one_minus_a2 = -jnp.tanh(log_a) * (jnp.exp(2.0 * log_a) + 1.0)
        u_ref[:, cols] = jnp.sqrt(one_minus_a2) * ig * xk

    row = lax.broadcasted_iota(jnp.int32, (SUBLANE, LRU_SCAN_LANES), 0)
    for sl in range(width // LRU_SCAN_LANES):
        cols = slice(sl * LRU_SCAN_LANES, (sl + 1) * LRU_SCAN_LANES)

        def body(r, h):
            rows = pl.ds(pl.multiple_of(r * SUBLANE, SUBLANE), SUBLANE)
            a = a_ref[rows, cols]
            u = u_ref[rows, cols]
            for sh in (1, 2, 4):
                a_sh = jnp.where(row >= sh, pltpu.roll(a, sh, 0), 1.0)
                u_sh = jnp.where(row >= sh, pltpu.roll(u, sh, 0), 0.0)
                u = a * u_sh + u
                a = a * a_sh
            hs = a * h + u
            y_ref[rows, cols] = hs * _gelu_tanh(gate_ref[rows, cols])
            return hs[SUBLANE - 1:SUBLANE, :]

        h_ref[:, cols] = lax.fori_loop(0, tm // SUBLANE, body, h_ref[:, cols])
    hl_ref[...] = h_ref[...]


def _lru_mix(gx, row0, nb, seq, conv_buf, h0, conv_w, conv_b, w_r, b_r, w_i, b_i, lam):
    width = gx.shape[1] // 2
    tm = _row_tile(seq, 128) if seq % 16 == 0 else seq
    assert seq % tm == 0 and tm % SUBLANE == 0 and row0 % tm == 0 and width % LRU_SCAN_LANES == 0
    nt = seq // tm
    nconv = conv_w.shape[0]
    conv8 = jnp.pad(conv_buf, ((0, 0), (SUBLANE - (nconv - 1), 0), (0, 0)))
    blk0 = row0 // tm
    sub = tm // SUBLANE

    def row_map(b, t):
        return (blk0 + b * nt + t, 0)

    def xb_map(b, t):
        return (blk0 + b * nt + t, 1)

    def prev_map(b, t):
        return (jnp.maximum((blk0 + b * nt + t) * sub - 1, 0), 1)

    full2 = lambda b, t: (0, 0)
    full3 = lambda b, t: (0, 0, 0)
    y, hl = pl.pallas_call(
        functools.partial(_lru_kernel, tm=tm, width=width),
        grid=(nb, nt),
        in_specs=[pl.BlockSpec((tm, width), row_map),
                  pl.BlockSpec((tm, width), xb_map),
                  pl.BlockSpec((SUBLANE, width), prev_map),
                  pl.BlockSpec((None, SUBLANE, width), lambda b, t: (b, 0, 0)),
                  pl.BlockSpec((None, 1, width), lambda b, t: (b, 0, 0)),
                  pl.BlockSpec(conv_w.shape, full2),
                  pl.BlockSpec((1, width), full2),
                  pl.BlockSpec(w_r.shape, full3),
                  pl.BlockSpec((1, width), full2),
                  pl.BlockSpec(w_i.shape, full3),
                  pl.BlockSpec((1, width), full2),
                  pl.BlockSpec((1, width), full2)],
        out_specs=[pl.BlockSpec((tm, width), lambda b, t: (b * nt + t, 0)),
                   pl.BlockSpec((None, 1, width), lambda b, t: (b, 0, 0))],
        out_shape=[jax.ShapeDtypeStruct((nb * seq, width), F32),
                   jax.ShapeDtypeStruct((nb, 1, width), F32)],
        scratch_shapes=[pltpu.VMEM((tm + SUBLANE, width), F32),
                        pltpu.VMEM((tm, width), F32),
                        pltpu.VMEM((tm, width), F32),
                        pltpu.VMEM((1, width), F32)],
        compiler_params=pltpu.CompilerParams(dimension_semantics=("parallel", "arbitrary"),
                                             vmem_limit_bytes=V7X_VMEM_LIMIT_BYTES),
        name="lru_mix",
    )(gx, gx, gx, conv8, h0[:, None, :], conv_w, conv_b[None], w_r.astype(BF16), b_r.reshape(1, width),
      w_i.astype(BF16), b_i.reshape(1, width), lam[None])
    return y, hl[:, 0]


KEY_CHUNK = 128
TINY = 1e-30
ATT_CHUNKS = 2


def _bucket_of(dist):
    n = jnp.maximum(dist, 0)
    exact = REL_BUCKETS // 2
    log_ratio = jnp.log(jnp.maximum(n, 1).astype(F32) / exact) / math.log(REL_MAX_DIST / exact)
    large = jnp.minimum(exact + (log_ratio * (REL_BUCKETS - exact)).astype(jnp.int32), REL_BUCKETS - 1)
    return jnp.where(n < exact, n, large)


def _bias_from_buckets(bucket, tab_ref, h):
    out = jnp.full(bucket.shape, tab_ref[0, h], F32)
    for k in range(1, REL_BUCKETS):
        out = jnp.where(bucket == k, tab_ref[k, h], out)
    return out


def _head_rmsnorm(x, w_row):
    return x * lax.rsqrt(jnp.mean(x * x, axis=-1, keepdims=True) + NORM_EPS) * w_row


def _bias_tiles_kernel(tab_ref, o_ref, *, tq):
    i = lax.broadcasted_iota(jnp.int32, (tq, KEY_CHUNK), 0)
    j = lax.broadcasted_iota(jnp.int32, (tq, KEY_CHUNK), 1)
    buckets = [_bucket_of(i - j + KEY_CHUNK * k) for k in range(2)]
    nheads = o_ref.shape[1] // tq

    def body(h, carry):
        rows = pl.ds(pl.multiple_of(h * tq, SUBLANE), tq)
        for k in range(2):
            o_ref[k, rows, :] = _bias_from_buckets(buckets[k], tab_ref, h)
        o_ref[2, rows, :] = jnp.full((tq, KEY_CHUNK), tab_ref[REL_BUCKETS - 1, h], F32)
        return carry

    lax.fori_loop(0, nheads, body, 0)


def _bias_tiles(table, tq):
    nheads = table.shape[1]
    return pl.pallas_call(
        functools.partial(_bias_tiles_kernel, tq=tq),
        in_specs=[pl.BlockSpec(memory_space=pltpu.SMEM)],
        out_specs=pl.BlockSpec(memory_space=pltpu.VMEM),
        out_shape=jax.ShapeDtypeStruct((3, nheads * tq, KEY_CHUNK), F32),
        name="nsa_bias_tiles",
    )(table)


PAGES_PER_STEP = 8


def _pages_per_step(npages):
    p = PAGES_PER_STEP
    while npages % p:
        p //= 2
    return p


def _slot_spec(src, lead, slot, groups, d, k, pps):
    gd = groups * d
    if src.shape[-2:] == (groups, d):
        nlead = len(lead)
        assert src.ndim == nlead + 5 and src.shape[-4] == KEY_CHUNK
        return pl.BlockSpec((None,) * (nlead + 1) + (KEY_CHUNK, None, groups, d),
                            lambda b, p, t: tuple(lead) + (t[b, p * pps + k], 0, slot, 0, 0))
    assert not lead
    if src.ndim == 2:
        return pl.BlockSpec((KEY_CHUNK, gd), lambda b, p, t: (t[b, p * pps + k], slot))
    assert src.ndim == 3 and src.shape[1] == KEY_CHUNK
    return pl.BlockSpec((None, KEY_CHUNK, gd), lambda b, p, t: (t[b, p * pps + k], 0, slot))


def _slab(ref, g, d):
    return ref[:, g, :] if len(ref.shape) == 3 else ref[:, g * d:(g + 1) * d]


def _compress_ab_kernel(tbl_ref, *refs, groups, d, pps):
    x_refs, (pe_ref, w1_ref, o_ref, slab_ref) = refs[:2 * pps], refs[2 * pps:]
    half = CMP_LEN // 2
    nchunk = KEY_CHUNK // CMP_STRIDE
    for k in range(pps):
        for s in range(2):
            for g in range(groups):
                slab_ref[k, s * groups + g] = _slab(x_refs[2 * k + s], g, d)
    rows_g = pps * nchunk
    for s in range(2):
        acc = [jnp.zeros((groups * rows_g, d), F32) for _ in range(2)]
        for l in range(half):
            x = jnp.concatenate(
                [slab_ref[k, s * groups + g, pl.ds(l, nchunk, stride=CMP_STRIDE), :]
                 for g in range(groups) for k in range(pps)], axis=0)
            for part in range(2):
                ll = part * half + l
                xa = (x + pe_ref[s, ll:ll + 1, :]).astype(BF16)
                acc[part] = acc[part] + jnp.dot(xa, w1_ref[s, ll], preferred_element_type=F32)
        for part in range(2):
            for g in range(groups):
                o_ref[s, part, g] = acc[part][g * rows_g:(g + 1) * rows_g]


def _compress_ab(src, lead, tbl, pe, w1b, groups, d):
    nb, npages = tbl.shape
    pps = _pages_per_step(npages)
    nchunk = KEY_CHUNK // CMP_STRIDE
    assert CMP_LEN == 2 * CMP_STRIDE
    grid_spec = pltpu.PrefetchScalarGridSpec(
        num_scalar_prefetch=1, grid=(nb, npages // pps),
        in_specs=[_slot_spec(src, lead, s, groups, d, k, pps) for k in range(pps) for s in range(2)]
        + [pl.BlockSpec(pe.shape, lambda b, p, t: (0, 0, 0)),
           pl.BlockSpec(w1b.shape, lambda b, p, t: (0, 0, 0, 0))],
        out_specs=pl.BlockSpec((None, 2, 2, groups, pps * nchunk, d), lambda b, p, t: (b, 0, 0, 0, p, 0)),
        scratch_shapes=[pltpu.VMEM((pps, 2 * groups, KEY_CHUNK, d), F32)])
    return pl.pallas_call(
        functools.partial(_compress_ab_kernel, groups=groups, d=d, pps=pps),
        grid_spec=grid_spec,
        out_shape=jax.ShapeDtypeStruct((nb, 2, 2, groups, npages * nchunk, d), F32),
        compiler_params=pltpu.CompilerParams(dimension_semantics=("parallel", "arbitrary"),
                                             vmem_limit_bytes=V7X_VMEM_LIMIT_BYTES),
        name="nsa_compress_ab",
    )(tbl, *([src] * (2 * pps)), pe, w1b)


def _compress_finish_kernel(ab_ref, w2_ref, kw_ref, o_ref, *, groups, d):
    n = ab_ref.shape[3]
    for s in range(2):
        for g in range(groups):
            first = ab_ref[s, 0, g]
            second = pltpu.roll(ab_ref[s, 1, g], n - 1, 0)
            hid = jax.nn.silu(first + second).astype(BF16)
            out = jnp.dot(hid, w2_ref[s], preferred_element_type=F32)
            if s == 0:
                out = _head_rmsnorm(out, kw_ref[...])
            o_ref[s, :, g * d:(g + 1) * d] = out.astype(o_ref.dtype)


def _compress_finish(ab, w2b, k_norm_row, groups, d):
    nb, _, _, _, n, _ = ab.shape
    return pl.pallas_call(
        functools.partial(_compress_finish_kernel, groups=groups, d=d),
        grid=(nb,),
        in_specs=[pl.BlockSpec((None, 2, 2, groups, n, d), lambda b: (b, 0, 0, 0, 0, 0)),
                  pl.BlockSpec(w2b.shape, lambda b: (0, 0, 0)),
                  pl.BlockSpec((1, d), lambda b: (0, 0))],
        out_specs=pl.BlockSpec((None, 2, n, groups * d), lambda b: (b, 0, 0, 0)),
        out_shape=jax.ShapeDtypeStruct((nb, 2, n, groups * d), BF16),
        compiler_params=pltpu.CompilerParams(dimension_semantics=("parallel",)),
        name="nsa_compress_finish",
    )(ab, w2b, k_norm_row)


def _select_kernel(tab_ref, q_ref, kcv_ref, qw_ref, ocmp_ref, msel_ref, *, tq, groups, hpg, d, q_pos0, n_cmp, n_sel,
                   nselp, scale):
    qi = pl.program_id(1)
    pos0 = q_pos0 + qi * tq
    nck = kcv_ref.shape[1]
    t_c = lax.broadcasted_iota(jnp.int32, (tq, nck), 0)
    n_c = lax.broadcasted_iota(jnp.int32, (tq, nck), 1)
    dist = pos0 + t_c - (n_c * CMP_STRIDE + CMP_LEN - 1)
    cmask = (dist >= 0) & (n_c < n_cmp)
    cmaskf = cmask.astype(F32)
    assert REL_MAX_DIST == LANE and nck % LANE == 0
    didx = jnp.clip(dist, 0, REL_MAX_DIST - 1)
    dist_buckets = _bucket_of(lax.broadcasted_iota(jnp.int32, (1, LANE), 1))

    def dist_bias(h):
        by_dist = jnp.broadcast_to(_bias_from_buckets(dist_buckets, tab_ref, h), (tq, LANE))
        return jnp.concatenate([jnp.take_along_axis(by_dist, didx[:, k * LANE:(k + 1) * LANE], axis=1)
                                for k in range(nck // LANE)], axis=1)

    n_r = lax.broadcasted_iota(jnp.int32, (nck, nselp), 0)
    j_r = lax.broadcasted_iota(jnp.int32, (nck, nselp), 1)
    cover = ((n_r * CMP_STRIDE < j_r * SEL_LEN + SEL_LEN) & (n_r * CMP_STRIDE + CMP_LEN > j_r * SEL_LEN)
             & (n_r < n_cmp) & (j_r < n_sel)).astype(BF16)
    j_s = lax.broadcasted_iota(jnp.int32, (tq, nselp), 1)
    pos = pos0 + lax.broadcasted_iota(jnp.int32, (tq, nselp), 0)
    cur = pos // SEL_LEN
    visible = (j_s * SEL_LEN <= pos) & (j_s < n_sel)
    forced = ((j_s == 0) | (j_s == cur) | (j_s == cur - 1)).astype(F32)
    cmask_all = jnp.concatenate([cmask] * hpg, axis=0)
    cmaskf_all = jnp.concatenate([cmaskf] * hpg, axis=0)
    for g in range(groups):
        kc = kcv_ref[0, :, g * d:(g + 1) * d]
        vc = kcv_ref[1, :, g * d:(g + 1) * d]
        heads = range(g * hpg, (g + 1) * hpg)
        qn = jnp.concatenate([_head_rmsnorm(q_ref[:, h * d:(h + 1) * d], qw_ref[...]).astype(BF16) for h in heads],
                             axis=0)
        bias = jnp.concatenate([dist_bias(h) for h in heads], axis=0)
        logits = lax.dot_general(qn, kc, (((1,), (1,)), ((), ())), preferred_element_type=F32) * scale
        logits = jnp.where(cmask_all, logits + bias, NEG_INF)
        m = jnp.max(logits, axis=-1, keepdims=True)
        p = jnp.exp(logits - m)
        p = p / jnp.sum(p, axis=-1, keepdims=True) * cmaskf_all
        o = jnp.dot(p.astype(BF16), vc, preferred_element_type=F32)
        p_grp = jnp.zeros((tq, nck), F32)
        for e, h in enumerate(heads):
            ocmp_ref[:, h * d:(h + 1) * d] = o[e * tq:(e + 1) * tq]
            p_grp = p_grp + p[e * tq:(e + 1) * tq]
        p_sel = jnp.dot(p_grp.astype(BF16), cover, preferred_element_type=F32)
        score = jnp.where(visible, p_sel + FORCE_BONUS * forced, NEG_INF)
        cnt = jnp.zeros((tq, nselp), jnp.int32)
        for jp in range(n_sel):
            col = score[:, jp:jp + 1]
            beats = (col > score) | ((col == score) & (j_s > jp))
            cnt = cnt + beats.astype(jnp.int32)
        msel_ref[:, g * nselp:(g + 1) * nselp] = ((cnt < SEL_TOPK) & visible).astype(F32)


def _nsa_select(table, qp, row0, kcv, q_norm_row, *, nb, seq, tq, q_pos0, n_cmp, n_sel, groups, hpg, d):
    nq = seq // tq
    n_q = groups * hpg * d
    nselp = -(-n_sel // LANE) * LANE
    nck = kcv.shape[2]
    assert row0 % tq == 0
    blk0 = row0 // tq
    kern = functools.partial(_select_kernel, tq=tq, groups=groups, hpg=hpg, d=d, q_pos0=q_pos0, n_cmp=n_cmp,
                             n_sel=n_sel, nselp=nselp, scale=d ** -0.5)
    return pl.pallas_call(
        kern, grid=(nb, nq),
        in_specs=[pl.BlockSpec(memory_space=pltpu.SMEM),
                  pl.BlockSpec((tq, n_q), lambda b, q: (blk0 + b * nq + q, 0)),
                  pl.BlockSpec((None, 2, nck, groups * d), lambda b, q: (b, 0, 0, 0)),
                  pl.BlockSpec((1, d), lambda b, q: (0, 0))],
        out_specs=[pl.BlockSpec((tq, n_q), lambda b, q: (b * nq + q, 0)),
                   pl.BlockSpec((tq, groups * nselp), lambda b, q: (b * nq + q, 0))],
        out_shape=[jax.ShapeDtypeStruct((nb * seq, n_q), F32),
                   jax.ShapeDtypeStruct((nb * seq, groups * nselp), F32)],
        compiler_params=pltpu.CompilerParams(dimension_semantics=("parallel", "parallel"),
                                             vmem_limit_bytes=V7X_VMEM_LIMIT_BYTES),
        name="nsa_cmp_select",
    )(table, qp, kcv, q_norm_row)


def _kv_pack_kernel(tbl_ref, *refs, groups, d, pps):
    x_refs, (kw_ref, k_ref, v_ref, slab_ref) = refs[:2 * pps], refs[2 * pps:]

    def dense(x_ref, i, g):
        if len(x_ref.shape) == 3:
            slab_ref[i * groups + g] = _slab(x_ref, g, d)
            return slab_ref[i * groups + g]
        return _slab(x_ref, g, d)

    for k in range(pps):
        rows = slice(k * KEY_CHUNK, (k + 1) * KEY_CHUNK)
        for g in range(groups):
            cols = slice(g * d, (g + 1) * d)
            k_ref[rows, cols] = _head_rmsnorm(dense(x_refs[2 * k], 0, g), kw_ref[...]).astype(k_ref.dtype)
            v_ref[rows, cols] = dense(x_refs[2 * k + 1], 1, g).astype(v_ref.dtype)


def _kv_pack(src, lead, tbl, k_slot, k_norm_row, groups, d):
    nb, npages = tbl.shape
    pps = _pages_per_step(npages)
    gd = groups * d
    grid_spec = pltpu.PrefetchScalarGridSpec(
        num_scalar_prefetch=1, grid=(nb, npages // pps),
        in_specs=[_slot_spec(src, lead, k_slot + s, groups, d, k, pps) for k in range(pps) for s in range(2)]
        + [pl.BlockSpec((1, d), lambda b, p, t: (0, 0))],
        out_specs=[pl.BlockSpec((None, pps * KEY_CHUNK, gd), lambda b, p, t: (b, p, 0)),
                   pl.BlockSpec((None, pps * KEY_CHUNK, gd), lambda b, p, t: (b, p, 0))],
        scratch_shapes=[pltpu.VMEM((2 * groups, KEY_CHUNK, d), F32)])
    return pl.pallas_call(
        functools.partial(_kv_pack_kernel, groups=groups, d=d, pps=pps),
        grid_spec=grid_spec,
        out_shape=[jax.ShapeDtypeStruct((nb, npages * KEY_CHUNK, gd), BF16)] * 2,
        compiler_params=pltpu.CompilerParams(dimension_semantics=("parallel", "parallel"),
                                             vmem_limit_bytes=V7X_VMEM_LIMIT_BYTES),
        name="nsa_kv_pack",
    )(tbl, *([src] * (2 * pps)), k_norm_row)


def _attend_kernel(tiles_ref, q_ref, ocmp_ref, gate_ref, msel_ref, ks_ref, vs_ref, kw_ref, vw_ref, qw_ref, o_ref,
                   qn_ref, ms_ref, ls_ref, as_ref, mw_ref, lw_ref, aw_ref, *, tq, groups, hpg, d, q_pos0,
                   win_chunk0, n_win_steps, nselp, scale):
    qi = pl.program_id(1)
    c = pl.program_id(2)
    pos0 = q_pos0 + qi * tq
    p_hi = (pos0 + tq - 1) // KEY_CHUNK // ATT_CHUNKS
    rows_g = hpg * tq
    nheads = groups * hpg

    @pl.when(c == 0)
    def _():
        for h in range(nheads):
            qn_ref[h * tq:(h + 1) * tq, :] = _head_rmsnorm(q_ref[:, h * d:(h + 1) * d], qw_ref[...]).astype(BF16)
        for m_ref, l_ref, a_ref in ((ms_ref, ls_ref, as_ref), (mw_ref, lw_ref, aw_ref)):
            m_ref[...] = jnp.full(m_ref.shape, NEG_INF, F32)
            l_ref[...] = jnp.zeros(l_ref.shape, F32)
            a_ref[...] = jnp.zeros(a_ref.shape, F32)

    assert tq & (tq - 1) == 0 and d == LANE
    t_idx = lax.broadcasted_iota(jnp.int32, (rows_g, KEY_CHUNK), 0) & (tq - 1)
    j_idx = lax.broadcasted_iota(jnp.int32, (rows_g, KEY_CHUNK), 1)
    ones_v = jnp.ones((KEY_CHUNK, d), BF16)

    def step(g, k_ref, v_ref, m_ref, l_ref, a_ref, chunk0, masks):
        rows = slice(g * rows_g, (g + 1) * rows_g)
        q = qn_ref[rows, :]
        logits = []
        for u, mask in enumerate(masks):
            delta = pos0 - (chunk0 + u) * KEY_CHUNK
            tile = jnp.clip(delta // KEY_CHUNK, 0, 2)
            k = k_ref[u * KEY_CHUNK:(u + 1) * KEY_CHUNK, g * d:(g + 1) * d]
            s = lax.dot_general(q, k, (((1,), (1,)), ((), ())), preferred_element_type=F32) * scale
            s = s + tiles_ref[tile, pl.ds(g * rows_g, rows_g), :]
            logits.append(jnp.where(mask, s, NEG_INF))
        m_prev = m_ref[rows, :]
        m_new = m_prev
        for s in logits:
            m_new = jnp.maximum(m_new, jnp.max(s, axis=-1, keepdims=True))
        alpha = jnp.exp(m_prev - m_new)
        pv = None
        for u, (s, mask) in enumerate(zip(logits, masks)):
            p = jnp.where(mask, jnp.exp(s - m_new), 0.0).astype(BF16)
            v_ext = jnp.concatenate([v_ref[u * KEY_CHUNK:(u + 1) * KEY_CHUNK, g * d:(g + 1) * d], ones_v], axis=1)
            part = jnp.dot(p, v_ext, preferred_element_type=F32)
            pv = part if pv is None else pv + part
        l_ref[rows, :] = alpha * l_ref[rows, :] + pv[:, d:]
        a_ref[rows, :] = alpha * a_ref[rows, :] + pv[:, :d]
        m_ref[rows, :] = m_new

    @pl.when(c <= p_hi)
    def _():
        chunk0 = c * ATT_CHUNKS
        blocks_per_chunk = KEY_CHUNK // SEL_LEN
        jr = lax.broadcasted_iota(jnp.int32, (nselp, KEY_CHUNK), 0)
        lr = lax.broadcasted_iota(jnp.int32, (nselp, KEY_CHUNK), 1)
        causal, expand = [], []
        for u in range(ATT_CHUNKS):
            causal.append(pos0 - (chunk0 + u) * KEY_CHUNK + t_idx - j_idx >= 0)
            expand.append((jr == (chunk0 + u) * blocks_per_chunk + lr // SEL_LEN).astype(BF16))
        for g in range(groups):
            ms = msel_ref[:, g * nselp:(g + 1) * nselp].astype(BF16)
            masks = []
            for u in range(ATT_CHUNKS):
                sel = jnp.dot(ms, expand[u], preferred_element_type=F32)
                masks.append(causal[u] & (jnp.concatenate([sel] * hpg, axis=0) > 0.5))
            step(g, ks_ref, vs_ref, ms_ref, ls_ref, as_ref, chunk0, masks)

    wpair = p_hi - c

    @pl.when((c < n_win_steps) & (wpair * ATT_CHUNKS >= win_chunk0))
    def _():
        chunk0 = wpair * ATT_CHUNKS
        masks = []
        for u in range(ATT_CHUNKS):
            dist = pos0 - (chunk0 + u) * KEY_CHUNK + t_idx - j_idx
            masks.append((dist >= 0) & (dist < WINDOW))
        for g in range(groups):
            step(g, kw_ref, vw_ref, mw_ref, lw_ref, aw_ref, chunk0, masks)

    @pl.when(c == pl.num_programs(2) - 1)
    def _():
        gates = jax.nn.sigmoid(gate_ref[...])
        for h in range(nheads):
            r = slice(h * tq, (h + 1) * tq)
            cols = slice(h * d, (h + 1) * d)
            o_sel = as_ref[r, :] / jnp.maximum(ls_ref[r, :], TINY)
            o_win = aw_ref[r, :] / jnp.maximum(lw_ref[r, :], TINY)
            o = (gates[:, 3 * h:3 * h + 1] * ocmp_ref[:, cols] + gates[:, 3 * h + 1:3 * h + 2] * o_sel
                 + gates[:, 3 * h + 2:3 * h + 3] * o_win)
            o_ref[:, cols] = o.astype(o_ref.dtype)


def _nsa_attend(tiles, qp, gp, row0, ocmp, msel, ksel, vsel, kwin, vwin, q_norm_row, *, nb, seq, tq, q_pos0,
                win_chunk0, groups, hpg, d):
    nq = seq // tq
    n_q = groups * hpg * d
    gd = groups * d
    nheads = groups * hpg
    nselp = msel.shape[1] // groups
    pair = ATT_CHUNKS * KEY_CHUNK
    assert ksel.shape[1] % pair == 0 and kwin.shape[1] % pair == 0 and win_chunk0 % ATT_CHUNKS == 0
    npair = ksel.shape[1] // pair
    nwpair = kwin.shape[1] // pair
    wpair0 = win_chunk0 // ATT_CHUNKS
    n_win_steps = (WINDOW // KEY_CHUNK + ATT_CHUNKS - 1) // ATT_CHUNKS + 1
    ncg = max(npair, n_win_steps)
    assert row0 % tq == 0 and q_pos0 % KEY_CHUNK == 0 and (tq == KEY_CHUNK or nq == 1) and tq <= KEY_CHUNK
    blk0 = row0 // tq

    def p_hi(q):
        return (q_pos0 + q * tq + tq - 1) // KEY_CHUNK // ATT_CHUNKS

    def sel_map(b, q, c):
        return (b, jnp.minimum(c, p_hi(q)), 0)

    def win_map(b, q, c):
        return (b, jnp.clip(p_hi(q) - jnp.minimum(c, n_win_steps - 1) - wpair0, 0, nwpair - 1), 0)

    grp_map = lambda b, q, c: (b * nq + q, 0)
    all_map = lambda b, q, c: (blk0 + b * nq + q, 0)
    kern = functools.partial(_attend_kernel, tq=tq, groups=groups, hpg=hpg, d=d, q_pos0=q_pos0,
                             win_chunk0=win_chunk0, n_win_steps=n_win_steps, nselp=nselp, scale=d ** -0.5)
    return pl.pallas_call(
        kern, grid=(nb, nq, ncg),
        in_specs=[pl.BlockSpec(tiles.shape, lambda b, q, c: (0, 0, 0)),
                  pl.BlockSpec((tq, n_q), all_map),
                  pl.BlockSpec((tq, n_q), grp_map),
                  pl.BlockSpec((tq, gp.shape[1]), all_map),
                  pl.BlockSpec((tq, groups * nselp), grp_map),
                  pl.BlockSpec((None, pair, gd), sel_map),
                  pl.BlockSpec((None, pair, gd), sel_map),
                  pl.BlockSpec((None, pair, gd), win_map),
                  pl.BlockSpec((None, pair, gd), win_map),
                  pl.BlockSpec((1, d), lambda b, q, c: (0, 0))],
        out_specs=pl.BlockSpec((tq, n_q), grp_map),
        out_shape=jax.ShapeDtypeStruct((nb * seq, n_q), BF16 if tq % (2 * SUBLANE) == 0 else F32),
        scratch_shapes=[pltpu.VMEM((nheads * tq, d), BF16),
                        pltpu.VMEM((nheads * tq, d), F32), pltpu.VMEM((nheads * tq, d), F32),
                        pltpu.VMEM((nheads * tq, d), F32),
                        pltpu.VMEM((nheads * tq, d), F32), pltpu.VMEM((nheads * tq, d), F32),
                        pltpu.VMEM((nheads * tq, d), F32)],
        compiler_params=pltpu.CompilerParams(dimension_semantics=("parallel", "parallel", "arbitrary"),
                                             vmem_limit_bytes=V7X_VMEM_LIMIT_BYTES),
        name="nsa_attend",
    )(tiles, qp, ocmp, gp, msel, ksel, vsel, kwin, vwin, q_norm_row)


def _nsa_group(qp, kvp, gp, row0, nb, seq, q_pos0, past, win_past, w, groups, hpg, d):
    q_norm_w, k_norm_w, cmp_pe, cmp_w1, cmp_w2, table = w
    gd = groups * d
    tq = KEY_CHUNK if seq % KEY_CHUNK == 0 else seq
    assert tq % SUBLANE == 0
    w1b, w2b = cmp_w1.astype(BF16), cmp_w2.astype(BF16)
    if past is None:
        assert seq % KEY_CHUNK == 0 and row0 % KEY_CHUNK == 0
        npg = seq // KEY_CHUNK
        tbl = row0 // KEY_CHUNK + jnp.arange(nb * npg, dtype=jnp.int32).reshape(nb, npg)
        src = kvp
        ab = _compress_ab(src, (), tbl, cmp_pe, w1b, groups, d)
        ksel, vsel = _kv_pack(src, (), tbl, 2, k_norm_w[1:2], groups, d)
        kwin, vwin = _kv_pack(src, (), tbl, 4, k_norm_w[2:3], groups, d)
        tk = seq
        win_chunk0 = 0
    else:
        cache, lead, tbl = past
        npg = tbl.shape[1]
        past_len = npg * KEY_CHUNK
        assert q_pos0 == past_len and seq < CMP_STRIDE and seq <= KEY_CHUNK
        ab = _compress_ab(cache, lead, tbl, cmp_pe, w1b, groups, d)
        ksel_p, vsel_p = _kv_pack(cache, lead, tbl, 2, k_norm_w[1:2], groups, d)
        wb = win_past.shape[1]
        assert wb % KEY_CHUNK == 0 and wb <= WINDOW and (past_len - wb) % KEY_CHUNK == 0
        nwp = wb // KEY_CHUNK
        wtbl = jnp.arange(nb * nwp, dtype=jnp.int32).reshape(nb, nwp)
        win_pages = win_past.reshape((nb * nwp, KEY_CHUNK) + win_past.shape[2:])
        kwin_p, vwin_p = _kv_pack(win_pages, (), wtbl, 0, k_norm_w[2:3], groups, d)
        new = kvp[row0:row0 + nb * seq].reshape(nb, seq, kvp.shape[1])
        new = jnp.pad(new, ((0, 0), (0, KEY_CHUNK - seq), (0, 0)))
        ntbl = jnp.arange(nb, dtype=jnp.int32).reshape(nb, 1)
        ksel_n, vsel_n = _kv_pack(new, (), ntbl, 2, k_norm_w[1:2], groups, d)
        kwin_n, vwin_n = _kv_pack(new, (), ntbl, 4, k_norm_w[2:3], groups, d)

        def join(past_part, new_part):
            rows = past_part.shape[1] + new_part.shape[1]
            fill = jnp.zeros((nb, -rows % (ATT_CHUNKS * KEY_CHUNK), gd), past_part.dtype)
            return jnp.concatenate([past_part, new_part, fill], axis=1)

        ksel, vsel = join(ksel_p, ksel_n), join(vsel_p, vsel_n)
        kwin, vwin = join(kwin_p, kwin_n), join(vwin_p, vwin_n)
        tk = past_len + seq
        win_chunk0 = (past_len - wb) // KEY_CHUNK
    n_cmp = (tk - CMP_LEN) // CMP_STRIDE + 1
    n_sel = -(-tk // SEL_LEN)
    assert n_cmp <= ab.shape[4] - 1
    kcv = _compress_finish(ab, w2b, k_norm_w[0:1], groups, d)
    ocmp, msel = _nsa_select(table, qp, row0, kcv, q_norm_w[None], nb=nb, seq=seq, tq=tq, q_pos0=q_pos0,
                             n_cmp=n_cmp, n_sel=n_sel, groups=groups, hpg=hpg, d=d)
    tiles = _bias_tiles(table, tq)
    return _nsa_attend(tiles, qp, gp, row0, ocmp, msel, ksel, vsel, kwin, vwin, q_norm_w[None], nb=nb, seq=seq,
                       tq=tq, q_pos0=q_pos0, win_chunk0=win_chunk0, groups=groups, hpg=hpg, d=d)


def kernel(x_prompt, x_sample, state_ssd, state_ssd_conv, state_lru, state_lru_conv, cache_nsa_kv, cache_nsa_win, page_table, ffn_norm, w_ffn_in, w_ffn_out, mix_norm, ssd_w_in, ssd_conv_w, ssd_conv_b, ssd_dt_bias, ssd_a_log, ssd_d, ssd_norm, ssd_w_out, lru_w_in, lru_conv_w, lru_conv_b, lru_w_r, lru_b_r, lru_w_i, lru_b_i, lru_lambda, lru_w_out, nsa_w_in, nsa_q_norm, nsa_k_norm, nsa_cmp_pe, nsa_cmp_w1, nsa_cmp_w2, rel_bias_table, nsa_w_out):
    prm = dict(ffn_norm=ffn_norm, w_ffn_in=w_ffn_in, w_ffn_out=w_ffn_out)
    bp, sp, dm = x_prompt.shape
    bs, ss, _ = x_sample.shape
    mp, ms = bp * sp, bs * ss
    depth = mix_norm.shape[0]
    G = NSA_GROUPS
    E = NSA_HEADS // G
    d_head = dm // NSA_HEADS
    n_pages = page_table.shape[1]
    page = cache_nsa_kv.shape[2]
    past_len = n_pages * page
    assert page == KEY_CHUNK

    def merge(tp, ts):
        return jnp.concatenate([tp.reshape(mp, -1), ts.reshape(ms, -1)], axis=0)

    def tail(buf, rows, prompt, c0, c1):
        n = buf.shape[1]
        nb_, seq, r0 = (bp, sp, 0) if prompt else (bs, ss, mp)
        k = min(n, seq)
        new = jnp.stack([rows[r0 + (b + 1) * seq - k:r0 + (b + 1) * seq, c0:c1] for b in range(nb_)])
        return jnp.concatenate([buf, new.reshape((nb_, k) + buf.shape[2:])], axis=1)[:, -n:]

    x = merge(x_prompt, x_sample)
    outs = {k: [] for k in ('ssd_h_p', 'ssd_h_s', 'ssd_buf_p', 'ssd_buf_s', 'lru_h_p', 'lru_h_s',
                            'lru_buf_p', 'lru_buf_s', 'rows_p', 'rows_s', 'win_p', 'win_s')}
    for i in range(depth):
        kind, j = i % N_MIXERS, i // N_MIXERS
        x = _ffn(x, prm, i, 0)
        hn = _rmsnorm(x, mix_norm[i][None])
        if kind == 0:
            d_inner = ssd_w_out.shape[1]
            conv_dim = ssd_conv_w.shape[2]
            heads = ssd_dt_bias.shape[1]
            z = _matmul(hn, ssd_w_in, (j,), kdim=dm, n0=0, ndim=d_inner, tn=512, name="ssd_in_z")
            xbc = _matmul(hn, ssd_w_in, (j,), kdim=dm, n0=d_inner, ndim=conv_dim, tn=512, name="ssd_in_xbc")
            dt_raw = _matmul(hn, ssd_w_in, (j,), kdim=dm, n0=d_inner + conv_dim, ndim=heads, tn=heads,
                             name="ssd_in_dt")
            w = (ssd_conv_w[j], ssd_conv_b[j], ssd_dt_bias[j], ssd_a_log[j], ssd_d[j], ssd_norm[j])
            q = SSD_CHUNK
            assert sp % q == 0
            nbuf = ssd_conv_w.shape[1] - 1
            hd, nstate = d_inner // heads, state_ssd.shape[-1]
            shape = (SSD_GROUPS, nstate, hd)
            zero_buf = jnp.zeros((bp, nbuf, conv_dim), F32)
            zero_h = jnp.zeros((bp,) + state_ssd.shape[2:], F32)
            y_p, st_p = _ssd_mix(xbc, z, dt_raw, 0, bp, sp // q, q, zero_buf, zero_h, *w, *shape)
            y_s, st_s = _ssd_mix(xbc, z, dt_raw, mp, bs, 1, ss, state_ssd_conv[j], state_ssd[j], *w, *shape)
            outs['ssd_h_p'].append(st_p); outs['ssd_h_s'].append(st_s)
            outs['ssd_buf_p'].append(tail(zero_buf, xbc, True, 0, conv_dim))
            outs['ssd_buf_s'].append(tail(state_ssd_conv[j], xbc, False, 0, conv_dim))
            x = _out_proj(jnp.concatenate([y_p, y_s.astype(BF16)], axis=0), ssd_w_out, (j,), x, 1.0, "ssd_out")
        elif kind == 1:
            width = lru_w_out.shape[1]
            nbuf = lru_conv_w.shape[1] - 1
            gx = _matmul(hn, lru_w_in, (j,), kdim=dm, n0=0, ndim=2 * width, tn=512, name="lru_in")
            w = (lru_conv_w[j], lru_conv_b[j], lru_w_r[j], lru_b_r[j], lru_w_i[j], lru_b_i[j], lru_lambda[j])
            zero_buf = jnp.zeros((bp, nbuf, width), F32)
            y_p, st_p = _lru_mix(gx, 0, bp, sp, zero_buf, jnp.zeros((bp, width), F32), *w)
            y_s, st_s = _lru_mix(gx, mp, bs, ss, state_lru_conv[j], state_lru[j], *w)
            outs['lru_h_p'].append(st_p); outs['lru_h_s'].append(st_s)
            outs['lru_buf_p'].append(tail(zero_buf, gx, True, width, 2 * width))
            outs['lru_buf_s'].append(tail(state_lru_conv[j], gx, False, width, 2 * width))
            x = _out_proj(jnp.concatenate([y_p, y_s], axis=0).astype(BF16), lru_w_out, (j,), x, 1.0, "lru_out")
        else:
            n_q, n_rows, n_kv = NSA_HEADS * d_head, 4 * G * d_head, 6 * G * d_head
            qp = _matmul(hn, nsa_w_in, (j,), kdim=dm, n0=0, ndim=n_q, tn=512, name="nsa_in_q")
            kvp = _matmul(hn, nsa_w_in, (j,), kdim=dm, n0=n_q, ndim=n_kv, tn=512, name="nsa_in_kv")
            n_gate = nsa_w_in.shape[2] - n_q - n_kv
            w_gate = jnp.pad(nsa_w_in[j][None, :, n_q + n_kv:], ((0, 0), (0, 0), (0, LANE - n_gate)))
            gp = _matmul(hn, w_gate, (0,), kdim=dm, n0=0, ndim=LANE, tn=LANE, name="nsa_in_gate")[:, :n_gate]
            w = (nsa_q_norm[j], nsa_k_norm[j], nsa_cmp_pe[j], nsa_cmp_w1[j], nsa_cmp_w2[j], rel_bias_table)
            o_p = _nsa_group(qp, kvp, gp, 0, bp, sp, 0, None, None, w, G, E, d_head)
            o_s = _nsa_group(qp, kvp, gp, mp, bs, ss, past_len, (cache_nsa_kv, (j,), page_table), cache_nsa_win[j],
                             w, G, E, d_head)
            outs['rows_p'].append(kvp[:mp, :n_rows].reshape(bp, sp, 4, G, d_head))
            outs['rows_s'].append(kvp[mp:, :n_rows].reshape(bs, ss, 4, G, d_head))
            outs['win_p'].append(tail(jnp.zeros((bp,) + cache_nsa_win.shape[2:], F32), kvp, True, n_rows, n_kv))
            outs['win_s'].append(tail(cache_nsa_win[j], kvp, False, n_rows, n_kv))
            x = _out_proj(jnp.concatenate([o_p.astype(BF16), o_s.astype(BF16)], axis=0), nsa_w_out, (j,), x, 1.0,
                          "nsa_out")
        x = _ffn(x, prm, i, 1)
    y_prompt, y_sample = x[:mp].reshape(bp, sp, dm), x[mp:].reshape(bs, ss, dm)
    st = {k: jnp.stack(v) for k, v in outs.items()}
    return (y_prompt, y_sample, st['ssd_h_p'], st['ssd_h_s'], st['ssd_buf_p'], st['ssd_buf_s'],
            st['lru_h_p'], st['lru_h_s'], st['lru_buf_p'], st['lru_buf_s'],
            st['rows_p'], st['rows_s'], st['win_p'], st['win_s'])
```

```python
import functools
import math

import jax
import jax.numpy as jnp
from jax import lax
from jax.experimental import pallas as pl
from jax.experimental.pallas import tpu as pltpu

F32 = jnp.float32
BF16 = jnp.bfloat16

NORM_EPS = 1e-6
N_MIXERS = 3
SSD_HEAD_DIM = 64
SSD_STATE = 128
SSD_GROUPS = 8
SSD_CHUNK = 128
LRU_BLOCKS = 16
LRU_C = 8.0
NSA_HEADS = 32
NSA_GROUPS = 4
CMP_LEN = 32
CMP_STRIDE = 16
SEL_LEN = 64
SEL_TOPK = 16
WINDOW = 512
SEL_QBLOCK = 64
WIN_QBLOCK = 128
FORCE_BONUS = 1e4
NEG_INF = -1e30
REL_BUCKETS = 32
REL_MAX_DIST = 128

V7X_VMEM_LIMIT_BYTES = 56 * 1024 * 1024
LANE = 128
SUBLANE = 8


def _row_tile(m, cap):
    best = None
    for t in range(16, min(m, cap) + 1, 16):
        if m % t == 0:
            best = t
    assert best is not None, m
    return best


def _rmsnorm_kernel(x_ref, w_ref, o_ref):
    x = x_ref[...]
    y = x * lax.rsqrt(jnp.mean(x * x, axis=-1, keepdims=True) + NORM_EPS)
    o_ref[...] = (y * w_ref[...]).astype(o_ref.dtype)


def _rmsnorm(x, w_row):
    m, d = x.shape
    tm = _row_tile(m, 704)
    return pl.pallas_call(
        _rmsnorm_kernel,
        grid=(m // tm,),
        in_specs=[pl.BlockSpec((tm, d), lambda i: (i, 0)),
                  pl.BlockSpec((1, d), lambda i: (0, 0))],
        out_specs=pl.BlockSpec((tm, d), lambda i: (i, 0)),
        out_shape=jax.ShapeDtypeStruct((m, d), BF16),
        compiler_params=pltpu.CompilerParams(dimension_semantics=("parallel",),
                                             vmem_limit_bytes=V7X_VMEM_LIMIT_BYTES),
        name="rmsnorm",
    )(x, w_row)


def _mm_kernel(a_ref, w_ref, o_ref):
    acc = jnp.dot(a_ref[...], w_ref[...].astype(BF16), preferred_element_type=F32)
    o_ref[...] = acc.astype(o_ref.dtype)


def _mm_res_kernel(a_ref, w_ref, r_ref, o_ref, *, scale):
    acc = jnp.dot(a_ref[...], w_ref[...].astype(BF16), preferred_element_type=F32)
    o_ref[...] = r_ref[...] + scale * acc


def _mm_swiglu_kernel(a_ref, wg_ref, wu_ref, o_ref):
    a = a_ref[...]
    g = jnp.dot(a, wg_ref[...].astype(BF16), preferred_element_type=F32)
    u = jnp.dot(a, wu_ref[...].astype(BF16), preferred_element_type=F32)
    o_ref[...] = (jax.nn.silu(g) * u).astype(o_ref.dtype)


def _w_spec(w, lead, kdim, tn, k_blk, n_blk0):
    nlead = len(lead)
    return pl.BlockSpec((None,) * nlead + (kdim, tn),
                        lambda i, j: tuple(lead) + (k_blk, n_blk0 + j))


def _matmul(a, w, lead, *, a_k0=0, w_k0=0, kdim, n0, ndim, tn, tm_cap=1408,
            res=None, scale=1.0, out_dtype=F32, name="matmul"):
    m = a.shape[0]
    tm = _row_tile(m, tm_cap)
    assert a_k0 % kdim == 0 and w_k0 % kdim == 0 and n0 % tn == 0 and ndim % tn == 0
    a_spec = pl.BlockSpec((tm, kdim), lambda i, j: (i, a_k0 // kdim))
    w_spec = _w_spec(w, lead, kdim, tn, w_k0 // kdim, n0 // tn)
    o_spec = pl.BlockSpec((tm, tn), lambda i, j: (i, j))
    grid = (m // tm, ndim // tn)
    params = pltpu.CompilerParams(dimension_semantics=("parallel", "parallel"),
                                  vmem_limit_bytes=V7X_VMEM_LIMIT_BYTES)
    if res is None:
        return pl.pallas_call(
            _mm_kernel, grid=grid, in_specs=[a_spec, w_spec], out_specs=o_spec,
            out_shape=jax.ShapeDtypeStruct((m, ndim), out_dtype),
            compiler_params=params, name=name)(a, w)
    return pl.pallas_call(
        functools.partial(_mm_res_kernel, scale=scale), grid=grid,
        in_specs=[a_spec, w_spec, o_spec], out_specs=o_spec,
        out_shape=jax.ShapeDtypeStruct((m, ndim), F32),
        compiler_params=params, name=name)(a, w, res)


def _matmul_swiglu(a, w, lead, *, tn=256, tm_cap=2064):
    m, k = a.shape
    f = w.shape[-1] // 2
    tm = _row_tile(m, tm_cap)
    assert f % tn == 0
    a_spec = pl.BlockSpec((tm, k), lambda i, j: (i, 0))
    wg_spec = _w_spec(w, lead, k, tn, 0, 0)
    wu_spec = _w_spec(w, lead, k, tn, 0, f // tn)
    return pl.pallas_call(
        _mm_swiglu_kernel, grid=(m // tm, f // tn),
        in_specs=[a_spec, wg_spec, wu_spec],
        out_specs=pl.BlockSpec((tm, tn), lambda i, j: (i, j)),
        out_shape=jax.ShapeDtypeStruct((m, f), BF16),
        compiler_params=pltpu.CompilerParams(dimension_semantics=("parallel", "parallel"),
                                             vmem_limit_bytes=V7X_VMEM_LIMIT_BYTES),
        name="ffn_in_swiglu")(a, w, w)


def _out_proj(a, w, lead, res, scale, name):
    k = a.shape[1]
    n = w.shape[-1]
    nsplit = 1
    while (k // nsplit) > 5632 or k % nsplit:
        nsplit += 1
    kdim = k // nsplit
    assert kdim % LANE == 0
    out = res
    for s in range(nsplit):
        out = _matmul(a, w, lead, a_k0=s * kdim, w_k0=s * kdim, kdim=kdim, n0=0, ndim=n,
                      tn=512 if kdim <= 4096 else 256, res=out, scale=scale, name=name)
    return out


def _ffn(x, prm, i, which):
    xn = _rmsnorm(x, prm['ffn_norm'][i, which][None])
    h = _matmul_swiglu(xn, prm['w_ffn_in'], (i, which))
    return _out_proj(h, prm['w_ffn_out'], (i, which), x, 0.5, "ffn_out")


def _split3(v):
    hi = v.astype(BF16)
    r1 = v - hi.astype(F32)
    mid = r1.astype(BF16)
    lo = (r1 - mid.astype(F32)).astype(BF16)
    return hi, mid, lo


def _expand(parts, e):
    out = jnp.dot(parts[0], e, preferred_element_type=F32)
    for p in parts[1:]:
        out = out + jnp.dot(p, e, preferred_element_type=F32)
    return out


def _ssd_kernel(xbc_ref, prev_ref, conv8_ref, z_ref, dtr_ref, h0_ref, cw_ref, cb_ref, dtb_ref, alog_ref, dx_ref,
                nw_ref, ex_ref, y_ref, hl_ref, buf_ref, h_ref, xdt_ref, xw_ref, yoff_ref, yg_ref, *, q, d_inner,
                groups, nstate, hd, valid):
    c = pl.program_id(1)
    gw = d_inner // groups
    hpg = gw // hd
    conv_dim = buf_ref.shape[1]

    ppg = gw // LANE

    @pl.when(c == 0)
    def _():
        for g in range(groups):
            for p in range(ppg):
                r0 = (g * ppg + p) * LANE
                h_ref[g, :, p * LANE:(p + 1) * LANE] = h0_ref[r0:r0 + LANE, :].T

    buf_ref[0:SUBLANE, :] = jnp.where(c == 0, conv8_ref[...], prev_ref[...])
    buf_ref[SUBLANE:SUBLANE + valid, :] = xbc_ref[...]
    if valid < q:
        buf_ref[SUBLANE + valid:SUBLANE + q, :] = jnp.zeros((q - valid, conv_dim), F32)
    nconv = cw_ref.shape[0]
    first = SUBLANE - (nconv - 1)
    cblk = 512
    for cbk in range(conv_dim // cblk):
        cols = slice(cbk * cblk, (cbk + 1) * cblk)
        s = buf_ref[pl.ds(first, q), cols] * cw_ref[0:1, cols]
        for k in range(1, nconv):
            s = s + buf_ref[pl.ds(first + k, q), cols] * cw_ref[k:k + 1, cols]
        buf_ref[SUBLANE:SUBLANE + q, cols] = jax.nn.silu(cb_ref[:, cols] + s)

    heads = dtr_ref.shape[1]
    row = lax.broadcasted_iota(jnp.int32, (q, heads), 0)
    dt = jax.nn.softplus(dtr_ref[...] + dtb_ref[...])
    if valid < q:
        dt = jnp.concatenate([dt, jnp.zeros((q - valid, heads), F32)], axis=0)
    cs = dt * (-jnp.exp(alog_ref[...]))
    sh = 1
    while sh < q:
        cs = cs + jnp.where(row >= sh, pltpu.roll(cs, sh, 0), 0.0)
        sh *= 2
    cs_t = cs.T
    cs_last = cs[q - 1:q, :]
    to_end = jnp.exp(cs_last - cs)
    dt_parts = _split3(dt)
    dtw_parts = _split3(dt * to_end)
    ecs_parts = _split3(jnp.exp(cs))
    dec_parts = _split3(jnp.broadcast_to(jnp.exp(cs_last), (SUBLANE, heads)))
    li = lax.broadcasted_iota(jnp.int32, (q, q), 0)
    si = lax.broadcasted_iota(jnp.int32, (q, q), 1)
    causal = li >= si
    lane = lax.broadcasted_iota(jnp.int32, (q, 2 * hd), 1)
    xs0 = SUBLANE
    for g in range(groups):
        gc = slice(g * gw, (g + 1) * gw)
        eg = ex_ref[:, gc]
        xs = buf_ref[xs0:xs0 + q, gc]
        xdt_ref[...] = (xs * _expand(dt_parts, eg)).astype(BF16)
        xw_ref[...] = (xs * _expand(dtw_parts, eg)).astype(BF16)
        bcol = d_inner + g * nstate
        ccol = d_inner + groups * nstate + g * nstate
        b_f = buf_ref[xs0:xs0 + q, bcol:bcol + nstate]
        bg = b_f.astype(BF16)
        bg_t = b_f.T.astype(BF16)
        cg = buf_ref[xs0:xs0 + q, ccol:ccol + nstate].astype(BF16)
        cbm = lax.dot_general(cg, bg, (((1,), (1,)), ((), ())), preferred_element_type=F32)
        h_t = h_ref[g]
        yoff_ref[...] = jnp.dot(cg, h_t.astype(BF16), preferred_element_type=F32) * _expand(ecs_parts, eg)
        st = jnp.dot(bg_t, xw_ref[...], preferred_element_type=F32)
        h_ref[g] = _expand(dec_parts, eg)[0:1, :] * h_t + st
        for p in range(hpg // 2):
            pc = slice(p * 2 * hd, (p + 1) * 2 * hd)
            ac = slice(g * gw + p * 2 * hd, g * gw + (p + 1) * 2 * hd)
            x_pair = xdt_ref[:, pc]
            ys = []
            for e in range(2):
                h = g * hpg + 2 * p + e
                seg = cs[:, h:h + 1] - cs_t[h:h + 1, :]
                m = (cbm * jnp.exp(jnp.where(causal, seg, NEG_INF))).astype(BF16)
                ys.append(jnp.dot(m, x_pair, preferred_element_type=F32))
            y = jnp.where(lane < hd, ys[0], ys[1]) + yoff_ref[:, pc]
            y = y + dx_ref[:, ac] * buf_ref[xs0:xs0 + q, ac]
            zz = z_ref[:, ac]
            yg_ref[0:valid, pc] = y[0:valid] * (zz * jax.nn.sigmoid(zz))
        yg = yg_ref[0:valid, :]
        yn = yg * lax.rsqrt(jnp.mean(yg * yg, axis=-1, keepdims=True) + NORM_EPS) * nw_ref[:, gc]
        y_ref[:, gc] = yn.astype(y_ref.dtype)

    @pl.when(c == pl.num_programs(1) - 1)
    def _():
        for g in range(groups):
            for p in range(ppg):
                r0 = (g * ppg + p) * LANE
                hl_ref[r0:r0 + LANE, :] = h_ref[g, :, p * LANE:(p + 1) * LANE].T


def _ssd_mix(xbc, z, dt_raw, row0, nb, nchunk, valid, conv_buf, h0, conv_w, conv_b, dt_bias, a_log, d_skip, norm_w,
             groups, nstate, hd):
    q = SSD_CHUNK
    assert valid == q or (nchunk == 1 and valid % SUBLANE == 0 and valid < q)
    conv_dim = xbc.shape[1]
    d_inner = z.shape[1]
    heads = dt_raw.shape[1]
    gw = d_inner // groups
    assert row0 % valid == 0 and conv_dim == d_inner + 2 * groups * nstate and heads * hd == d_inner
    assert gw % LANE == 0 and 2 * hd == LANE and nstate == LANE and conv_dim % 512 == 0
    blk0 = row0 // valid
    sub = valid // SUBLANE
    nconv = conv_w.shape[0]
    conv8 = jnp.pad(conv_buf, ((0, 0), (SUBLANE - (nconv - 1), 0), (0, 0)))
    expand = (jnp.arange(heads)[:, None] == jnp.arange(d_inner)[None, :] // hd).astype(BF16)
    dx = jnp.repeat(d_skip, hd)[None]
    row_map = lambda b, c: (blk0 + b * nchunk + c, 0)
    full2 = lambda b, c: (0, 0)
    kern = functools.partial(_ssd_kernel, q=q, d_inner=d_inner, groups=groups, nstate=nstate, hd=hd, valid=valid)
    y, hl = pl.pallas_call(
        kern, grid=(nb, nchunk),
        in_specs=[pl.BlockSpec((valid, conv_dim), row_map),
                  pl.BlockSpec((SUBLANE, conv_dim), lambda b, c: (jnp.maximum((blk0 + b * nchunk + c) * sub - 1, 0), 0)),
                  pl.BlockSpec((None, SUBLANE, conv_dim), lambda b, c: (b, 0, 0)),
                  pl.BlockSpec((valid, d_inner), row_map),
                  pl.BlockSpec((valid, heads), row_map),
                  pl.BlockSpec((None, heads * hd, nstate), lambda b, c: (b, 0, 0), pipeline_mode=pl.Buffered(1)),
                  pl.BlockSpec(conv_w.shape, full2),
                  pl.BlockSpec((1, conv_dim), full2),
                  pl.BlockSpec((1, heads), full2),
                  pl.BlockSpec((1, heads), full2),
                  pl.BlockSpec((1, d_inner), full2),
                  pl.BlockSpec((1, d_inner), full2),
                  pl.BlockSpec((heads, d_inner), full2, pipeline_mode=pl.Buffered(1))],
        out_specs=[pl.BlockSpec((valid, d_inner), lambda b, c: (b * nchunk + c, 0)),
                   pl.BlockSpec((None, heads * hd, nstate), lambda b, c: (b, 0, 0))],
        out_shape=[jax.ShapeDtypeStruct((nb * nchunk * valid, d_inner), BF16 if valid == q else F32),
                   jax.ShapeDtypeStruct((nb, heads * hd, nstate), F32)],
        scratch_shapes=[pltpu.VMEM((q + SUBLANE, conv_dim), F32),
                        pltpu.VMEM((groups, nstate, gw), F32),
                        pltpu.VMEM((q, gw), BF16), pltpu.VMEM((q, gw), BF16),
                        pltpu.VMEM((q, gw), F32), pltpu.VMEM((q, gw), F32)],
        compiler_params=pltpu.CompilerParams(dimension_semantics=("parallel", "arbitrary"),
                                             vmem_limit_bytes=V7X_VMEM_LIMIT_BYTES),
        name="ssd_mix",
    )(xbc, xbc, conv8, z, dt_raw, h0.reshape(nb, heads * hd, nstate), conv_w, conv_b[None], dt_bias[None],
      a_log[None], dx, norm_w[None], expand)
    return y, hl.reshape(nb, heads, hd, nstate)


LRU_SCAN_LANES = 512


def _gelu_tanh(x):
    return 0.5 * x * (1.0 + jnp.tanh(math.sqrt(2.0 / math.pi) * (x + 0.044715 * (x * x * x))))


def _lru_kernel(gate_ref, xb_ref, prev_ref, conv8_ref, h0_ref, cw_ref, cb_ref, wr_ref, br_ref, wi_ref, bi_ref,
                lam_ref, y_ref, hl_ref, e_ref, a_ref, u_ref, h_ref, *, tm, width):
    t = pl.program_id(1)

    @pl.when(t == 0)
    def _():
        h_ref[...] = h0_ref[...]

    e_ref[0:SUBLANE, :] = jnp.where(t == 0, conv8_ref[...], prev_ref[...])
    e_ref[SUBLANE:SUBLANE + tm, :] = xb_ref[...]
    nconv = cw_ref.shape[0]
    first = SUBLANE - (nconv - 1)
    nblk = wr_ref.shape[0]
    bd = width // nblk
    for k in range(nblk):
        cols = slice(k * bd, (k + 1) * bd)
        s = e_ref[pl.ds(first, tm), cols] * cw_ref[0:1, cols]
        for c in range(1, nconv):
            s = s + e_ref[pl.ds(first + c, tm), cols] * cw_ref[c:c + 1, cols]
        xk = cb_ref[:, cols] + s
        xkb = xk.astype(BF16)
        r = jax.nn.sigmoid(jnp.dot(xkb, wr_ref[k], preferred_element_type=F32) + br_ref[:, cols])
        ig = jax.nn.sigmoid(jnp.dot(xkb, wi_ref[k], preferred_element_type=F32) + bi_ref[:, cols])
        log_a = (-LRU_C * r) * jax.nn.softplus(-lam_ref[:, cols])
        a_ref[:, cols] = jnp.exp(log_a)
        one_minus_a2 = -jnp.tanh(log_a) * (jnp.exp(2.0 * log_a) + 1.0)
        u_ref[:, cols] = jnp.sqrt(one_minus_a2) * ig * xk

    row = lax.broadcasted_iota(jnp.int32, (SUBLANE, LRU_SCAN_LANES), 0)
    for sl in range(width // LRU_SCAN_LANES):
        cols = slice(sl * LRU_SCAN_LANES, (sl + 1) * LRU_SCAN_LANES)

        def body(r, h):
            rows = pl.ds(pl.multiple_of(r * SUBLANE, SUBLANE), SUBLANE)
            a = a_ref[rows, cols]
            u = u_ref[rows, cols]
            for sh in (1, 2, 4):
                a_sh = jnp.where(row >= sh, pltpu.roll(a, sh, 0), 1.0)
                u_sh = jnp.where(row >= sh, pltpu.roll(u, sh, 0), 0.0)
                u = a * u_sh + u
                a = a * a_sh
            hs = a * h + u
            y_ref[rows, cols] = hs * _gelu_tanh(gate_ref[rows, cols])
            return hs[SUBLANE - 1:SUBLANE, :]

        h_ref[:, cols] = lax.fori_loop(0, tm // SUBLANE, body, h_ref[:, cols])
    hl_ref[...] = h_ref[...]


def _lru_mix(gx, row0, nb, seq, conv_buf, h0, conv_w, conv_b, w_r, b_r, w_i, b_i, lam):
    width = gx.shape[1] // 2
    tm = _row_tile(seq, 128) if seq % 16 == 0 else seq
    assert seq % tm == 0 and tm % SUBLANE == 0 and row0 % tm == 0 and width % LRU_SCAN_LANES == 0
    nt = seq // tm
    nconv = conv_w.shape[0]
    conv8 = jnp.pad(conv_buf, ((0, 0), (SUBLANE - (nconv - 1), 0), (0, 0)))
    blk0 = row0 // tm
    sub = tm // SUBLANE

    def row_map(b, t):
        return (blk0 + b * nt + t, 0)

    def xb_map(b, t):
        return (blk0 + b * nt + t, 1)

    def prev_map(b, t):
        return (jnp.maximum((blk0 + b * nt + t) * sub - 1, 0), 1)

    full2 = lambda b, t: (0, 0)
    full3 = lambda b, t: (0, 0, 0)
    y, hl = pl.pallas_call(
        functools.partial(_lru_kernel, tm=tm, width=width),
        grid=(nb, nt),
        in_specs=[pl.BlockSpec((tm, width), row_map),
                  pl.BlockSpec((tm, width), xb_map),
                  pl.BlockSpec((SUBLANE, width), prev_map),
                  pl.BlockSpec((None, SUBLANE, width), lambda b, t: (b, 0, 0)),
                  pl.BlockSpec((None, 1, width), lambda b, t: (b, 0, 0)),
                  pl.BlockSpec(conv_w.shape, full2),
                  pl.BlockSpec((1, width), full2),
                  pl.BlockSpec(w_r.shape, full3),
                  pl.BlockSpec((1, width), full2),
                  pl.BlockSpec(w_i.shape, full3),
                  pl.BlockSpec((1, width), full2),
                  pl.BlockSpec((1, width), full2)],
        out_specs=[pl.BlockSpec((tm, width), lambda b, t: (b * nt + t, 0)),
                   pl.BlockSpec((None, 1, width), lambda b, t: (b, 0, 0))],
        out_shape=[jax.ShapeDtypeStruct((nb * seq, width), F32),
                   jax.ShapeDtypeStruct((nb, 1, width), F32)],
        scratch_shapes=[pltpu.VMEM((tm + SUBLANE, width), F32),
                        pltpu.VMEM((tm, width), F32),
                        pltpu.VMEM((tm, width), F32),
                        pltpu.VMEM((1, width), F32)],
        compiler_params=pltpu.CompilerParams(dimension_semantics=("parallel", "arbitrary"),
                                             vmem_limit_bytes=V7X_VMEM_LIMIT_BYTES),
        name="lru_mix",
    )(gx, gx, gx, conv8, h0[:, None, :], conv_w, conv_b[None], w_r.astype(BF16), b_r.reshape(1, width),
      w_i.astype(BF16), b_i.reshape(1, width), lam[None])
    return y, hl[:, 0]


KEY_CHUNK = 128
TINY = 1e-30
ATT_CHUNKS = 2


def _bucket_of(dist):
    n = jnp.maximum(dist, 0)
    exact = REL_BUCKETS // 2
    log_ratio = jnp.log(jnp.maximum(n, 1).astype(F32) / exact) / math.log(REL_MAX_DIST / exact)
    large = jnp.minimum(exact + (log_ratio * (REL_BUCKETS - exact)).astype(jnp.int32), REL_BUCKETS - 1)
    return jnp.where(n < exact, n, large)


def _bias_from_buckets(bucket, tab_ref, h):
    out = jnp.full(bucket.shape, tab_ref[0, h], F32)
    for k in range(1, REL_BUCKETS):
        out = jnp.where(bucket == k, tab_ref[k, h], out)
    return out


def _head_rmsnorm(x, w_row):
    return x * lax.rsqrt(jnp.mean(x * x, axis=-1, keepdims=True) + NORM_EPS) * w_row


def _bias_tiles_kernel(tab_ref, o_ref, *, tq):
    i = lax.broadcasted_iota(jnp.int32, (tq, KEY_CHUNK), 0)
    j = lax.broadcasted_iota(jnp.int32, (tq, KEY_CHUNK), 1)
    buckets = [_bucket_of(i - j + KEY_CHUNK * k) for k in range(2)]
    nheads = o_ref.shape[1] // tq

    def body(h, carry):
        rows = pl.ds(pl.multiple_of(h * tq, SUBLANE), tq)
        for k in range(2):
            o_ref[k, rows, :] = _bias_from_buckets(buckets[k], tab_ref, h)
        o_ref[2, rows, :] = jnp.full((tq, KEY_CHUNK), tab_ref[REL_BUCKETS - 1, h], F32)
        return carry

    lax.fori_loop(0, nheads, body, 0)


def _bias_tiles(table, tq):
    nheads = table.shape[1]
    return pl.pallas_call(
        functools.partial(_bias_tiles_kernel, tq=tq),
        in_specs=[pl.BlockSpec(memory_space=pltpu.SMEM)],
        out_specs=pl.BlockSpec(memory_space=pltpu.VMEM),
        out_shape=jax.ShapeDtypeStruct((3, nheads * tq, KEY_CHUNK), F32),
        name="nsa_bias_tiles",
    )(table)


PAGES_PER_STEP = 8


def _pages_per_step(npages):
    p = PAGES_PER_STEP
    while npages % p:
        p //= 2
    return p


def _slot_spec(src, lead, slot, groups, d, k, pps):
    gd = groups * d
    if src.shape[-2:] == (groups, d):
        nlead = len(lead)
        assert src.ndim == nlead + 5 and src.shape[-4] == KEY_CHUNK
        return pl.BlockSpec((None,) * (nlead + 1) + (KEY_CHUNK, None, groups, d),
                            lambda b, p, t: tuple(lead) + (t[b, p * pps + k], 0, slot, 0, 0))
    assert not lead
    if src.ndim == 2:
        return pl.BlockSpec((KEY_CHUNK, gd), lambda b, p, t: (t[b, p * pps + k], slot))
    assert src.ndim == 3 and src.shape[1] == KEY_CHUNK
    return pl.BlockSpec((None, KEY_CHUNK, gd), lambda b, p, t: (t[b, p * pps + k], 0, slot))


def _slab(ref, g, d):
    return ref[:, g, :] if len(ref.shape) == 3 else ref[:, g * d:(g + 1) * d]


def _compress_ab_kernel(tbl_ref, *refs, groups, d, pps):
    x_refs, (pe_ref, w1_ref, o_ref, slab_ref) = refs[:2 * pps], refs[2 * pps:]
    half = CMP_LEN // 2
    nchunk = KEY_CHUNK // CMP_STRIDE
    for k in range(pps):
        for s in range(2):
            for g in range(groups):
                slab_ref[k, s * groups + g] = _slab(x_refs[2 * k + s], g, d)
    rows_g = pps * nchunk
    for s in range(2):
        acc = [jnp.zeros((groups * rows_g, d), F32) for _ in range(2)]
        for l in range(half):
            x = jnp.concatenate(
                [slab_ref[k, s * groups + g, pl.ds(l, nchunk, stride=CMP_STRIDE), :]
                 for g in range(groups) for k in range(pps)], axis=0)
            for part in range(2):
                ll = part * half + l
                xa = (x + pe_ref[s, ll:ll + 1, :]).astype(BF16)
                acc[part] = acc[part] + jnp.dot(xa, w1_ref[s, ll], preferred_element_type=F32)
        for part in range(2):
            for g in range(groups):
                o_ref[s, part, g] = acc[part][g * rows_g:(g + 1) * rows_g]


def _compress_ab(src, lead, tbl, pe, w1b, groups, d):
    nb, npages = tbl.shape
    pps = _pages_per_step(npages)
    nchunk = KEY_CHUNK // CMP_STRIDE
    assert CMP_LEN == 2 * CMP_STRIDE
    grid_spec = pltpu.PrefetchScalarGridSpec(
        num_scalar_prefetch=1, grid=(nb, npages // pps),
        in_specs=[_slot_spec(src, lead, s, groups, d, k, pps) for k in range(pps) for s in range(2)]
        + [pl.BlockSpec(pe.shape, lambda b, p, t: (0, 0, 0)),
           pl.BlockSpec(w1b.shape, lambda b, p, t: (0, 0, 0, 0))],
        out_specs=pl.BlockSpec((None, 2, 2, groups, pps * nchunk, d), lambda b, p, t: (b, 0, 0, 0, p, 0)),
        scratch_shapes=[pltpu.VMEM((pps, 2 * groups, KEY_CHUNK, d), F32)])
    return pl.pallas_call(
        functools.partial(_compress_ab_kernel, groups=groups, d=d, pps=pps),
        grid_spec=grid_spec,
        out_shape=jax.ShapeDtypeStruct((nb, 2, 2, groups, npages * nchunk, d), F32),
        compiler_params=pltpu.CompilerParams(dimension_semantics=("parallel", "arbitrary"),
                                             vmem_limit_bytes=V7X_VMEM_LIMIT_BYTES),
        name="nsa_compress_ab",
    )(tbl, *([src] * (2 * pps)), pe, w1b)


def _compress_finish_kernel(ab_ref, w2_ref, kw_ref, o_ref, *, groups, d):
    n = ab_ref.shape[3]
    for s in range(2):
        for g in range(groups):
            first = ab_ref[s, 0, g]
            second = pltpu.roll(ab_ref[s, 1, g], n - 1, 0)
            hid = jax.nn.silu(first + second).astype(BF16)
            out = jnp.dot(hid, w2_ref[s], preferred_element_type=F32)
            if s == 0:
                out = _head_rmsnorm(out, kw_ref[...])
            o_ref[s, :, g * d:(g + 1) * d] = out.astype(o_ref.dtype)


def _compress_finish(ab, w2b, k_norm_row, groups, d):
    nb, _, _, _, n, _ = ab.shape
    return pl.pallas_call(
        functools.partial(_compress_finish_kernel, groups=groups, d=d),
        grid=(nb,),
        in_specs=[pl.BlockSpec((None, 2, 2, groups, n, d), lambda b: (b, 0, 0, 0, 0, 0)),
                  pl.BlockSpec(w2b.shape, lambda b: (0, 0, 0)),
                  pl.BlockSpec((1, d), lambda b: (0, 0))],
        out_specs=pl.BlockSpec((None, 2, n, groups * d), lambda b: (b, 0, 0, 0)),
        out_shape=jax.ShapeDtypeStruct((nb, 2, n, groups * d), BF16),
        compiler_params=pltpu.CompilerParams(dimension_semantics=("parallel",)),
        name="nsa_compress_finish",
    )(ab, w2b, k_norm_row)


def _select_kernel(tab_ref, q_ref, kcv_ref, qw_ref, ocmp_ref, msel_ref, *, tq, groups, hpg, d, q_pos0, n_cmp, n_sel,
                   nselp, scale):
    qi = pl.program_id(1)
    pos0 = q_pos0 + qi * tq
    nck = kcv_ref.shape[1]
    t_c = lax.broadcasted_iota(jnp.int32, (tq, nck), 0)
    n_c = lax.broadcasted_iota(jnp.int32, (tq, nck), 1)
    dist = pos0 + t_c - (n_c * CMP_STRIDE + CMP_LEN - 1)
    cmask = (dist >= 0) & (n_c < n_cmp)
    cmaskf = cmask.astype(F32)
    assert REL_MAX_DIST == LANE and nck % LANE == 0
    didx = jnp.clip(dist, 0, REL_MAX_DIST - 1)
    dist_buckets = _bucket_of(lax.broadcasted_iota(jnp.int32, (1, LANE), 1))

    def dist_bias(h):
        by_dist = jnp.broadcast_to(_bias_from_buckets(dist_buckets, tab_ref, h), (tq, LANE))
        return jnp.concatenate([jnp.take_along_axis(by_dist, didx[:, k * LANE:(k + 1) * LANE], axis=1)
                                for k in range(nck // LANE)], axis=1)

    n_r = lax.broadcasted_iota(jnp.int32, (nck, nselp), 0)
    j_r = lax.broadcasted_iota(jnp.int32, (nck, nselp), 1)
    cover = ((n_r * CMP_STRIDE < j_r * SEL_LEN + SEL_LEN) & (n_r * CMP_STRIDE + CMP_LEN > j_r * SEL_LEN)
             & (n_r < n_cmp) & (j_r < n_sel)).astype(BF16)
    j_s = lax.broadcasted_iota(jnp.int32, (tq, nselp), 1)
    pos = pos0 + lax.broadcasted_iota(jnp.int32, (tq, nselp), 0)
    cur = pos // SEL_LEN
    visible = (j_s * SEL_LEN <= pos) & (j_s < n_sel)
    forced = ((j_s == 0) | (j_s == cur) | (j_s == cur - 1)).astype(F32)
    cmask_all = jnp.concatenate([cmask] * hpg, axis=0)
    cmaskf_all = jnp.concatenate([cmaskf] * hpg, axis=0)
    for g in range(groups):
        kc = kcv_ref[0, :, g * d:(g + 1) * d]
        vc = kcv_ref[1, :, g * d:(g + 1) * d]
        heads = range(g * hpg, (g + 1) * hpg)
        qn = jnp.concatenate([_head_rmsnorm(q_ref[:, h * d:(h + 1) * d], qw_ref[...]).astype(BF16) for h in heads],
                             axis=0)
        bias = jnp.concatenate([dist_bias(h) for h in heads], axis=0)
        logits = lax.dot_general(qn, kc, (((1,), (1,)), ((), ())), preferred_element_type=F32) * scale
        logits = jnp.where(cmask_all, logits + bias, NEG_INF)
        m = jnp.max(logits, axis=-1, keepdims=True)
        p = jnp.exp(logits - m)
        p = p / jnp.sum(p, axis=-1, keepdims=True) * cmaskf_all
        o = jnp.dot(p.astype(BF16), vc, preferred_element_type=F32)
        p_grp = jnp.zeros((tq, nck), F32)
        for e, h in enumerate(heads):
            ocmp_ref[:, h * d:(h + 1) * d] = o[e * tq:(e + 1) * tq]
            p_grp = p_grp + p[e * tq:(e + 1) * tq]
        p_sel = jnp.dot(p_grp.astype(BF16), cover, preferred_element_type=F32)
        score = jnp.where(visible, p_sel + FORCE_BONUS * forced, NEG_INF)
        if tq == LANE and nselp == LANE:
            nrow = -(-n_sel // SUBLANE) * SUBLANE
            st = score.T[0:nrow, :]
            jr = lax.broadcasted_iota(jnp.int32, (nrow, tq), 0)
            cnt_t = jnp.zeros((nrow, tq), jnp.int32)
            for jp in range(n_sel):
                rowv = st[jp:jp + 1, :]
                beats = (rowv > st) | ((rowv == st) & (jr > jp))
                cnt_t = cnt_t + beats.astype(jnp.int32)
            top_t = (cnt_t < SEL_TOPK).astype(F32)
            top = jnp.concatenate([top_t, jnp.zeros((nselp - nrow, tq), F32)], axis=0).T
            msel_ref[:, g * nselp:(g + 1) * nselp] = jnp.where(visible, top, 0.0)
        else:
            cnt = jnp.zeros((tq, nselp), jnp.int32)
            for jp in range(n_sel):
                col = score[:, jp:jp + 1]
                beats = (col > score) | ((col == score) & (j_s > jp))
                cnt = cnt + beats.astype(jnp.int32)
            msel_ref[:, g * nselp:(g + 1) * nselp] = ((cnt < SEL_TOPK) & visible).astype(F32)


def _nsa_select(table, qp, row0, kcv, q_norm_row, *, nb, seq, tq, q_pos0, n_cmp, n_sel, groups, hpg, d):
    nq = seq // tq
    n_q = groups * hpg * d
    nselp = -(-n_sel // LANE) * LANE
    nck = kcv.shape[2]
    assert row0 % tq == 0
    blk0 = row0 // tq
    kern = functools.partial(_select_kernel, tq=tq, groups=groups, hpg=hpg, d=d, q_pos0=q_pos0, n_cmp=n_cmp,
                             n_sel=n_sel, nselp=nselp, scale=d ** -0.5)
    return pl.pallas_call(
        kern, grid=(nb, nq),
        in_specs=[pl.BlockSpec(memory_space=pltpu.SMEM),
                  pl.BlockSpec((tq, n_q), lambda b, q: (blk0 + b * nq + q, 0)),
                  pl.BlockSpec((None, 2, nck, groups * d), lambda b, q: (b, 0, 0, 0)),
                  pl.BlockSpec((1, d), lambda b, q: (0, 0))],
        out_specs=[pl.BlockSpec((tq, n_q), lambda b, q: (b * nq + q, 0)),
                   pl.BlockSpec((tq, groups * nselp), lambda b, q: (b * nq + q, 0))],
        out_shape=[jax.ShapeDtypeStruct((nb * seq, n_q), F32),
                   jax.ShapeDtypeStruct((nb * seq, groups * nselp), F32)],
        compiler_params=pltpu.CompilerParams(dimension_semantics=("parallel", "parallel"),
                                             vmem_limit_bytes=V7X_VMEM_LIMIT_BYTES),
        name="nsa_cmp_select",
    )(table, qp, kcv, q_norm_row)


def _kv_pack_kernel(tbl_ref, *refs, groups, d, pps):
    x_refs, (kw_ref, k_ref, v_ref, slab_ref) = refs[:2 * pps], refs[2 * pps:]

    def dense(x_ref, i, g):
        if len(x_ref.shape) == 3:
            slab_ref[i * groups + g] = _slab(x_ref, g, d)
            return slab_ref[i * groups + g]
        return _slab(x_ref, g, d)

    for k in range(pps):
        rows = slice(k * KEY_CHUNK, (k + 1) * KEY_CHUNK)
        for g in range(groups):
            cols = slice(g * d, (g + 1) * d)
            k_ref[rows, cols] = _head_rmsnorm(dense(x_refs[2 * k], 0, g), kw_ref[...]).astype(k_ref.dtype)
            v_ref[rows, cols] = dense(x_refs[2 * k + 1], 1, g).astype(v_ref.dtype)


def _kv_pack(src, lead, tbl, k_slot, k_norm_row, groups, d):
    nb, npages = tbl.shape
    pps = _pages_per_step(npages)
    gd = groups * d
    grid_spec = pltpu.PrefetchScalarGridSpec(
        num_scalar_prefetch=1, grid=(nb, npages // pps),
        in_specs=[_slot_spec(src, lead, k_slot + s, groups, d, k, pps) for k in range(pps) for s in range(2)]
        + [pl.BlockSpec((1, d), lambda b, p, t: (0, 0))],
        out_specs=[pl.BlockSpec((None, pps * KEY_CHUNK, gd), lambda b, p, t: (b, p, 0)),
                   pl.BlockSpec((None, pps * KEY_CHUNK, gd), lambda b, p, t: (b, p, 0))],
        scratch_shapes=[pltpu.VMEM((2 * groups, KEY_CHUNK, d), F32)])
    return pl.pallas_call(
        functools.partial(_kv_pack_kernel, groups=groups, d=d, pps=pps),
        grid_spec=grid_spec,
        out_shape=[jax.ShapeDtypeStruct((nb, npages * KEY_CHUNK, gd), BF16)] * 2,
        compiler_params=pltpu.CompilerParams(dimension_semantics=("parallel", "parallel"),
                                             vmem_limit_bytes=V7X_VMEM_LIMIT_BYTES),
        name="nsa_kv_pack",
    )(tbl, *([src] * (2 * pps)), k_norm_row)


def _attend_kernel(tiles_ref, q_ref, ocmp_ref, gate_ref, msel_ref, ks_ref, vs_ref, kw_ref, vw_ref, qw_ref, o_ref,
                   qn_ref, ms_ref, ls_ref, as_ref, mw_ref, lw_ref, aw_ref, *, tq, groups, hpg, d, q_pos0,
                   win_chunk0, n_win_steps, nselp, scale):
    qi = pl.program_id(1)
    c = pl.program_id(2)
    pos0 = q_pos0 + qi * tq
    p_hi = (pos0 + tq - 1) // KEY_CHUNK // ATT_CHUNKS
    rows_g = hpg * tq
    nheads = groups * hpg

    @pl.when(c == 0)
    def _():
        for h in range(nheads):
            qn_ref[h * tq:(h + 1) * tq, :] = _head_rmsnorm(q_ref[:, h * d:(h + 1) * d], qw_ref[...]).astype(BF16)
        for m_ref, l_ref, a_ref in ((ms_ref, ls_ref, as_ref), (mw_ref, lw_ref, aw_ref)):
            m_ref[...] = jnp.full(m_ref.shape, NEG_INF, F32)
            l_ref[...] = jnp.zeros(l_ref.shape, F32)
            a_ref[...] = jnp.zeros(a_ref.shape, F32)

    assert tq & (tq - 1) == 0 and d == LANE
    t_idx = lax.broadcasted_iota(jnp.int32, (rows_g, KEY_CHUNK), 0) & (tq - 1)
    j_idx = lax.broadcasted_iota(jnp.int32, (rows_g, KEY_CHUNK), 1)
    ones_v = jnp.ones((KEY_CHUNK, d), BF16)

    def step(g, k_ref, v_ref, m_ref, l_ref, a_ref, chunk0, masks):
        rows = slice(g * rows_g, (g + 1) * rows_g)
        q = qn_ref[rows, :]
        logits = []
        for u, mask in enumerate(masks):
            delta = pos0 - (chunk0 + u) * KEY_CHUNK
            tile = jnp.clip(delta // KEY_CHUNK, 0, 2)
            k = k_ref[u * KEY_CHUNK:(u + 1) * KEY_CHUNK, g * d:(g + 1) * d]
            s = lax.dot_general(q, k, (((1,), (1,)), ((), ())), preferred_element_type=F32) * scale
            s = s + tiles_ref[tile, pl.ds(g * rows_g, rows_g), :]
            logits.append(jnp.where(mask, s, NEG_INF))
        m_prev = m_ref[rows, :]
        m_new = m_prev
        for s in logits:
            m_new = jnp.maximum(m_new, jnp.max(s, axis=-1, keepdims=True))
        alpha = jnp.exp(m_prev - m_new)
        pv = None
        for u, (s, mask) in enumerate(zip(logits, masks)):
            p = jnp.where(mask, jnp.exp(s - m_new), 0.0).astype(BF16)
            v_ext = jnp.concatenate([v_ref[u * KEY_CHUNK:(u + 1) * KEY_CHUNK, g * d:(g + 1) * d], ones_v], axis=1)
            part = jnp.dot(p, v_ext, preferred_element_type=F32)
            pv = part if pv is None else pv + part
        l_ref[rows, :] = alpha * l_ref[rows, :] + pv[:, d:]
        a_ref[rows, :] = alpha * a_ref[rows, :] + pv[:, :d]
        m_ref[rows, :] = m_new

    @pl.when(c <= p_hi)
    def _():
        chunk0 = c * ATT_CHUNKS
        blocks_per_chunk = KEY_CHUNK // SEL_LEN
        jr = lax.broadcasted_iota(jnp.int32, (nselp, KEY_CHUNK), 0)
        lr = lax.broadcasted_iota(jnp.int32, (nselp, KEY_CHUNK), 1)
        causal, expand = [], []
        for u in range(ATT_CHUNKS):
            causal.append(pos0 - (chunk0 + u) * KEY_CHUNK + t_idx - j_idx >= 0)
            expand.append((jr == (chunk0 + u) * blocks_per_chunk + lr // SEL_LEN).astype(BF16))
        for g in range(groups):
            ms = msel_ref[:, g * nselp:(g + 1) * nselp].astype(BF16)
            masks = []
            for u in range(ATT_CHUNKS):
                sel = jnp.dot(ms, expand[u], preferred_element_type=F32)
                masks.append(causal[u] & (jnp.concatenate([sel] * hpg, axis=0) > 0.5))
            step(g, ks_ref, vs_ref, ms_ref, ls_ref, as_ref, chunk0, masks)

    wpair = p_hi - c

    @pl.when((c < n_win_steps) & (wpair * ATT_CHUNKS >= win_chunk0))
    def _():
        chunk0 = wpair * ATT_CHUNKS
        masks = []
        for u in range(ATT_CHUNKS):
            dist = pos0 - (chunk0 + u) * KEY_CHUNK + t_idx - j_idx
            masks.append((dist >= 0) & (dist < WINDOW))
        for g in range(groups):
            step(g, kw_ref, vw_ref, mw_ref, lw_ref, aw_ref, chunk0, masks)

    @pl.when(c == pl.num_programs(2) - 1)
    def _():
        gates = jax.nn.sigmoid(gate_ref[...])
        for h in range(nheads):
            r = slice(h * tq, (h + 1) * tq)
            cols = slice(h * d, (h + 1) * d)
            o_sel = as_ref[r, :] / jnp.maximum(ls_ref[r, :], TINY)
            o_win = aw_ref[r, :] / jnp.maximum(lw_ref[r, :], TINY)
            o = (gates[:, 3 * h:3 * h + 1] * ocmp_ref[:, cols] + gates[:, 3 * h + 1:3 * h + 2] * o_sel
                 + gates[:, 3 * h + 2:3 * h + 3] * o_win)
            o_ref[:, cols] = o.astype(o_ref.dtype)


def _nsa_attend(tiles, qp, gp, row0, ocmp, msel, ksel, vsel, kwin, vwin, q_norm_row, *, nb, seq, tq, q_pos0,
                win_chunk0, groups, hpg, d):
    nq = seq // tq
    n_q = groups * hpg * d
    gd = groups * d
    nheads = groups * hpg
    nselp = msel.shape[1] // groups
    pair = ATT_CHUNKS * KEY_CHUNK
    assert ksel.shape[1] % pair == 0 and kwin.shape[1] % pair == 0 and win_chunk0 % ATT_CHUNKS == 0
    npair = ksel.shape[1] // pair
    nwpair = kwin.shape[1] // pair
    wpair0 = win_chunk0 // ATT_CHUNKS
    n_win_steps = (WINDOW // KEY_CHUNK + ATT_CHUNKS - 1) // ATT_CHUNKS + 1
    ncg = max(npair, n_win_steps)
    assert row0 % tq == 0 and q_pos0 % KEY_CHUNK == 0 and (tq == KEY_CHUNK or nq == 1) and tq <= KEY_CHUNK
    blk0 = row0 // tq

    def p_hi(q):
        return (q_pos0 + q * tq + tq - 1) // KEY_CHUNK // ATT_CHUNKS

    def sel_map(b, q, c):
        return (b, jnp.minimum(c, p_hi(q)), 0)

    def win_map(b, q, c):
        return (b, jnp.clip(p_hi(q) - jnp.minimum(c, n_win_steps - 1) - wpair0, 0, nwpair - 1), 0)

    grp_map = lambda b, q, c: (b * nq + q, 0)
    all_map = lambda b, q, c: (blk0 + b * nq + q, 0)
    kern = functools.partial(_attend_kernel, tq=tq, groups=groups, hpg=hpg, d=d, q_pos0=q_pos0,
                             win_chunk0=win_chunk0, n_win_steps=n_win_steps, nselp=nselp, scale=d ** -0.5)
    return pl.pallas_call(
        kern, grid=(nb, nq, ncg),
        in_specs=[pl.BlockSpec(tiles.shape, lambda b, q, c: (0, 0, 0)),
                  pl.BlockSpec((tq, n_q), all_map),
                  pl.BlockSpec((tq, n_q), grp_map),
                  pl.BlockSpec((tq, gp.shape[1]), all_map),
                  pl.BlockSpec((tq, groups * nselp), grp_map),
                  pl.BlockSpec((None, pair, gd), sel_map),
                  pl.BlockSpec((None, pair, gd), sel_map),
                  pl.BlockSpec((None, pair, gd), win_map),
                  pl.BlockSpec((None, pair, gd), win_map),
                  pl.BlockSpec((1, d), lambda b, q, c: (0, 0))],
        out_specs=pl.BlockSpec((tq, n_q), grp_map),
        out_shape=jax.ShapeDtypeStruct((nb * seq, n_q), BF16 if tq % (2 * SUBLANE) == 0 else F32),
        scratch_shapes=[pltpu.VMEM((nheads * tq, d), BF16),
                        pltpu.VMEM((nheads * tq, d), F32), pltpu.VMEM((nheads * tq, d), F32),
                        pltpu.VMEM((nheads * tq, d), F32),
                        pltpu.VMEM((nheads * tq, d), F32), pltpu.VMEM((nheads * tq, d), F32),
                        pltpu.VMEM((nheads * tq, d), F32)],
        compiler_params=pltpu.CompilerParams(dimension_semantics=("parallel", "parallel", "arbitrary"),
                                             vmem_limit_bytes=V7X_VMEM_LIMIT_BYTES),
        name="nsa_attend",
    )(tiles, qp, ocmp, gp, msel, ksel, vsel, kwin, vwin, q_norm_row)


def _nsa_group(qp, kvp, gp, row0, nb, seq, q_pos0, past, win_past, w, groups, hpg, d):
    q_norm_w, k_norm_w, cmp_pe, cmp_w1, cmp_w2, table = w
    gd = groups * d
    tq = KEY_CHUNK if seq % KEY_CHUNK == 0 else seq
    assert tq % SUBLANE == 0
    w1b, w2b = cmp_w1.astype(BF16), cmp_w2.astype(BF16)
    if past is None:
        assert seq % KEY_CHUNK == 0 and row0 % KEY_CHUNK == 0
        npg = seq // KEY_CHUNK
        tbl = row0 // KEY_CHUNK + jnp.arange(nb * npg, dtype=jnp.int32).reshape(nb, npg)
        src = kvp
        ab = _compress_ab(src, (), tbl, cmp_pe, w1b, groups, d)
        ksel, vsel = _kv_pack(src, (), tbl, 2, k_norm_w[1:2], groups, d)
        kwin, vwin = _kv_pack(src, (), tbl, 4, k_norm_w[2:3], groups, d)
        tk = seq
        win_chunk0 = 0
    else:
        cache, lead, tbl = past
        npg = tbl.shape[1]
        past_len = npg * KEY_CHUNK
        assert q_pos0 == past_len and seq < CMP_STRIDE and seq <= KEY_CHUNK
        ab = _compress_ab(cache, lead, tbl, cmp_pe, w1b, groups, d)
        ksel_p, vsel_p = _kv_pack(cache, lead, tbl, 2, k_norm_w[1:2], groups, d)
        wb = win_past.shape[1]
        assert wb % KEY_CHUNK == 0 and wb <= WINDOW and (past_len - wb) % KEY_CHUNK == 0
        nwp = wb // KEY_CHUNK
        wtbl = jnp.arange(nb * nwp, dtype=jnp.int32).reshape(nb, nwp)
        win_pages = win_past.reshape((nb * nwp, KEY_CHUNK) + win_past.shape[2:])
        kwin_p, vwin_p = _kv_pack(win_pages, (), wtbl, 0, k_norm_w[2:3], groups, d)
        new = kvp[row0:row0 + nb * seq].reshape(nb, seq, kvp.shape[1])
        new = jnp.pad(new, ((0, 0), (0, KEY_CHUNK - seq), (0, 0)))
        ntbl = jnp.arange(nb, dtype=jnp.int32).reshape(nb, 1)
        ksel_n, vsel_n = _kv_pack(new, (), ntbl, 2, k_norm_w[1:2], groups, d)
        kwin_n, vwin_n = _kv_pack(new, (), ntbl, 4, k_norm_w[2:3], groups, d)

        def join(past_part, new_part):
            rows = past_part.shape[1] + new_part.shape[1]
            fill = jnp.zeros((nb, -rows % (ATT_CHUNKS * KEY_CHUNK), gd), past_part.dtype)
            return jnp.concatenate([past_part, new_part, fill], axis=1)

        ksel, vsel = join(ksel_p, ksel_n), join(vsel_p, vsel_n)
        kwin, vwin = join(kwin_p, kwin_n), join(vwin_p, vwin_n)
        tk = past_len + seq
        win_chunk0 = (past_len - wb) // KEY_CHUNK
    n_cmp = (tk - CMP_LEN) // CMP_STRIDE + 1
    n_sel = -(-tk // SEL_LEN)
    assert n_cmp <= ab.shape[4] - 1
    kcv = _compress_finish(ab, w2b, k_norm_w[0:1], groups, d)
    ocmp, msel = _nsa_select(table, qp, row0, kcv, q_norm_w[None], nb=nb, seq=seq, tq=tq, q_pos0=q_pos0,
                             n_cmp=n_cmp, n_sel=n_sel, groups=groups, hpg=hpg, d=d)
    tiles = _bias_tiles(table, tq)
    return _nsa_attend(tiles, qp, gp, row0, ocmp, msel, ksel, vsel, kwin, vwin, q_norm_w[None], nb=nb, seq=seq,
                       tq=tq, q_pos0=q_pos0, win_chunk0=win_chunk0, groups=groups, hpg=hpg, d=d)


def kernel(x_prompt, x_sample, state_ssd, state_ssd_conv, state_lru, state_lru_conv, cache_nsa_kv, cache_nsa_win, page_table, ffn_norm, w_ffn_in, w_ffn_out, mix_norm, ssd_w_in, ssd_conv_w, ssd_conv_b, ssd_dt_bias, ssd_a_log, ssd_d, ssd_norm, ssd_w_out, lru_w_in, lru_conv_w, lru_conv_b, lru_w_r, lru_b_r, lru_w_i, lru_b_i, lru_lambda, lru_w_out, nsa_w_in, nsa_q_norm, nsa_k_norm, nsa_cmp_pe, nsa_cmp_w1, nsa_cmp_w2, rel_bias_table, nsa_w_out):
    prm = dict(ffn_norm=ffn_norm, w_ffn_in=w_ffn_in, w_ffn_out=w_ffn_out)
    bp, sp, dm = x_prompt.shape
    bs, ss, _ = x_sample.shape
    mp, ms = bp * sp, bs * ss
    depth = mix_norm.shape[0]
    G = NSA_GROUPS
    E = NSA_HEADS // G
    d_head = dm // NSA_HEADS
    n_pages = page_table.shape[1]
    page = cache_nsa_kv.shape[2]
    past_len = n_pages * page
    assert page == KEY_CHUNK

    def merge(tp, ts):
        return jnp.concatenate([tp.reshape(mp, -1), ts.reshape(ms, -1)], axis=0)

    def tail(buf, rows, prompt, c0, c1):
        n = buf.shape[1]
        nb_, seq, r0 = (bp, sp, 0) if prompt else (bs, ss, mp)
        k = min(n, seq)
        new = jnp.stack([rows[r0 + (b + 1) * seq - k:r0 + (b + 1) * seq, c0:c1] for b in range(nb_)])
        return jnp.concatenate([buf, new.reshape((nb_, k) + buf.shape[2:])], axis=1)[:, -n:]

    x = merge(x_prompt, x_sample)
    outs = {k: [] for k in ('ssd_h_p', 'ssd_h_s', 'ssd_buf_p', 'ssd_buf_s', 'lru_h_p', 'lru_h_s',
                            'lru_buf_p', 'lru_buf_s', 'rows_p', 'rows_s', 'win_p', 'win_s')}
    for i in range(depth):
        kind, j = i % N_MIXERS, i // N_MIXERS
        x = _ffn(x, prm, i, 0)
        hn = _rmsnorm(x, mix_norm[i][None])
        if kind == 0:
            d_inner = ssd_w_out.shape[1]
            conv_dim = ssd_conv_w.shape[2]
            heads = ssd_dt_bias.shape[1]
            z = _matmul(hn, ssd_w_in, (j,), kdim=dm, n0=0, ndim=d_inner, tn=512, name="ssd_in_z")
            xbc = _matmul(hn, ssd_w_in, (j,), kdim=dm, n0=d_inner, ndim=conv_dim, tn=512, name="ssd_in_xbc")
            dt_raw = _matmul(hn, ssd_w_in, (j,), kdim=dm, n0=d_inner + conv_dim, ndim=heads, tn=heads,
                             name="ssd_in_dt")
            w = (ssd_conv_w[j], ssd_conv_b[j], ssd_dt_bias[j], ssd_a_log[j], ssd_d[j], ssd_norm[j])
            q = SSD_CHUNK
            assert sp % q == 0
            nbuf = ssd_conv_w.shape[1] - 1
            hd, nstate = d_inner // heads, state_ssd.shape[-1]
            shape = (SSD_GROUPS, nstate, hd)
            zero_buf = jnp.zeros((bp, nbuf, conv_dim), F32)
            zero_h = jnp.zeros((bp,) + state_ssd.shape[2:], F32)
            y_p, st_p = _ssd_mix(xbc, z, dt_raw, 0, bp, sp // q, q, zero_buf, zero_h, *w, *shape)
            y_s, st_s = _ssd_mix(xbc, z, dt_raw, mp, bs, 1, ss, state_ssd_conv[j], state_ssd[j], *w, *shape)
            outs['ssd_h_p'].append(st_p); outs['ssd_h_s'].append(st_s)
            outs['ssd_buf_p'].append(tail(zero_buf, xbc, True, 0, conv_dim))
            outs['ssd_buf_s'].append(tail(state_ssd_conv[j], xbc, False, 0, conv_dim))
            x = _out_proj(jnp.concatenate([y_p, y_s.astype(BF16)], axis=0), ssd_w_out, (j,), x, 1.0, "ssd_out")
        elif kind == 1:
            width = lru_w_out.shape[1]
            nbuf = lru_conv_w.shape[1] - 1
            gx = _matmul(hn, lru_w_in, (j,), kdim=dm, n0=0, ndim=2 * width, tn=512, name="lru_in")
            w = (lru_conv_w[j], lru_conv_b[j], lru_w_r[j], lru_b_r[j], lru_w_i[j], lru_b_i[j], lru_lambda[j])
            zero_buf = jnp.zeros((bp, nbuf, width), F32)
            y_p, st_p = _lru_mix(gx, 0, bp, sp, zero_buf, jnp.zeros((bp, width), F32), *w)
            y_s, st_s = _lru_mix(gx, mp, bs, ss, state_lru_conv[j], state_lru[j], *w)
            outs['lru_h_p'].append(st_p); outs['lru_h_s'].append(st_s)
            outs['lru_buf_p'].append(tail(zero_buf, gx, True, width, 2 * width))
            outs['lru_buf_s'].append(tail(state_lru_conv[j], gx, False, width, 2 * width))
            x = _out_proj(jnp.concatenate([y_p, y_s], axis=0).astype(BF16), lru_w_out, (j,), x, 1.0, "lru_out")
        else:
            n_q, n_rows, n_kv = NSA_HEADS * d_head, 4 * G * d_head, 6 * G * d_head
            qp = _matmul(hn, nsa_w_in, (j,), kdim=dm, n0=0, ndim=n_q, tn=512, name="nsa_in_q")
            kvp = _matmul(hn, nsa_w_in, (j,), kdim=dm, n0=n_q, ndim=n_kv, tn=512, name="nsa_in_kv")
            n_gate = nsa_w_in.shape[2] - n_q - n_kv
            w_gate = jnp.pad(nsa_w_in[j][None, :, n_q + n_kv:], ((0, 0), (0, 0), (0, LANE - n_gate)))
            gp = _matmul(hn, w_gate, (0,), kdim=dm, n0=0, ndim=LANE, tn=LANE, name="nsa_in_gate")[:, :n_gate]
            w = (nsa_q_norm[j], nsa_k_norm[j], nsa_cmp_pe[j], nsa_cmp_w1[j], nsa_cmp_w2[j], rel_bias_table)
            o_p = _nsa_group(qp, kvp, gp, 0, bp, sp, 0, None, None, w, G, E, d_head)
            o_s = _nsa_group(qp, kvp, gp, mp, bs, ss, past_len, (cache_nsa_kv, (j,), page_table), cache_nsa_win[j],
                             w, G, E, d_head)
            outs['rows_p'].append(kvp[:mp, :n_rows].reshape(bp, sp, 4, G, d_head))
            outs['rows_s'].append(kvp[mp:, :n_rows].reshape(bs, ss, 4, G, d_head))
            outs['win_p'].append(tail(jnp.zeros((bp,) + cache_nsa_win.shape[2:], F32), kvp, True, n_rows, n_kv))
            outs['win_s'].append(tail(cache_nsa_win[j], kvp, False, n_rows, n_kv))
            x = _out_proj(jnp.concatenate([o_p.astype(BF16), o_s.astype(BF16)], axis=0), nsa_w_out, (j,), x, 1.0,
                          "nsa_out")
        x = _ffn(x, prm, i, 1)
    y_prompt, y_sample = x[:mp].reshape(bp, sp, dm), x[mp:].reshape(bs, ss, dm)
    st = {k: jnp.stack(v) for k, v in outs.items()}
    return (y_prompt, y_sample, st['ssd_h_p'], st['ssd_h_s'], st['ssd_buf_p'], st['ssd_buf_s'],
            st['lru_h_p'], st['lru_h_s'], st['lru_buf_p'], st['lru_buf_s'],
            st['rows_p'], st['rows_s'], st['win_p'], st['win_s'])
```
